```python
import jax, jax.numpy as jnp
from jax import lax

D_MODEL = 4096
BATCH = 2
SEQ = 4096
DEPTH = 2

CTX_LEN = 256
GRID_W = 64
HEAD_DIM = 128
NA_HEADS = 12
NA_WIN_H = 8
NA_WIN_W = 16
GQ_HEADS = 16
GQ_KV_HEADS = 4
Q_BLOCK = 128
ROPE_THETA = 10000.0
LRU_WIDTH = 1536
LRU_BLOCKS = 12
CONV_W = 4
LRU_C = 8.0
N_BRANCH = 3
D_FF = 11008
N_EXPERTS = 8
TOP_K = 2
D_FF_EXPERT = 3072
N_DENSE = (DEPTH + 1) // 2
N_MOE = DEPTH // 2
ALPHA = (2 * DEPTH) ** 0.25
BETA = (8 * DEPTH) ** -0.25
EPS = 1e-6
NEG_INF = -1e30

NA_WIDTH = NA_HEADS * HEAD_DIM
GQ_Q_WIDTH = GQ_HEADS * HEAD_DIM
GQ_KV_WIDTH = GQ_KV_HEADS * HEAD_DIM
LRU_BLOCK = LRU_WIDTH // LRU_BLOCKS
IN_SPLITS = (NA_WIDTH, NA_WIDTH, NA_WIDTH, GQ_Q_WIDTH, GQ_KV_WIDTH, GQ_KV_WIDTH, LRU_WIDTH, LRU_WIDTH, N_BRANCH * D_MODEL)
IN_COLS = sum(IN_SPLITS)

kernel_name = 'hybrid_natten_gqa_rglru_moe_dit'


def layer_norm(x, g, b):
    xf = x.astype(jnp.float32)
    mu = jnp.mean(xf, axis=-1, keepdims=True)
    var = jnp.mean(jnp.square(xf - mu), axis=-1, keepdims=True)
    return ((xf - mu) * lax.rsqrt(var + EPS) * g.astype(jnp.float32) + b.astype(jnp.float32)).astype(x.dtype)


def rms_norm(x, g):
    xf = x.astype(jnp.float32)
    return (xf * lax.rsqrt(jnp.mean(jnp.square(xf), axis=-1, keepdims=True) + EPS) * g.astype(jnp.float32)).astype(x.dtype)


def split_cols(p):
    parts = []
    start = 0
    for width in IN_SPLITS:
        parts.append(p[..., start:start + width])
        start += width
    return parts


def heads(t, n):
    return t.reshape(t.shape[0], t.shape[1], n, HEAD_DIM)


def rope_1d(x, pos):
    half = x.shape[-1] // 2
    inv = ROPE_THETA ** (-jnp.arange(half, dtype=jnp.float32) / half)
    ang = pos.astype(jnp.float32)[:, None] * inv[None, :]
    cos = jnp.cos(ang)[:, None, :]
    sin = jnp.sin(ang)[:, None, :]
    xf = x.astype(jnp.float32)
    x1, x2 = xf[..., :half], xf[..., half:]
    return jnp.concatenate([x1 * cos - x2 * sin, x2 * cos + x1 * sin], axis=-1).astype(x.dtype)


def rope_2d(x):
    t = jnp.arange(x.shape[1])
    hd2 = HEAD_DIM // 2
    return jnp.concatenate([rope_1d(x[..., :hd2], t // GRID_W), rope_1d(x[..., hd2:], t % GRID_W)], axis=-1)


def attend(q, k, v):
    B, L, H, d = q.shape
    KV = k.shape[2]
    qg = q.reshape(B, L, KV, H // KV, d)
    s = jnp.einsum('blkgd,bmkd->bkglm', qg, k).astype(jnp.float32) * (d ** -0.5)
    p = jax.nn.softmax(s, axis=-1).astype(v.dtype)
    return jnp.einsum('bkglm,bmkd->blkgd', p, v).reshape(B, L, H * d)


def blocked_attention(q, k_all, v_all):
    B, L, H, d = q.shape
    nblk = L // Q_BLOCK
    qb = jnp.swapaxes(q.reshape(B, nblk, Q_BLOCK, H, d), 0, 1)
    out = lax.map(lambda qq: attend(qq, k_all, v_all), qb)
    return jnp.swapaxes(out, 0, 1).reshape(B, L, H * d)


def natten_latent(q, k, v, k_ctx, v_ctx, rpb):
    B, S, H, d = q.shape
    rows = S // GRID_W
    wh = min(NA_WIN_H, rows)
    r = jnp.arange(rows)
    rs = jnp.clip(r - wh // 2, 0, rows - wh)
    row_idx = rs[:, None] + jnp.arange(wh)[None, :]
    roff = row_idx - r[:, None] + (NA_WIN_H - 1)
    cols = jnp.arange(GRID_W)
    cs = jnp.clip(cols - NA_WIN_W // 2, 0, GRID_W - NA_WIN_W)
    valid = (cols[None, :] >= cs[:, None]) & (cols[None, :] < cs[:, None] + NA_WIN_W)
    coff = jnp.clip(cols[None, :] - cols[:, None], -(NA_WIN_W - 1), NA_WIN_W - 1) + (NA_WIN_W - 1)
    bias = rpb.astype(jnp.float32)[:, roff[:, None, :, None], coff[None, :, None, :]]
    qg = q.reshape(B, rows, GRID_W, H, d)
    kg = k.reshape(B, rows, GRID_W, H, d)[:, row_idx]
    vg = v.reshape(B, rows, GRID_W, H, d)[:, row_idx]
    scale = d ** -0.5
    s_lat = jnp.einsum('brqhd,brwkhd->bhrqwk', qg, kg).astype(jnp.float32) * scale + bias[None]
    s_lat = jnp.where(valid[:, None, :], s_lat, NEG_INF).reshape(B, H, rows, GRID_W, wh * GRID_W)
    s_ctx = jnp.einsum('brqhd,bchd->bhrqc', qg, k_ctx).astype(jnp.float32) * scale
    p = jax.nn.softmax(jnp.concatenate([s_lat, s_ctx], axis=-1), axis=-1).astype(v.dtype)
    n_lat = wh * GRID_W
    p_lat = p[..., :n_lat].reshape(B, H, rows, GRID_W, wh, GRID_W)
    out = jnp.einsum('bhrqwk,brwkhd->brqhd', p_lat, vg) + jnp.einsum('bhrqc,bchd->brqhd', p[..., n_lat:], v_ctx)
    return out.reshape(B, S, H * d)


def centred_conv(x, w, b):
    L = x.shape[1]
    xp = jnp.pad(x, ((0, 0), (CONV_W // 2, CONV_W - 1 - CONV_W // 2), (0, 0)))
    out = b
    for tap in range(CONV_W):
        out = out + w[tap] * xp[:, tap:tap + L]
    return out


def rglru_coeffs(xc, w_r, b_r, w_i, b_i, lam):
    B, L, W = xc.shape
    xb = xc.reshape(B, L, LRU_BLOCKS, LRU_BLOCK)
    gate_r = jax.nn.sigmoid(jnp.einsum('blgi,gij->blgj', xb, w_r).reshape(B, L, W) + b_r).astype(jnp.float32)
    gate_i = jax.nn.sigmoid(jnp.einsum('blgi,gij->blgj', xb, w_i).reshape(B, L, W) + b_i).astype(jnp.float32)
    log_a = -LRU_C * gate_r * jax.nn.softplus(-lam.astype(jnp.float32))
    a = jnp.exp(log_a)
    b = jnp.sqrt(-jnp.expm1(2.0 * log_a)) * (gate_i * xc.astype(jnp.float32))
    return a, b


def scan_combine(e1, e2):
    a1, b1 = e1
    a2, b2 = e2
    return a1 * a2, a2 * b1 + b2


def linear_scan(a, b, h0):
    b = b.at[:, 0].add(a[:, 0] * h0)
    _, h = lax.associative_scan(scan_combine, (a, b), axis=1)
    return h


def bidirectional_rglru(xc_x, xc_c, w_r, b_r, w_i, b_i, lam):
    B = xc_x.shape[0]
    h0 = jnp.zeros((B, LRU_WIDTH), jnp.float32)
    a_c, b_c = rglru_coeffs(xc_c, w_r[0], b_r[0], w_i[0], b_i[0], lam[0])
    a_x, b_x = rglru_coeffs(xc_x, w_r[0], b_r[0], w_i[0], b_i[0], lam[0])
    hc_f = linear_scan(a_c, b_c, h0)
    hx_f = linear_scan(a_x, b_x, hc_f[:, -1])
    a_c, b_c = rglru_coeffs(xc_c, w_r[1], b_r[1], w_i[1], b_i[1], lam[1])
    a_x, b_x = rglru_coeffs(xc_x, w_r[1], b_r[1], w_i[1], b_i[1], lam[1])
    hc_b = linear_scan(jnp.flip(a_c, 1), jnp.flip(b_c, 1), h0)
    hx_b = linear_scan(jnp.flip(a_x, 1), jnp.flip(b_x, 1), hc_b[:, -1])
    return hx_f + jnp.flip(hx_b, 1), hc_f + jnp.flip(hc_b, 1)


def merge_branches(ya, yb, yc, gates, w_br_na, w_br_gq, w_br_lru, w_o):
    g = jax.nn.sigmoid(gates.reshape(gates.shape[:-1] + (N_BRANCH, D_MODEL)))
    merged = g[..., 0, :] * (ya @ w_br_na) + g[..., 1, :] * (yb @ w_br_gq) + g[..., 2, :] * (yc @ w_br_lru)
    return merged @ w_o


def token_mixers(u_x, u_c, need_ctx, w_in, rpb, q_gain, k_gain, conv_w, conv_b, w_r, b_r, w_i, b_i, lam,
                 w_br_na, w_br_gq, w_br_lru, w_o):
    px = split_cols(u_x @ w_in)
    pc = split_cols(u_c @ w_in)
    na_q, na_k, na_v = heads(px[0], NA_HEADS), heads(px[1], NA_HEADS), heads(px[2], NA_HEADS)
    cna_q, cna_k, cna_v = heads(pc[0], NA_HEADS), heads(pc[1], NA_HEADS), heads(pc[2], NA_HEADS)
    ya_x = natten_latent(na_q, na_k, na_v, cna_k, cna_v, rpb)
    gq_q = rope_2d(rms_norm(heads(px[3], GQ_HEADS), q_gain))
    gq_k = rope_2d(rms_norm(heads(px[4], GQ_KV_HEADS), k_gain))
    gq_v = heads(px[5], GQ_KV_HEADS)
    cgq_q = rms_norm(heads(pc[3], GQ_HEADS), q_gain)
    cgq_k = rms_norm(heads(pc[4], GQ_KV_HEADS), k_gain)
    cgq_v = heads(pc[5], GQ_KV_HEADS)
    k_all = jnp.concatenate([gq_k, cgq_k], axis=1)
    v_all = jnp.concatenate([gq_v, cgq_v], axis=1)
    yb_x = blocked_attention(gq_q, k_all, v_all)
    xc_x = centred_conv(px[6], conv_w, conv_b)
    xc_c = centred_conv(pc[6], conv_w, conv_b)
    h_x, h_c = bidirectional_rglru(xc_x, xc_c, w_r, b_r, w_i, b_i, lam)
    yc_x = h_x.astype(u_x.dtype) * jax.nn.gelu(px[7])
    y_x = merge_branches(ya_x, yb_x, yc_x, px[8], w_br_na, w_br_gq, w_br_lru, w_o)
    if not need_ctx:
        return y_x, None
    ya_c = attend(cna_q, cna_k, cna_v)
    yb_c = attend(cgq_q, cgq_k, cgq_v)
    yc_c = h_c.astype(u_c.dtype) * jax.nn.gelu(pc[7])
    y_c = merge_branches(ya_c, yb_c, yc_c, pc[8], w_br_na, w_br_gq, w_br_lru, w_o)
    return y_x, y_c


def swiglu(u, w1, w3, w2):
    return (jax.nn.silu(u @ w1) * (u @ w3)) @ w2


def moe_swiglu(u, router, w1, w3, w2):
    logits = (u @ router).astype(jnp.float32)
    top_v, top_i = lax.top_k(logits, TOP_K)
    top_w = jax.nn.softmax(top_v, axis=-1)
    gates = jnp.sum(jax.nn.one_hot(top_i, N_EXPERTS, dtype=jnp.float32) * top_w[..., None], axis=-2).astype(u.dtype)
    out = jnp.zeros_like(u)
    for e in range(N_EXPERTS):
        out = out + gates[..., e:e + 1] * swiglu(u, w1[e], w3[e], w2[e])
    return out


def setup_inputs(seed: int = 0) -> dict:
    key = jax.random.key(seed)
    ks = jax.random.split(key, 32)

    def nrm(k, shape, scale):
        return jax.random.normal(k, shape, jnp.float32) * scale

    u = jax.random.uniform(ks[17], (DEPTH, 2, LRU_WIDTH), jnp.float32, minval=0.9, maxval=0.999)
    a0 = u ** (1.0 / LRU_C)
    return {
        'x': nrm(ks[0], (BATCH, SEQ, D_MODEL), 1.0),
        'c': nrm(ks[1], (BATCH, D_MODEL), 1.0),
        'ctx': nrm(ks[2], (BATCH, CTX_LEN, D_MODEL), 1.0),
        'c_ctx': nrm(ks[3], (D_MODEL,), 1.0),
        'w_ada': nrm(ks[4], (DEPTH, D_MODEL, 6 * D_MODEL), 0.5 * D_MODEL ** -0.5),
        'b_ada': nrm(ks[5], (DEPTH, 6 * D_MODEL), 0.01),
        'w_in': nrm(ks[6], (DEPTH, D_MODEL, IN_COLS), D_MODEL ** -0.5),
        'na_rpb': nrm(ks[7], (DEPTH, NA_HEADS, 2 * NA_WIN_H - 1, 2 * NA_WIN_W - 1), 0.2),
        'gq_q_gain': 1.0 + nrm(ks[8], (DEPTH, HEAD_DIM), 0.05),
        'gq_k_gain': 1.0 + nrm(ks[9], (DEPTH, HEAD_DIM), 0.05),
        'lru_conv_w': nrm(ks[10], (DEPTH, CONV_W, LRU_WIDTH), CONV_W ** -0.5),
        'lru_conv_b': nrm(ks[11], (DEPTH, LRU_WIDTH), 0.01),
        'lru_w_r': nrm(ks[12], (DEPTH, 2, LRU_BLOCKS, LRU_BLOCK, LRU_BLOCK), LRU_BLOCK ** -0.5),
        'lru_b_r': nrm(ks[13], (DEPTH, 2, LRU_WIDTH), 0.01),
        'lru_w_i': nrm(ks[14], (DEPTH, 2, LRU_BLOCKS, LRU_BLOCK, LRU_BLOCK), LRU_BLOCK ** -0.5),
        'lru_b_i': nrm(ks[15], (DEPTH, 2, LRU_WIDTH), 0.01),
        'lru_lambda': jnp.log(a0) - jnp.log1p(-a0),
        'w_br_na': nrm(ks[18], (DEPTH, NA_WIDTH, D_MODEL), NA_WIDTH ** -0.5),
        'w_br_gq': nrm(ks[19], (DEPTH, GQ_Q_WIDTH, D_MODEL), GQ_Q_WIDTH ** -0.5),
        'w_br_lru': nrm(ks[20], (DEPTH, LRU_WIDTH, D_MODEL), LRU_WIDTH ** -0.5),
        'w_o': nrm(ks[21], (DEPTH, D_MODEL, D_MODEL), BETA * D_MODEL ** -0.5),
        'ln1_g': 1.0 + nrm(ks[22], (DEPTH, D_MODEL), 0.05),
        'ln1_b': nrm(ks[23], (DEPTH, D_MODEL), 0.01),
        'ln2_g': 1.0 + nrm(ks[24], (DEPTH, D_MODEL), 0.05),
        'ln2_b': nrm(ks[25], (DEPTH, D_MODEL), 0.01),
        'ffn_w1': nrm(ks[26], (N_DENSE, D_MODEL, D_FF), D_MODEL ** -0.5),
        'ffn_w3': nrm(ks[27], (N_DENSE, D_MODEL, D_FF), D_MODEL ** -0.5),
        'ffn_w2': nrm(ks[28], (N_DENSE, D_FF, D_MODEL), BETA * D_FF ** -0.5),
        'moe_router': nrm(ks[29], (N_MOE, D_MODEL, N_EXPERTS), D_MODEL ** -0.5),
        'moe_w1': nrm(ks[30], (N_MOE, N_EXPERTS, D_MODEL, D_FF_EXPERT), D_MODEL ** -0.5),
        'moe_w3': nrm(ks[31], (N_MOE, N_EXPERTS, D_MODEL, D_FF_EXPERT), D_MODEL ** -0.5),
        'moe_w2': nrm(ks[16], (N_MOE, N_EXPERTS, D_FF_EXPERT, D_MODEL), BETA * D_FF_EXPERT ** -0.5),
    }


def reference(x, c, ctx, c_ctx, w_ada, b_ada, w_in, na_rpb, gq_q_gain, gq_k_gain, lru_conv_w, lru_conv_b,
              lru_w_r, lru_b_r, lru_w_i, lru_b_i, lru_lambda, w_br_na, w_br_gq, w_br_lru, w_o,
              ln1_g, ln1_b, ln2_g, ln2_b, ffn_w1, ffn_w3, ffn_w2, moe_router, moe_w1, moe_w3, moe_w2):
    B = x.shape[0]
    silu_c = jax.nn.silu(c)
    silu_cc = jax.nn.silu(c_ctx)
    for l in range(DEPTH):
        need_ctx = l < DEPTH - 1
        mod_x = (silu_c @ w_ada[l] + b_ada[l]).reshape(B, 6, 1, D_MODEL)
        mod_c = (silu_cc @ w_ada[l] + b_ada[l]).reshape(6, 1, D_MODEL)
        u_x = x * (1.0 + mod_x[:, 1]) + mod_x[:, 0]
        u_c = ctx * (1.0 + mod_c[1]) + mod_c[0]
        y_x, y_c = token_mixers(u_x, u_c, need_ctx, w_in[l], na_rpb[l], gq_q_gain[l], gq_k_gain[l],
                                lru_conv_w[l], lru_conv_b[l], lru_w_r[l], lru_b_r[l], lru_w_i[l], lru_b_i[l],
                                lru_lambda[l], w_br_na[l], w_br_gq[l], w_br_lru[l], w_o[l])
        x = layer_norm(ALPHA * x + mod_x[:, 2] * y_x, ln1_g[l], ln1_b[l])
        u2_x = x * (1.0 + mod_x[:, 4]) + mod_x[:, 3]
        if l % 2 == 0:
            f_x = swiglu(u2_x, ffn_w1[l // 2], ffn_w3[l // 2], ffn_w2[l // 2])
        else:
            f_x = moe_swiglu(u2_x, moe_router[l // 2], moe_w1[l // 2], moe_w3[l // 2], moe_w2[l // 2])
        x = layer_norm(ALPHA * x + mod_x[:, 5] * f_x, ln2_g[l], ln2_b[l])
        if need_ctx:
            ctx = layer_norm(ALPHA * ctx + mod_c[2] * y_c, ln1_g[l], ln1_b[l])
            u2_c = ctx * (1.0 + mod_c[4]) + mod_c[3]
            if l % 2 == 0:
                f_c = swiglu(u2_c, ffn_w1[l // 2], ffn_w3[l // 2], ffn_w2[l // 2])
            else:
                f_c = moe_swiglu(u2_c, moe_router[l // 2], moe_w1[l // 2], moe_w3[l // 2], moe_w2[l // 2])
            ctx = layer_norm(ALPHA * ctx + mod_c[5] * f_c, ln2_g[l], ln2_b[l])
    return x
```

```python
import collections
import functools

import numpy as np
import jax
import jax.numpy as jnp
from jax import lax
from jax.experimental import pallas as pl
from jax.experimental.pallas import tpu as pltpu

F32 = jnp.float32
BF16 = jnp.bfloat16

V7X_LANES = 128
V7X_SUBLANES = 8
V7X_VMEM_BYTES = 64 * 1024 * 1024
VMEM_CAP_BYTES = V7X_VMEM_BYTES - 6 * 1024 * 1024

Cfg = collections.namedtuple(
    "Cfg",
    "d_model batch seq depth ctx_len grid_w head_dim na_heads na_win_h na_win_w gq_heads gq_kv "
    "rope_theta lru_width lru_blocks conv_w lru_c d_ff n_experts d_ff_expert eps neg_inf "
    "row_tile mm_bm mm_bn glu_bn down_bm na_qrows gq_tq gq_kc lru_cw lru_chunk")

FULL_CFG = Cfg(
    d_model=4096, batch=2, seq=4096, depth=2, ctx_len=256, grid_w=64, head_dim=128,
    na_heads=12, na_win_h=8, na_win_w=16, gq_heads=16, gq_kv=4, rope_theta=10000.0,
    lru_width=1536, lru_blocks=12, conv_w=4, lru_c=8.0, d_ff=11008, n_experts=8,
    d_ff_expert=3072, eps=1e-6, neg_inf=-1e30,
    row_tile=256, mm_bm=512, mm_bn=512, glu_bn=256, down_bm=256, na_qrows=4, gq_tq=256, gq_kc=1024,
    lru_cw=256, lru_chunk=512)


def _splits(cfg):
    na_w = cfg.na_heads * cfg.head_dim
    gq_q = cfg.gq_heads * cfg.head_dim
    gq_kv = cfg.gq_kv * cfg.head_dim
    widths = (na_w, na_w, na_w, gq_q, gq_kv, gq_kv, cfg.lru_width, cfg.lru_width, 3 * cfg.d_model)
    offs = tuple(int(v) for v in np.cumsum((0,) + widths[:-1]))
    return offs, widths


def _exact_div(a, b):
    assert a % b == 0, (a, b)
    return a // b


def _params(vmem_bytes, n_axes):
    limit = int(min(VMEM_CAP_BYTES, max(vmem_bytes * 5 // 4 + (4 << 20), 16 << 20)))
    return pltpu.CompilerParams(dimension_semantics=("arbitrary",) * n_axes, vmem_limit_bytes=limit)


def _mm_kernel(*refs, n_w, cast_w, glu, has_gate, has_add):
    it = iter(refs)
    x_ref = next(it)
    w_refs = [next(it) for _ in range(n_w)]
    g_ref = next(it) if has_gate else None
    a_ref = next(it) if has_add else None
    o_ref = next(it)
    wb_refs = [next(it) for _ in range(n_w)] if cast_w else w_refs

    if cast_w:
        @pl.when(pl.program_id(2) == 0)
        def _():
            for w_ref, wb_ref in zip(w_refs, wb_refs):
                wb_ref[...] = w_ref[...].astype(BF16)

    x = x_ref[...]
    acc = jnp.dot(x, wb_refs[0][...], preferred_element_type=F32)
    if glu:
        acc = jax.nn.silu(acc) * jnp.dot(x, wb_refs[1][...], preferred_element_type=F32)
    if has_gate:
        lane = lax.broadcasted_iota(jnp.int32, g_ref.shape, 1)
        col = jnp.sum(jnp.where(lane == pl.program_id(0), g_ref[...], 0.0), axis=1, keepdims=True)
        acc = acc * col
    if has_add:
        acc = acc + a_ref[...].astype(F32)
    o_ref[...] = acc.astype(o_ref.dtype)


def _matmul(x, ws, w_lead, *, k_blk=0, n_off=0, n_total=None, bm, bn, out_dtype, glu=False,
            gates=None, add=None, out_cols=None, out_col_off=0, n_experts=1, name="mm"):
    M = x.shape[0]
    K = ws[0].shape[-2]
    n_total = ws[0].shape[-1] if n_total is None else n_total
    nj = _exact_div(n_total, bn)
    ni = _exact_div(M, bm)
    out_cols = n_experts * n_total if out_cols is None else out_cols
    cast_w = ws[0].dtype != BF16
    multi = n_experts > 1
    n_lead = len(w_lead) + (1 if multi else 0)
    nob = _exact_div(n_off, bn)
    ocb = _exact_div(out_col_off, bn)

    def w_map(e, j, i):
        return tuple(w_lead) + ((e,) if multi else ()) + (0, nob + j)

    def out_map(e, j, i):
        return (i, ocb + e * nj + j)

    in_specs = [pl.BlockSpec((bm, K), lambda e, j, i: (i, k_blk))]
    in_specs += [pl.BlockSpec((None,) * n_lead + (K, bn), w_map) for _ in ws]
    args = [x] + list(ws)
    if gates is not None:
        in_specs.append(pl.BlockSpec((bm, V7X_LANES), lambda e, j, i: (i, 0)))
        args.append(gates)
    aliases = {}
    if add is not None:
        in_specs.append(pl.BlockSpec((bm, bn), out_map))
        args.append(add)
        if add.dtype == jnp.dtype(out_dtype) and add.shape == (M, out_cols):
            aliases = {len(args) - 1: 0}
    scratch = [pltpu.VMEM((K, bn), BF16) for _ in ws] if cast_w else []
    wbytes = ws[0].dtype.itemsize
    vmem = (2 * bm * K * 2 + len(ws) * (2 * K * bn * wbytes + (K * bn * 2 if cast_w else 0))
            + 2 * bm * bn * jnp.dtype(out_dtype).itemsize + (2 * bm * bn * 4 if add is not None else 0)
            + 3 * bm * bn * 4)
    kern = functools.partial(_mm_kernel, n_w=len(ws), cast_w=cast_w, glu=glu,
                             has_gate=gates is not None, has_add=add is not None)
    return pl.pallas_call(
        kern,
        out_shape=jax.ShapeDtypeStruct((M, out_cols), out_dtype),
        grid=(n_experts, nj, ni),
        in_specs=in_specs,
        out_specs=pl.BlockSpec((bm, bn), out_map),
        scratch_shapes=scratch,
        input_output_aliases=aliases,
        compiler_params=_params(vmem, 3),
        name=name,
    )(*args)


def _ada_kernel(c_ref, w_ref, b_ref, o_ref):
    c = c_ref[...]
    s = (c * jax.nn.sigmoid(c)).astype(BF16)
    acc = jnp.dot(s, w_ref[...].astype(BF16), preferred_element_type=F32)
    o_ref[...] = acc + b_ref[...]


def _ada_table(cond, w_ada, b_ada3, l, bn=1024):
    R, D = cond.shape
    N = w_ada.shape[-1]
    vmem = 2 * D * bn * 4 + D * bn * 2 + 4 * R * bn * 4 + 2 * R * D * 4
    return pl.pallas_call(
        _ada_kernel,
        out_shape=jax.ShapeDtypeStruct((R, N), F32),
        grid=(_exact_div(N, bn),),
        in_specs=[pl.BlockSpec((R, D), lambda j: (0, 0)),
                  pl.BlockSpec((None, D, bn), lambda j: (l, 0, j)),
                  pl.BlockSpec((None, 1, bn), lambda j: (l, 0, j))],
        out_specs=pl.BlockSpec((R, bn), lambda j: (0, j)),
        compiler_params=_params(vmem, 1),
        name="ada_table",
    )(cond, w_ada, b_ada3)


def _mod_sel(cfg):
    tiles_per_batch = _exact_div(cfg.ctx_len + cfg.seq, cfg.row_tile)
    ctx_tiles = _exact_div(cfg.ctx_len, cfg.row_tile)

    def sel(i):
        return jnp.where(i % tiles_per_batch < ctx_tiles, cfg.batch, i // tiles_per_batch)
    return sel


def _modulate_kernel(x_ref, m_ref, u_ref, *, shift_row, scale_row):
    m = m_ref[...]
    u = x_ref[...] * (1.0 + m[scale_row:scale_row + 1, :]) + m[shift_row:shift_row + 1, :]
    u_ref[...] = u.astype(u_ref.dtype)


def _modulate(cfg, x_all, modtab, shift_row, scale_row):
    M, D = x_all.shape
    rt = cfg.row_tile
    sel = _mod_sel(cfg)
    vmem = 2 * rt * D * 4 + 2 * rt * D * 2 + 2 * 8 * D * 4 + 2 * rt * D * 4
    return pl.pallas_call(
        functools.partial(_modulate_kernel, shift_row=shift_row, scale_row=scale_row),
        out_shape=jax.ShapeDtypeStruct((M, D), BF16),
        grid=(_exact_div(M, rt),),
        in_specs=[pl.BlockSpec((rt, D), lambda i: (i, 0)),
                  pl.BlockSpec((None, 6, D), lambda i: (sel(i), 0, 0))],
        out_specs=pl.BlockSpec((rt, D), lambda i: (i, 0)),
        compiler_params=_params(vmem, 1),
        name="modulate",
    )(x_all, modtab)


def _ln_kernel(*refs, alpha, eps, gate_row, l, next_rows, n_experts):
    it = iter(refs)
    x_ref, y_ref, mc_ref = next(it), next(it), next(it)
    mn_ref = next(it) if next_rows is not None else None
    g_ref, b_ref = next(it), next(it)
    r_ref = next(it) if n_experts else None
    xo_ref = next(it)
    u_ref = next(it) if next_rows is not None else None
    go_ref = next(it) if n_experts else None

    gate = mc_ref[gate_row:gate_row + 1, :]
    z = alpha * x_ref[...] + gate * y_ref[...].astype(F32)
    mu = jnp.mean(z, axis=-1, keepdims=True)
    zc = z - mu
    var = jnp.mean(zc * zc, axis=-1, keepdims=True)
    xn = zc * lax.rsqrt(var + eps) * g_ref[l:l + 1, :] + b_ref[l:l + 1, :]
    xo_ref[...] = xn
    if next_rows is not None:
        shift_row, scale_row = next_rows
        mn = mn_ref[...]
        u = xn * (1.0 + mn[scale_row:scale_row + 1, :]) + mn[shift_row:shift_row + 1, :]
        u_ref[...] = u.astype(u_ref.dtype)
        if n_experts:
            logits = jnp.dot(u, r_ref[...], preferred_element_type=F32,
                             precision=lax.Precision.HIGHEST)
            lane = lax.broadcasted_iota(jnp.int32, logits.shape, 1)
            neg = jnp.float32(-jnp.inf)
            lg = jnp.where(lane < n_experts, logits, neg)
            m1 = jnp.max(lg, axis=-1, keepdims=True)
            i1 = jnp.min(jnp.where(lg == m1, lane, V7X_LANES), axis=-1, keepdims=True)
            lg2 = jnp.where(lane == i1, neg, lg)
            m2 = jnp.max(lg2, axis=-1, keepdims=True)
            i2 = jnp.min(jnp.where(lg2 == m2, lane, V7X_LANES), axis=-1, keepdims=True)
            e2 = jnp.exp(m2 - m1)
            w1 = 1.0 / (1.0 + e2)
            w2 = e2 / (1.0 + e2)
            go_ref[...] = jnp.where(lane == i1, w1, 0.0) + jnp.where(lane == i2, w2, 0.0)


def _layer_norm(cfg, x_all, y, mod_cur, gate_row, ln_g, ln_b, l, mod_next=None, next_rows=None,
                router=None):
    M, D = x_all.shape
    rt = cfg.row_tile
    sel = _mod_sel(cfg)
    alpha = float((2 * cfg.depth) ** 0.25)
    n_experts = cfg.n_experts if router is not None else 0
    row_spec = pl.BlockSpec((rt, D), lambda i: (i, 0))
    mod_spec = pl.BlockSpec((None, 6, D), lambda i: (sel(i), 0, 0))
    par_spec = pl.BlockSpec(ln_g.shape, lambda i: (0, 0))
    in_specs = [row_spec, row_spec, mod_spec]
    args = [x_all, y, mod_cur]
    if next_rows is not None:
        in_specs.append(mod_spec)
        args.append(mod_next)
    in_specs += [par_spec, par_spec]
    args += [ln_g, ln_b]
    out_shape = [jax.ShapeDtypeStruct((M, D), F32)]
    out_specs = [row_spec]
    if next_rows is not None:
        out_shape.append(jax.ShapeDtypeStruct((M, D), BF16))
        out_specs.append(row_spec)
    if n_experts:
        in_specs.append(pl.BlockSpec(router.shape, lambda i: (0, 0)))
        args.append(router)
        out_shape.append(jax.ShapeDtypeStruct((M, V7X_LANES), F32))
        out_specs.append(pl.BlockSpec((rt, V7X_LANES), lambda i: (i, 0)))
    vmem = 2 * rt * D * (4 + 4 + 4 + 2) + 4 * 8 * D * 4 + 4 * rt * D * 4 + (2 * D * 128 * 4 * 4 if n_experts else 0)
    kern = functools.partial(_ln_kernel, alpha=alpha, eps=cfg.eps, gate_row=gate_row, l=l,
                             next_rows=next_rows, n_experts=n_experts)
    return pl.pallas_call(
        kern,
        out_shape=out_shape,
        grid=(_exact_div(M, rt),),
        in_specs=in_specs,
        out_specs=out_specs,
        compiler_params=_params(vmem, 1),
        name="deepnorm_ln",
    )(*args)


def _rms_rope(x, gain, cos, sin, eps):
    xn = x * lax.rsqrt(jnp.mean(x * x, axis=-1, keepdims=True) + eps) * gain
    if cos is None:
        return xn
    d = x.shape[-1]
    q = d // 4
    lane = lax.broadcasted_iota(jnp.int32, xn.shape, 1)
    fwd = pltpu.roll(xn, d - q, axis=1)
    bwd = pltpu.roll(xn, q, axis=1)
    swapped = jnp.where(lane % (2 * q) < q, fwd, bwd)
    return xn * cos + swapped * sin


def _dot_nt(a, b):
    return lax.dot_general(a, b, (((1,), (1,)), ((), ())), preferred_element_type=F32)


def _attend_chunks(q, chunks):
    m = l = acc = None
    for k, v, bias in chunks:
        s = _dot_nt(q, k)
        if bias is not None:
            s = s + bias
        cm = jnp.max(s, axis=-1, keepdims=True)
        if m is None:
            m_new = cm
            p = jnp.exp(s - m_new)
            l = jnp.sum(p, axis=-1, keepdims=True)
            acc = jnp.dot(p.astype(BF16), v, preferred_element_type=F32)
        else:
            m_new = jnp.maximum(m, cm)
            a = jnp.exp(m - m_new)
            p = jnp.exp(s - m_new)
            l = a * l + jnp.sum(p, axis=-1, keepdims=True)
            acc = a * acc + jnp.dot(p.astype(BF16), v, preferred_element_type=F32)
        m = m_new
    return acc / l


def _na_kernel(q_ref, k_ref, v_ref, bias_ref, o_ref, *, ctx_len, grid_w, qrows, krows, n_rows, scale):
    t = pl.program_id(2)
    ks = jnp.clip(qrows * t - qrows, 0, n_rows - krows)
    start = pl.multiple_of(ctx_len + ks * grid_w, 16)
    nk = krows * grid_w
    q = (q_ref[...].astype(F32) * scale).astype(BF16)
    k_win = k_ref[pl.ds(start, nk), :]
    v_win = v_ref[pl.ds(start, nk), :]
    k_ctx = k_ref[0:ctx_len, :]
    v_ctx = v_ref[0:ctx_len, :]
    o = _attend_chunks(q, [(k_win, v_win, bias_ref[...]), (k_ctx, v_ctx, None)])
    o_ref[...] = o.astype(o_ref.dtype)


def _na_bias(cfg, rpb):
    W = cfg.grid_w
    n_rows = _exact_div(cfg.seq, W)
    qr, kr = cfg.na_qrows, cfg.na_qrows + cfg.na_win_h
    n_t = _exact_div(n_rows, qr)
    wh, ww = cfg.na_win_h, cfg.na_win_w
    roffs, valids = [], []
    for t in (0, 1, n_t - 1):
        ks = int(np.clip(qr * t - qr, 0, n_rows - kr))
        r = qr * t + np.arange(qr)
        rs = np.clip(r - wh // 2, 0, n_rows - wh)
        kabs = ks + np.arange(kr)
        vrow = (kabs[None, :] >= rs[:, None]) & (kabs[None, :] < rs[:, None] + wh)
        roff = np.clip(kabs[None, :] - r[:, None] + (wh - 1), 0, 2 * wh - 2)
        roffs.append(roff)
        valids.append(vrow)
    roff = np.stack(roffs)
    vrow = np.stack(valids)
    cols = np.arange(W)
    cs = np.clip(cols - ww // 2, 0, W - ww)
    vcol = (cols[None, :] >= cs[:, None]) & (cols[None, :] < cs[:, None] + ww)
    coff = np.clip(cols[None, :] - cols[:, None], -(ww - 1), ww - 1) + (ww - 1)
    ridx = np.broadcast_to(roff[:, :, None, :, None], (3, qr, W, kr, W)).reshape(3, qr * W, kr * W)
    cidx = np.broadcast_to(coff[None, None, :, None, :], (3, qr, W, kr, W)).reshape(3, qr * W, kr * W)
    valid = (vrow[:, :, None, :, None] & vcol[None, None, :, None, :]).reshape(3, qr * W, kr * W)
    bias = rpb.astype(F32)[:, ridx, cidx]
    return jnp.where(valid[None], bias, cfg.neg_inf)


def _na_attention(cfg, p, bias, offs):
    M = p.shape[0]
    W, hd = cfg.grid_w, cfg.head_dim
    RB = cfg.ctx_len + cfg.seq
    n_rows = _exact_div(cfg.seq, W)
    qr, kr = cfg.na_qrows, cfg.na_qrows + cfg.na_win_h
    n_t = _exact_div(n_rows, qr)
    qb = qr * W
    q0 = _exact_div(cfg.ctx_len, qb)
    tiles_b = _exact_div(RB, qb)
    cq, ck, cv = (_exact_div(offs[i], hd) for i in range(3))

    def q_map(b, h, t):
        return (b * tiles_b + q0 + t, cq + h)

    def var_map(b, h, t):
        return (h, jnp.where(t == 0, 0, jnp.where(t == n_t - 1, 2, 1)), 0, 0)

    kern = functools.partial(_na_kernel, ctx_len=cfg.ctx_len, grid_w=W, qrows=qr, krows=kr,
                             n_rows=n_rows, scale=float(hd) ** -0.5)
    vmem = 4 * RB * hd * 2 + 2 * qb * kr * W * 4 + 6 * qb * (kr * W + cfg.ctx_len) * 4
    return pl.pallas_call(
        kern,
        out_shape=jax.ShapeDtypeStruct((M, cfg.na_heads * hd), BF16),
        grid=(cfg.batch, cfg.na_heads, n_t),
        in_specs=[pl.BlockSpec((qb, hd), q_map),
                  pl.BlockSpec((RB, hd), lambda b, h, t: (b, ck + h)),
                  pl.BlockSpec((RB, hd), lambda b, h, t: (b, cv + h)),
                  pl.BlockSpec((None, None, qb, kr * W), var_map)],
        out_specs=pl.BlockSpec((qb, hd), lambda b, h, t: (b * tiles_b + q0 + t, h)),
        compiler_params=_params(vmem, 3),
        name="na_attention",
    )(p, p, p, bias)


def _gq_kernel(q_ref, k_ref, v_ref, qg_ref, kg_ref, cos_ref, sin_ref, o_ref, kn_ref, *,
               l, groups, hd, tq, kc, ctx_len, seq, eps, scale):
    qi = pl.program_id(2)

    @pl.when(qi == 0)
    def _():
        kn = _rms_rope(k_ref[...].astype(F32), kg_ref[l:l + 1, :], cos_ref[...], sin_ref[...], eps)
        kn_ref[...] = kn.astype(BF16)

    row0 = pl.multiple_of(ctx_len + qi * tq, 8)
    cos = cos_ref[pl.ds(row0, tq), :]
    sin = sin_ref[pl.ds(row0, tq), :]
    gain = qg_ref[l:l + 1, :] * scale
    qs = [_rms_rope(q_ref[:, g * hd:(g + 1) * hd].astype(F32), gain, cos, sin, eps).astype(BF16)
          for g in range(groups)]
    q = jnp.concatenate(qs, axis=0)
    bounds = [(0, ctx_len)] + [(ctx_len + c * kc, kc) for c in range(seq // kc)]
    o = _attend_chunks(q, [(kn_ref[s0:s0 + n, :], v_ref[s0:s0 + n, :], None) for s0, n in bounds])
    for g in range(groups):
        o_ref[:, g * hd:(g + 1) * hd] = o[g * tq:(g + 1) * tq, :].astype(o_ref.dtype)


def _rope_tables(cfg):
    hd = cfg.head_dim
    q = hd // 4
    t = np.arange(cfg.seq)
    inv = cfg.rope_theta ** (-np.arange(q, dtype=np.float64) / q)
    ang_r = (t // cfg.grid_w)[:, None] * inv[None, :]
    ang_c = (t % cfg.grid_w)[:, None] * inv[None, :]
    cos = np.concatenate([np.cos(ang_r), np.cos(ang_r), np.cos(ang_c), np.cos(ang_c)], axis=1)
    sin = np.concatenate([-np.sin(ang_r), np.sin(ang_r), -np.sin(ang_c), np.sin(ang_c)], axis=1)
    cos = np.concatenate([np.ones((cfg.ctx_len, hd)), cos], axis=0)
    sin = np.concatenate([np.zeros((cfg.ctx_len, hd)), sin], axis=0)
    return jnp.asarray(cos, F32), jnp.asarray(sin, F32)


def _gq_attention(cfg, p, q_gain, k_gain, cos, sin, l, offs):
    M = p.shape[0]
    hd = cfg.head_dim
    RB = cfg.ctx_len + cfg.seq
    G = _exact_div(cfg.gq_heads, cfg.gq_kv)
    tq = cfg.gq_tq
    nq = _exact_div(cfg.seq, tq)
    q0 = _exact_div(cfg.ctx_len, tq)
    tiles_b = _exact_div(RB, tq)
    cq = _exact_div(offs[3], G * hd)
    ck = _exact_div(offs[4], hd)
    cv = _exact_div(offs[5], hd)
    kc = cfg.gq_kc
    _exact_div(cfg.seq, kc)
    kern = functools.partial(_gq_kernel, l=l, groups=G, hd=hd, tq=tq, kc=kc, ctx_len=cfg.ctx_len,
                             seq=cfg.seq, eps=cfg.eps, scale=float(hd) ** -0.5)
    vmem = 4 * RB * hd * 2 + RB * hd * 2 + 4 * RB * hd * 4 + 2 * 2 * tq * G * hd * 2 + 6 * G * tq * kc * 4
    return pl.pallas_call(
        kern,
        out_shape=jax.ShapeDtypeStruct((M, cfg.gq_heads * hd), BF16),
        grid=(cfg.batch, cfg.gq_kv, nq),
        in_specs=[pl.BlockSpec((tq, G * hd), lambda b, k, i: (b * tiles_b + q0 + i, cq + k)),
                  pl.BlockSpec((RB, hd), lambda b, k, i: (b, ck + k)),
                  pl.BlockSpec((RB, hd), lambda b, k, i: (b, cv + k)),
                  pl.BlockSpec(q_gain.shape, lambda b, k, i: (0, 0)),
                  pl.BlockSpec(k_gain.shape, lambda b, k, i: (0, 0)),
                  pl.BlockSpec((RB, hd), lambda b, k, i: (0, 0)),
                  pl.BlockSpec((RB, hd), lambda b, k, i: (0, 0))],
        out_specs=pl.BlockSpec((tq, G * hd), lambda b, k, i: (b * tiles_b + q0 + i, k)),
        scratch_shapes=[pltpu.VMEM((RB, hd), BF16)],
        compiler_params=_params(vmem, 3),
        name="gq_attention",
    )(p, p, p, q_gain, k_gain, cos, sin)


def _ctx_attn_kernel(*refs, l, groups, hd, norm, eps, scale):
    if norm:
        q_ref, k_ref, v_ref, qg_ref, kg_ref, _, o_ref = refs
    else:
        q_ref, k_ref, v_ref, _, o_ref = refs
    k = k_ref[...]
    if norm:
        k = _rms_rope(k.astype(F32), kg_ref[l:l + 1, :], None, None, eps).astype(BF16)
    n = q_ref.shape[0]
    qs = []
    for g in range(groups):
        qg = q_ref[:, g * hd:(g + 1) * hd].astype(F32)
        if norm:
            qg = _rms_rope(qg, qg_ref[l:l + 1, :] * scale, None, None, eps)
        else:
            qg = qg * scale
        qs.append(qg.astype(BF16))
    q = jnp.concatenate(qs, axis=0)
    o = _attend_chunks(q, [(k, v_ref[...], None)])
    for g in range(groups):
        o_ref[:, g * hd:(g + 1) * hd] = o[g * n:(g + 1) * n, :].astype(o_ref.dtype)


def _ctx_attention(cfg, p, y_prev, n_kv, groups, col_q, col_k, col_v, l, gains=None):
    hd = cfg.head_dim
    C = cfg.ctx_len
    tiles_b = _exact_div(cfg.ctx_len + cfg.seq, C)
    cq = _exact_div(col_q, groups * hd)
    ck = _exact_div(col_k, hd)
    cv = _exact_div(col_v, hd)
    norm = gains is not None
    in_specs = [pl.BlockSpec((C, groups * hd), lambda b, k: (b * tiles_b, cq + k)),
                pl.BlockSpec((C, hd), lambda b, k: (b * tiles_b, ck + k)),
                pl.BlockSpec((C, hd), lambda b, k: (b * tiles_b, cv + k))]
    args = [p, p, p]
    if norm:
        in_specs += [pl.BlockSpec(gains[0].shape, lambda b, k: (0, 0)),
                     pl.BlockSpec(gains[1].shape, lambda b, k: (0, 0))]
        args += list(gains)
    in_specs.append(pl.BlockSpec(memory_space=pl.ANY))
    args.append(y_prev)
    kern = functools.partial(_ctx_attn_kernel, l=l, groups=groups, hd=hd, norm=norm, eps=cfg.eps,
                             scale=float(hd) ** -0.5)
    return pl.pallas_call(
        kern,
        out_shape=jax.ShapeDtypeStruct(y_prev.shape, y_prev.dtype),
        grid=(cfg.batch, n_kv),
        in_specs=in_specs,
        out_specs=pl.BlockSpec((C, groups * hd), lambda b, k: (b * tiles_b, k)),
        input_output_aliases={len(args) - 1: 0},
        compiler_params=_params(8 << 20, 2),
        name="ctx_attention",
    )(*args)


def _lru_kernel(x_ref, gate_ref, cw_ref, cb_ref, wr_ref, br_ref, wi_ref, bi_ref, lam_ref, o_ref,
                xpad_ref, af_ref, bf_ref, ab_ref, bb_ref, *, l, ctx_len, seq, chunk, lru_c, conv_w):
    cwid = x_ref.shape[1]
    nblk = cwid // V7X_LANES
    pad = V7X_SUBLANES
    ctx_base = pad
    lat_base = ctx_base + ctx_len + pad
    zeros = jnp.zeros((pad, cwid), F32)
    xpad_ref[0:pad, :] = zeros
    xpad_ref[ctx_base + ctx_len:lat_base, :] = zeros
    xpad_ref[lat_base + seq:lat_base + seq + pad, :] = zeros
    xpad_ref[ctx_base:ctx_base + ctx_len, :] = x_ref[0:ctx_len, :].astype(F32)
    xpad_ref[lat_base:lat_base + seq, :] = x_ref[ctx_len:ctx_len + seq, :].astype(F32)

    taps = cw_ref[...]
    cbias = cb_ref[l:l + 1, :]
    lam = lam_ref[...]
    nl = -lam
    softplus = jnp.maximum(nl, 0.0) + jnp.log1p(jnp.exp(-jnp.abs(nl)))
    a_refs = (af_ref, ab_ref)
    b_refs = (bf_ref, bb_ref)

    def coeffs(src_base, dst_row, rows):
        xc = cbias
        for tap in range(conv_w):
            xc = xc + taps[tap:tap + 1, :] * xpad_ref[pl.ds(src_base + tap - conv_w // 2, rows), :]
        xcb = xc.astype(BF16)
        for d in range(2):
            rs, is_ = [], []
            for j in range(nblk):
                xj = xcb[:, j * V7X_LANES:(j + 1) * V7X_LANES]
                rs.append(jnp.dot(xj, wr_ref[d, j].astype(BF16), preferred_element_type=F32))
                is_.append(jnp.dot(xj, wi_ref[d, j].astype(BF16), preferred_element_type=F32))
            gr = jax.nn.sigmoid(jnp.concatenate(rs, axis=1) + br_ref[d:d + 1, :])
            gi = jax.nn.sigmoid(jnp.concatenate(is_, axis=1) + bi_ref[d:d + 1, :])
            log_a = -lru_c * gr * softplus[d:d + 1, :]
            a = jnp.exp(log_a)
            a_refs[d][pl.ds(dst_row, rows), :] = a
            one_minus_a2 = -jnp.tanh(log_a) * (1.0 + a * a)
            b_refs[d][pl.ds(dst_row, rows), :] = jnp.sqrt(one_minus_a2) * (gi * xc)

    coeffs(ctx_base, 0, ctx_len)
    for c in range(seq // chunk):
        coeffs(lat_base + c * chunk, ctx_len + c * chunk, chunk)

    def scan_seg(first, n, h):
        def body(i, hs):
            hf, hb = hs
            tf = first + i
            tb = first + n - 1 - i
            hf = af_ref[pl.ds(tf, 1), :] * hf + bf_ref[pl.ds(tf, 1), :]
            bf_ref[pl.ds(tf, 1), :] = hf
            hb = ab_ref[pl.ds(tb, 1), :] * hb + bb_ref[pl.ds(tb, 1), :]
            bb_ref[pl.ds(tb, 1), :] = hb
            return hf, hb
        return lax.fori_loop(0, n, body, h, unroll=8)

    h0 = jnp.zeros((1, cwid), F32)
    hs = scan_seg(0, ctx_len, (h0, h0))
    scan_seg(ctx_len, seq, hs)

    def out_body(c, carry):
        r0 = pl.multiple_of(c * ctx_len, ctx_len)
        h = bf_ref[pl.ds(r0, ctx_len), :] + bb_ref[pl.ds(r0, ctx_len), :]
        g = gate_ref[pl.ds(r0, ctx_len), :].astype(F32)
        o_ref[pl.ds(r0, ctx_len), :] = (h * jax.nn.gelu(g)).astype(o_ref.dtype)
        return carry

    lax.fori_loop(0, (ctx_len + seq) // ctx_len, out_body, 0)


def _lru_mixer(cfg, p, conv_w, conv_b, w_r, b_r, w_i, b_i, lam, l, offs):
    M = p.shape[0]
    RB = cfg.ctx_len + cfg.seq
    cw = cfg.lru_cw
    ng = _exact_div(cfg.lru_width, cw)
    blk = _exact_div(cfg.lru_width, cfg.lru_blocks)
    assert blk == V7X_LANES
    nb = _exact_div(cw, blk)
    cx = _exact_div(offs[6], cw)
    cg = _exact_div(offs[7], cw)
    _exact_div(cfg.seq, cfg.lru_chunk)
    _exact_div(RB, cfg.ctx_len)
    kern = functools.partial(_lru_kernel, l=l, ctx_len=cfg.ctx_len, seq=cfg.seq, chunk=cfg.lru_chunk,
                             lru_c=cfg.lru_c, conv_w=cfg.conv_w)
    vmem = 5 * (RB + 32) * cw * 4 + 3 * 2 * RB * cw * 2 + 12 * cfg.lru_chunk * cw * 4 + (4 << 20)
    return pl.pallas_call(
        kern,
        out_shape=jax.ShapeDtypeStruct((M, cfg.lru_width), BF16),
        grid=(cfg.batch, ng),
        in_specs=[pl.BlockSpec((RB, cw), lambda b, g: (b, cx + g)),
                  pl.BlockSpec((RB, cw), lambda b, g: (b, cg + g)),
                  pl.BlockSpec((None, cfg.conv_w, cw), lambda b, g: (l, 0, g)),
                  pl.BlockSpec((conv_b.shape[0], cw), lambda b, g: (0, g)),
                  pl.BlockSpec((None, 2, nb, blk, blk), lambda b, g: (l, 0, g, 0, 0)),
                  pl.BlockSpec((None, 2, cw), lambda b, g: (l, 0, g)),
                  pl.BlockSpec((None, 2, nb, blk, blk), lambda b, g: (l, 0, g, 0, 0)),
                  pl.BlockSpec((None, 2, cw), lambda b, g: (l, 0, g)),
                  pl.BlockSpec((None, 2, cw), lambda b, g: (l, 0, g))],
        out_specs=pl.BlockSpec((RB, cw), lambda b, g: (b, g)),
        scratch_shapes=[pltpu.VMEM((RB + 3 * V7X_SUBLANES, cw), F32)] + [pltpu.VMEM((RB, cw), F32)] * 4,
        compiler_params=_params(vmem, 2),
        name="rglru_mixer",
    )(p, p, conv_w, conv_b, w_r, b_r, w_i, b_i, lam)


def _merge_kernel(ya_ref, yb_ref, yc_ref, wa_ref, wb_ref, wc_ref, g0_ref, g1_ref, g2_ref, o_ref,
                  wab_ref, wbb_ref, wcb_ref):
    @pl.when(pl.program_id(1) == 0)
    def _():
        wab_ref[...] = wa_ref[...].astype(BF16)
        wbb_ref[...] = wb_ref[...].astype(BF16)
        wcb_ref[...] = wc_ref[...].astype(BF16)

    acc = jax.nn.sigmoid(g0_ref[...].astype(F32)) * jnp.dot(ya_ref[...], wab_ref[...], preferred_element_type=F32)
    acc += jax.nn.sigmoid(g1_ref[...].astype(F32)) * jnp.dot(yb_ref[...], wbb_ref[...], preferred_element_type=F32)
    acc += jax.nn.sigmoid(g2_ref[...].astype(F32)) * jnp.dot(yc_ref[...], wcb_ref[...], preferred_element_type=F32)
    o_ref[...] = acc.astype(o_ref.dtype)


def _merge(cfg, ya, yb, yc, p, w_na, w_gq, w_lru, l, offs):
    M = ya.shape[0]
    D = cfg.d_model
    bm, bn = cfg.mm_bm, cfg.mm_bn
    nj = _exact_div(D, bn)
    gcol = _exact_div(offs[8], bn)
    ks = (ya.shape[1], yb.shape[1], yc.shape[1])
    ksum = sum(ks)
    vmem = 2 * bm * ksum * 2 + 2 * ksum * bn * 4 + ksum * bn * 2 + 3 * 2 * bm * bn * 2 + 2 * bm * bn * 2 + 4 * bm * bn * 4
    x_specs = [pl.BlockSpec((bm, k), lambda j, i: (i, 0)) for k in ks]
    w_specs = [pl.BlockSpec((None, k, bn), lambda j, i: (l, 0, j)) for k in ks]
    g_specs = [pl.BlockSpec((bm, bn), functools.partial(lambda j, i, s: (i, gcol + s * nj + j), s=s))
               for s in range(3)]
    return pl.pallas_call(
        _merge_kernel,
        out_shape=jax.ShapeDtypeStruct((M, D), BF16),
        grid=(nj, _exact_div(M, bm)),
        in_specs=x_specs + w_specs + g_specs,
        out_specs=pl.BlockSpec((bm, bn), lambda j, i: (i, j)),
        scratch_shapes=[pltpu.VMEM((k, bn), BF16) for k in ks],
        compiler_params=_params(vmem, 2),
        name="branch_merge",
    )(ya, yb, yc, w_na, w_gq, w_lru, p, p, p)


def _forward(cfg, x, c, ctx, c_ctx, w_ada, b_ada, w_in, na_rpb, gq_q_gain, gq_k_gain, lru_conv_w,
             lru_conv_b, lru_w_r, lru_b_r, lru_w_i, lru_b_i, lru_lambda, w_br_na, w_br_gq, w_br_lru,
             w_o, ln1_g, ln1_b, ln2_g, ln2_b, ffn_w1, ffn_w3, ffn_w2, moe_router, moe_w1, moe_w3,
             moe_w2):
    B, D = cfg.batch, cfg.d_model
    offs, widths = _splits(cfg)
    in_cols = offs[-1] + widths[-1]
    bm, bn = cfg.mm_bm, cfg.mm_bn

    x_all = jnp.concatenate([ctx, x], axis=1).reshape(B * (cfg.ctx_len + cfg.seq), D)
    cond = jnp.concatenate([c, c_ctx[None, :], jnp.zeros((V7X_SUBLANES - B - 1, D), F32)], axis=0)
    b_ada3 = b_ada.reshape(cfg.depth, 1, 6 * D)
    mods = [_ada_table(cond, w_ada, b_ada3, l).reshape(V7X_SUBLANES, 6, D) for l in range(cfg.depth)]
    cos, sin = _rope_tables(cfg)

    u = _modulate(cfg, x_all, mods[0], 0, 1)
    for l in range(cfg.depth):
        need_ctx = l < cfg.depth - 1
        mod = mods[l]
        p = _matmul(u, [w_in], (l,), bm=bm, bn=bn, out_dtype=BF16, name="in_proj")
        assert p.shape[1] == in_cols
        bias = _na_bias(cfg, na_rpb[l])
        ya = _na_attention(cfg, p, bias, offs)
        yb = _gq_attention(cfg, p, gq_q_gain, gq_k_gain, cos, sin, l, offs)
        if need_ctx:
            ya = _ctx_attention(cfg, p, ya, cfg.na_heads, 1, offs[0], offs[1], offs[2], l)
            yb = _ctx_attention(cfg, p, yb, cfg.gq_kv, cfg.gq_heads // cfg.gq_kv, offs[3], offs[4],
                                offs[5], l, gains=(gq_q_gain, gq_k_gain))
        yc = _lru_mixer(cfg, p, lru_conv_w, lru_conv_b, lru_w_r, lru_b_r, lru_w_i, lru_b_i,
                        lru_lambda, l, offs)
        merged = _merge(cfg, ya, yb, yc, p, w_br_na, w_br_gq, w_br_lru, l, offs)
        y = _matmul(merged, [w_o], (l,), bm=bm, bn=bn, out_dtype=F32, name="out_proj")

        dense = l % 2 == 0
        router = None
        if not dense:
            router = jnp.pad(moe_router[l // 2], ((0, 0), (0, V7X_LANES - cfg.n_experts)))
        res = _layer_norm(cfg, x_all, y, mod, 2, ln1_g, ln1_b, l, mod_next=mod, next_rows=(3, 4),
                          router=router)
        x_all, u2 = res[0], res[1]
        if dense:
            h = _matmul(u2, [ffn_w1, ffn_w3], (l // 2,), bm=bm, bn=cfg.glu_bn, out_dtype=BF16,
                        glu=True, name="ffn_up")
            w2 = ffn_w2[l // 2].astype(BF16)[None]
            f = _matmul(h, [w2], (0,), bm=cfg.down_bm, bn=bn, out_dtype=F32, name="ffn_down")
        else:
            gates = res[2]
            h = _matmul(u2, [moe_w1, moe_w3], (l // 2,), bm=bm, bn=cfg.glu_bn, out_dtype=BF16,
                        glu=True, gates=gates, n_experts=cfg.n_experts, name="moe_up")
            f = None
            for e in range(cfg.n_experts):
                f = _matmul(h, [moe_w2], (l // 2, e), k_blk=e, bm=bm, bn=bn, out_dtype=F32,
                            add=f, name="moe_down")
        if l + 1 < cfg.depth:
            x_all, u = _layer_norm(cfg, x_all, f, mod, 5, ln2_g, ln2_b, l, mod_next=mods[l + 1],
                                   next_rows=(0, 1))
        else:
            (x_all,) = _layer_norm(cfg, x_all, f, mod, 5, ln2_g, ln2_b, l)
    out = x_all.reshape(B, cfg.ctx_len + cfg.seq, D)[:, cfg.ctx_len:, :]
    return out


def kernel(x, c, ctx, c_ctx, w_ada, b_ada, w_in, na_rpb, gq_q_gain, gq_k_gain, lru_conv_w, lru_conv_b, lru_w_r, lru_b_r, lru_w_i, lru_b_i, lru_lambda, w_br_na, w_br_gq, w_br_lru, w_o, ln1_g, ln1_b, ln2_g, ln2_b, ffn_w1, ffn_w3, ffn_w2, moe_router, moe_w1, moe_w3, moe_w2):
    return _forward(FULL_CFG, x, c, ctx, c_ctx, w_ada, b_ada, w_in, na_rpb, gq_q_gain, gq_k_gain,
                    lru_conv_w, lru_conv_b, lru_w_r, lru_b_r, lru_w_i, lru_b_i, lru_lambda, w_br_na,
                    w_br_gq, w_br_lru, w_o, ln1_g, ln1_b, ln2_g, ln2_b, ffn_w1, ffn_w3, ffn_w2,
                    moe_router, moe_w1, moe_w3, moe_w2)
```

```python
import collections
import functools

import numpy as np
import jax
import jax.numpy as jnp
from jax import lax
from jax.experimental import pallas as pl
from jax.experimental.pallas import tpu as pltpu

F32 = jnp.float32
BF16 = jnp.bfloat16

V7X_LANES = 128
V7X_SUBLANES = 8
V7X_VMEM_BYTES = 64 * 1024 * 1024
VMEM_CAP_BYTES = V7X_VMEM_BYTES - 6 * 1024 * 1024

Cfg = collections.namedtuple(
    "Cfg",
    "d_model batch seq depth ctx_len grid_w head_dim na_heads na_win_h na_win_w gq_heads gq_kv "
    "rope_theta lru_width lru_blocks conv_w lru_c d_ff n_experts d_ff_expert eps neg_inf "
    "row_tile mm_bm mm_bn glu_bn down_bm na_qrows gq_tq gq_kc lru_cw lru_chunk")

FULL_CFG = Cfg(
    d_model=4096, batch=2, seq=4096, depth=2, ctx_len=256, grid_w=64, head_dim=128,
    na_heads=12, na_win_h=8, na_win_w=16, gq_heads=16, gq_kv=4, rope_theta=10000.0,
    lru_width=1536, lru_blocks=12, conv_w=4, lru_c=8.0, d_ff=11008, n_experts=8,
    d_ff_expert=3072, eps=1e-6, neg_inf=-1e30,
    row_tile=256, mm_bm=512, mm_bn=512, glu_bn=256, down_bm=256, na_qrows=4, gq_tq=256, gq_kc=1024,
    lru_cw=256, lru_chunk=512)


def _splits(cfg):
    na_w = cfg.na_heads * cfg.head_dim
    gq_q = cfg.gq_heads * cfg.head_dim
    gq_kv = cfg.gq_kv * cfg.head_dim
    widths = (na_w, na_w, na_w, gq_q, gq_kv, gq_kv, cfg.lru_width, cfg.lru_width, 3 * cfg.d_model)
    offs = tuple(int(v) for v in np.cumsum((0,) + widths[:-1]))
    return offs, widths


def _exact_div(a, b):
    assert a % b == 0, (a, b)
    return a // b


def _params(vmem_bytes, n_axes):
    limit = int(min(VMEM_CAP_BYTES, max(vmem_bytes * 5 // 4 + (4 << 20), 16 << 20)))
    return pltpu.CompilerParams(dimension_semantics=("arbitrary",) * n_axes, vmem_limit_bytes=limit)


def _mm_kernel(*refs, n_w, cast_w, glu, has_gate, has_add):
    it = iter(refs)
    x_ref = next(it)
    w_refs = [next(it) for _ in range(n_w)]
    g_ref = next(it) if has_gate else None
    a_ref = next(it) if has_add else None
    o_ref = next(it)
    wb_refs = [next(it) for _ in range(n_w)] if cast_w else w_refs

    if cast_w:
        @pl.when(pl.program_id(2) == 0)
        def _():
            for w_ref, wb_ref in zip(w_refs, wb_refs):
                wb_ref[...] = w_ref[...].astype(BF16)

    x = x_ref[...]
    acc = jnp.dot(x, wb_refs[0][...], preferred_element_type=F32)
    if glu:
        acc = jax.nn.silu(acc) * jnp.dot(x, wb_refs[1][...], preferred_element_type=F32)
    if has_gate:
        lane = lax.broadcasted_iota(jnp.int32, g_ref.shape, 1)
        col = jnp.sum(jnp.where(lane == pl.program_id(0), g_ref[...], 0.0), axis=1, keepdims=True)
        acc = acc * col
    if has_add:
        acc = acc + a_ref[...].astype(F32)
    o_ref[...] = acc.astype(o_ref.dtype)


def _matmul(x, ws, w_lead, *, k_blk=0, n_off=0, n_total=None, bm, bn, out_dtype, glu=False,
            gates=None, add=None, out_cols=None, out_col_off=0, n_experts=1, name="mm"):
    M = x.shape[0]
    K = ws[0].shape[-2]
    n_total = ws[0].shape[-1] if n_total is None else n_total
    nj = _exact_div(n_total, bn)
    ni = _exact_div(M, bm)
    out_cols = n_experts * n_total if out_cols is None else out_cols
    cast_w = ws[0].dtype != BF16
    multi = n_experts > 1
    n_lead = len(w_lead) + (1 if multi else 0)
    nob = _exact_div(n_off, bn)
    ocb = _exact_div(out_col_off, bn)

    def w_map(e, j, i):
        return tuple(w_lead) + ((e,) if multi else ()) + (0, nob + j)

    def out_map(e, j, i):
        return (i, ocb + e * nj + j)

    in_specs = [pl.BlockSpec((bm, K), lambda e, j, i: (i, k_blk))]
    in_specs += [pl.BlockSpec((None,) * n_lead + (K, bn), w_map) for _ in ws]
    args = [x] + list(ws)
    if gates is not None:
        in_specs.append(pl.BlockSpec((bm, V7X_LANES), lambda e, j, i: (i, 0)))
        args.append(gates)
    aliases = {}
    if add is not None:
        in_specs.append(pl.BlockSpec((bm, bn), out_map))
        args.append(add)
        if add.dtype == jnp.dtype(out_dtype) and add.shape == (M, out_cols):
            aliases = {len(args) - 1: 0}
    scratch = [pltpu.VMEM((K, bn), BF16) for _ in ws] if cast_w else []
    wbytes = ws[0].dtype.itemsize
    vmem = (2 * bm * K * 2 + len(ws) * (2 * K * bn * wbytes + (K * bn * 2 if cast_w else 0))
            + 2 * bm * bn * jnp.dtype(out_dtype).itemsize + (2 * bm * bn * 4 if add is not None else 0)
            + 3 * bm * bn * 4)
    kern = functools.partial(_mm_kernel, n_w=len(ws), cast_w=cast_w, glu=glu,
                             has_gate=gates is not None, has_add=add is not None)
    return pl.pallas_call(
        kern,
        out_shape=jax.ShapeDtypeStruct((M, out_cols), out_dtype),
        grid=(n_experts, nj, ni),
        in_specs=in_specs,
        out_specs=pl.BlockSpec((bm, bn), out_map),
        scratch_shapes=scratch,
        input_output_aliases=aliases,
        compiler_params=_params(vmem, 3),
        name=name,
    )(*args)


def _ada_kernel(c_ref, w_ref, b_ref, o_ref):
    c = c_ref[...]
    s = (c * jax.nn.sigmoid(c)).astype(BF16)
    acc = jnp.dot(s, w_ref[...].astype(BF16), preferred_element_type=F32)
    o_ref[...] = acc + b_ref[...]


def _ada_table(cond, w_ada, b_ada3, l, bn=1024):
    R, D = cond.shape
    N = w_ada.shape[-1]
    vmem = 2 * D * bn * 4 + D * bn * 2 + 4 * R * bn * 4 + 2 * R * D * 4
    return pl.pallas_call(
        _ada_kernel,
        out_shape=jax.ShapeDtypeStruct((R, N), F32),
        grid=(_exact_div(N, bn),),
        in_specs=[pl.BlockSpec((R, D), lambda j: (0, 0)),
                  pl.BlockSpec((None, D, bn), lambda j: (l, 0, j)),
                  pl.BlockSpec((None, 1, bn), lambda j: (l, 0, j))],
        out_specs=pl.BlockSpec((R, bn), lambda j: (0, j)),
        compiler_params=_params(vmem, 1),
        name="ada_table",
    )(cond, w_ada, b_ada3)


def _mod_sel(cfg):
    tiles_per_batch = _exact_div(cfg.ctx_len + cfg.seq, cfg.row_tile)
    ctx_tiles = _exact_div(cfg.ctx_len, cfg.row_tile)

    def sel(i):
        return jnp.where(i % tiles_per_batch < ctx_tiles, cfg.batch, i // tiles_per_batch)
    return sel


def _modulate_kernel(x_ref, m_ref, u_ref, *, shift_row, scale_row):
    m = m_ref[...]
    u = x_ref[...] * (1.0 + m[scale_row:scale_row + 1, :]) + m[shift_row:shift_row + 1, :]
    u_ref[...] = u.astype(u_ref.dtype)


def _modulate(cfg, x_all, modtab, shift_row, scale_row):
    M, D = x_all.shape
    rt = cfg.row_tile
    sel = _mod_sel(cfg)
    vmem = 2 * rt * D * 4 + 2 * rt * D * 2 + 2 * 8 * D * 4 + 2 * rt * D * 4
    return pl.pallas_call(
        functools.partial(_modulate_kernel, shift_row=shift_row, scale_row=scale_row),
        out_shape=jax.ShapeDtypeStruct((M, D), BF16),
        grid=(_exact_div(M, rt),),
        in_specs=[pl.BlockSpec((rt, D), lambda i: (i, 0)),
                  pl.BlockSpec((None, 6, D), lambda i: (sel(i), 0, 0))],
        out_specs=pl.BlockSpec((rt, D), lambda i: (i, 0)),
        compiler_params=_params(vmem, 1),
        name="modulate",
    )(x_all, modtab)


def _ln_kernel(*refs, alpha, eps, gate_row, l, next_rows, n_experts):
    it = iter(refs)
    x_ref, y_ref, mc_ref = next(it), next(it), next(it)
    mn_ref = next(it) if next_rows is not None else None
    g_ref, b_ref = next(it), next(it)
    r_ref = next(it) if n_experts else None
    xo_ref = next(it)
    u_ref = next(it) if next_rows is not None else None
    go_ref = next(it) if n_experts else None

    gate = mc_ref[gate_row:gate_row + 1, :]
    z = alpha * x_ref[...] + gate * y_ref[...].astype(F32)
    mu = jnp.mean(z, axis=-1, keepdims=True)
    zc = z - mu
    var = jnp.mean(zc * zc, axis=-1, keepdims=True)
    xn = zc * lax.rsqrt(var + eps) * g_ref[l:l + 1, :] + b_ref[l:l + 1, :]
    xo_ref[...] = xn
    if next_rows is not None:
        shift_row, scale_row = next_rows
        mn = mn_ref[...]
        u = xn * (1.0 + mn[scale_row:scale_row + 1, :]) + mn[shift_row:shift_row + 1, :]
        u_ref[...] = u.astype(u_ref.dtype)
        if n_experts:
            logits = jnp.dot(u, r_ref[...], preferred_element_type=F32,
                             precision=lax.Precision.HIGHEST)
            lane = lax.broadcasted_iota(jnp.int32, logits.shape, 1)
            neg = jnp.float32(-jnp.inf)
            lg = jnp.where(lane < n_experts, logits, neg)
            m1 = jnp.max(lg, axis=-1, keepdims=True)
            i1 = jnp.min(jnp.where(lg == m1, lane, V7X_LANES), axis=-1, keepdims=True)
            lg2 = jnp.where(lane == i1, neg, lg)
            m2 = jnp.max(lg2, axis=-1, keepdims=True)
            i2 = jnp.min(jnp.where(lg2 == m2, lane, V7X_LANES), axis=-1, keepdims=True)
            e2 = jnp.exp(m2 - m1)
            w1 = 1.0 / (1.0 + e2)
            w2 = e2 / (1.0 + e2)
            go_ref[...] = jnp.where(lane == i1, w1, 0.0) + jnp.where(lane == i2, w2, 0.0)


def _layer_norm(cfg, x_all, y, mod_cur, gate_row, ln_g, ln_b, l, mod_next=None, next_rows=None,
                router=None):
    M, D = x_all.shape
    rt = cfg.row_tile
    sel = _mod_sel(cfg)
    alpha = float((2 * cfg.depth) ** 0.25)
    n_experts = cfg.n_experts if router is not None else 0
    row_spec = pl.BlockSpec((rt, D), lambda i: (i, 0))
    mod_spec = pl.BlockSpec((None, 6, D), lambda i: (sel(i), 0, 0))
    par_spec = pl.BlockSpec(ln_g.shape, lambda i: (0, 0))
    in_specs = [row_spec, row_spec, mod_spec]
    args = [x_all, y, mod_cur]
    if next_rows is not None:
        in_specs.append(mod_spec)
        args.append(mod_next)
    in_specs += [par_spec, par_spec]
    args += [ln_g, ln_b]
    out_shape = [jax.ShapeDtypeStruct((M, D), F32)]
    out_specs = [row_spec]
    if next_rows is not None:
        out_shape.append(jax.ShapeDtypeStruct((M, D), BF16))
        out_specs.append(row_spec)
    if n_experts:
        in_specs.append(pl.BlockSpec(router.shape, lambda i: (0, 0)))
        args.append(router)
        out_shape.append(jax.ShapeDtypeStruct((M, V7X_LANES), F32))
        out_specs.append(pl.BlockSpec((rt, V7X_LANES), lambda i: (i, 0)))
    vmem = 2 * rt * D * (4 + 4 + 4 + 2) + 4 * 8 * D * 4 + 4 * rt * D * 4 + (2 * D * 128 * 4 * 4 if n_experts else 0)
    kern = functools.partial(_ln_kernel, alpha=alpha, eps=cfg.eps, gate_row=gate_row, l=l,
                             next_rows=next_rows, n_experts=n_experts)
    return pl.pallas_call(
        kern,
        out_shape=out_shape,
        grid=(_exact_div(M, rt),),
        in_specs=in_specs,
        out_specs=out_specs,
        compiler_params=_params(vmem, 1),
        name="deepnorm_ln",
    )(*args)


def _rms_rope(x, gain, cos, sin, eps):
    xn = x * lax.rsqrt(jnp.mean(x * x, axis=-1, keepdims=True) + eps) * gain
    if cos is None:
        return xn
    d = x.shape[-1]
    q = d // 4
    lane = lax.broadcasted_iota(jnp.int32, xn.shape, 1)
    fwd = pltpu.roll(xn, d - q, axis=1)
    bwd = pltpu.roll(xn, q, axis=1)
    swapped = jnp.where(lane % (2 * q) < q, fwd, bwd)
    return xn * cos + swapped * sin


def _dot_nt(a, b):
    return lax.dot_general(a, b, (((1,), (1,)), ((), ())), preferred_element_type=F32)


def _attend_chunks(q, chunks):
    m = l = acc = None
    for k, v, bias in chunks:
        s = _dot_nt(q, k)
        if bias is not None:
            s = s + bias
        cm = jnp.max(s, axis=-1, keepdims=True)
        if m is None:
            m_new = cm
            p = jnp.exp(s - m_new)
            l = jnp.sum(p, axis=-1, keepdims=True)
            acc = jnp.dot(p.astype(BF16), v, preferred_element_type=F32)
        else:
            m_new = jnp.maximum(m, cm)
            a = jnp.exp(m - m_new)
            p = jnp.exp(s - m_new)
            l = a * l + jnp.sum(p, axis=-1, keepdims=True)
            acc = a * acc + jnp.dot(p.astype(BF16), v, preferred_element_type=F32)
        m = m_new
    return acc / l


def _na_kernel(q_ref, k_ref, v_ref, bias_ref, o_ref, *, ctx_len, grid_w, qrows, krows, n_rows, scale):
    t = pl.program_id(2)
    ks = jnp.clip(qrows * t - qrows, 0, n_rows - krows)
    start = pl.multiple_of(ctx_len + ks * grid_w, 16)
    nk = krows * grid_w
    q = (q_ref[...].astype(F32) * scale).astype(BF16)
    k_win = k_ref[pl.ds(start, nk), :]
    v_win = v_ref[pl.ds(start, nk), :]
    k_ctx = k_ref[0:ctx_len, :]
    v_ctx = v_ref[0:ctx_len, :]
    o = _attend_chunks(q, [(k_win, v_win, bias_ref[...]), (k_ctx, v_ctx, None)])
    o_ref[...] = o.astype(o_ref.dtype)


def _na_bias(cfg, rpb):
    W = cfg.grid_w
    n_rows = _exact_div(cfg.seq, W)
    qr, kr = cfg.na_qrows, cfg.na_qrows + cfg.na_win_h
    n_t = _exact_div(n_rows, qr)
    wh, ww = cfg.na_win_h, cfg.na_win_w
    roffs, valids = [], []
    for t in (0, 1, n_t - 1):
        ks = int(np.clip(qr * t - qr, 0, n_rows - kr))
        r = qr * t + np.arange(qr)
        rs = np.clip(r - wh // 2, 0, n_rows - wh)
        kabs = ks + np.arange(kr)
        vrow = (kabs[None, :] >= rs[:, None]) & (kabs[None, :] < rs[:, None] + wh)
        roff = np.clip(kabs[None, :] - r[:, None] + (wh - 1), 0, 2 * wh - 2)
        roffs.append(roff)
        valids.append(vrow)
    roff = np.stack(roffs)
    vrow = np.stack(valids)
    cols = np.arange(W)
    cs = np.clip(cols - ww // 2, 0, W - ww)
    vcol = (cols[None, :] >= cs[:, None]) & (cols[None, :] < cs[:, None] + ww)
    coff = np.clip(cols[None, :] - cols[:, None], -(ww - 1), ww - 1) + (ww - 1)
    n_ro, n_co = 2 * wh - 1, 2 * ww - 1
    onehot = (np.arange(n_co)[:, None] == coff.reshape(1, W * W)).astype(np.float32)
    H = rpb.shape[0]
    colb = jnp.dot(rpb.astype(F32).reshape(H * n_ro, n_co), jnp.asarray(onehot),
                   precision=lax.Precision.HIGHEST).reshape(H, n_ro, W, W)
    colb = jnp.where(vcol[None, None], colb, cfg.neg_inf)
    masked = jnp.full((H, W, W), cfg.neg_inf, F32)
    variants = []
    for v in range(3):
        rows = []
        for a in range(qr):
            tiles = [colb[:, int(roff[v, a, b])] if vrow[v, a, b] else masked for b in range(kr)]
            rows.append(jnp.concatenate(tiles, axis=-1))
        variants.append(jnp.concatenate(rows, axis=1))
    return jnp.stack(variants, axis=1)


def _na_attention(cfg, p, bias, offs):
    M = p.shape[0]
    W, hd = cfg.grid_w, cfg.head_dim
    RB = cfg.ctx_len + cfg.seq
    n_rows = _exact_div(cfg.seq, W)
    qr, kr = cfg.na_qrows, cfg.na_qrows + cfg.na_win_h
    n_t = _exact_div(n_rows, qr)
    qb = qr * W
    q0 = _exact_div(cfg.ctx_len, qb)
    tiles_b = _exact_div(RB, qb)
    cq, ck, cv = (_exact_div(offs[i], hd) for i in range(3))

    def q_map(b, h, t):
        return (b * tiles_b + q0 + t, cq + h)

    def var_map(b, h, t):
        return (h, jnp.where(t == 0, 0, jnp.where(t == n_t - 1, 2, 1)), 0, 0)

    kern = functools.partial(_na_kernel, ctx_len=cfg.ctx_len, grid_w=W, qrows=qr, krows=kr,
                             n_rows=n_rows, scale=float(hd) ** -0.5)
    vmem = 4 * RB * hd * 2 + 2 * qb * kr * W * 4 + 6 * qb * (kr * W + cfg.ctx_len) * 4
    return pl.pallas_call(
        kern,
        out_shape=jax.ShapeDtypeStruct((M, cfg.na_heads * hd), BF16),
        grid=(cfg.batch, cfg.na_heads, n_t),
        in_specs=[pl.BlockSpec((qb, hd), q_map),
                  pl.BlockSpec((RB, hd), lambda b, h, t: (b, ck + h)),
                  pl.BlockSpec((RB, hd), lambda b, h, t: (b, cv + h)),
                  pl.BlockSpec((None, None, qb, kr * W), var_map)],
        out_specs=pl.BlockSpec((qb, hd), lambda b, h, t: (b * tiles_b + q0 + t, h)),
        compiler_params=_params(vmem, 3),
        name="na_attention",
    )(p, p, p, bias)


def _gq_kernel(q_ref, k_ref, v_ref, qg_ref, kg_ref, cos_ref, sin_ref, o_ref, kn_ref, *,
               l, groups, hd, tq, kc, ctx_len, seq, eps, scale):
    qi = pl.program_id(2)

    @pl.when(qi == 0)
    def _():
        kn = _rms_rope(k_ref[...].astype(F32), kg_ref[l:l + 1, :], cos_ref[...], sin_ref[...], eps)
        kn_ref[...] = kn.astype(BF16)

    row0 = pl.multiple_of(ctx_len + qi * tq, 8)
    cos = cos_ref[pl.ds(row0, tq), :]
    sin = sin_ref[pl.ds(row0, tq), :]
    gain = qg_ref[l:l + 1, :] * scale
    qs = [_rms_rope(q_ref[:, g * hd:(g + 1) * hd].astype(F32), gain, cos, sin, eps).astype(BF16)
          for g in range(groups)]
    q = jnp.concatenate(qs, axis=0)
    bounds = [(0, ctx_len)] + [(ctx_len + c * kc, kc) for c in range(seq // kc)]
    o = _attend_chunks(q, [(kn_ref[s0:s0 + n, :], v_ref[s0:s0 + n, :], None) for s0, n in bounds])
    for g in range(groups):
        o_ref[:, g * hd:(g + 1) * hd] = o[g * tq:(g + 1) * tq, :].astype(o_ref.dtype)


def _rope_tables(cfg):
    hd = cfg.head_dim
    q = hd // 4
    t = np.arange(cfg.seq)
    inv = cfg.rope_theta ** (-np.arange(q, dtype=np.float64) / q)
    ang_r = (t // cfg.grid_w)[:, None] * inv[None, :]
    ang_c = (t % cfg.grid_w)[:, None] * inv[None, :]
    cos = np.concatenate([np.cos(ang_r), np.cos(ang_r), np.cos(ang_c), np.cos(ang_c)], axis=1)
    sin = np.concatenate([-np.sin(ang_r), np.sin(ang_r), -np.sin(ang_c), np.sin(ang_c)], axis=1)
    cos = np.concatenate([np.ones((cfg.ctx_len, hd)), cos], axis=0)
    sin = np.concatenate([np.zeros((cfg.ctx_len, hd)), sin], axis=0)
    return jnp.asarray(cos, F32), jnp.asarray(sin, F32)


def _gq_attention(cfg, p, q_gain, k_gain, cos, sin, l, offs):
    M = p.shape[0]
    hd = cfg.head_dim
    RB = cfg.ctx_len + cfg.seq
    G = _exact_div(cfg.gq_heads, cfg.gq_kv)
    tq = cfg.gq_tq
    nq = _exact_div(cfg.seq, tq)
    q0 = _exact_div(cfg.ctx_len, tq)
    tiles_b = _exact_div(RB, tq)
    cq = _exact_div(offs[3], G * hd)
    ck = _exact_div(offs[4], hd)
    cv = _exact_div(offs[5], hd)
    kc = cfg.gq_kc
    _exact_div(cfg.seq, kc)
    kern = functools.partial(_gq_kernel, l=l, groups=G, hd=hd, tq=tq, kc=kc, ctx_len=cfg.ctx_len,
                             seq=cfg.seq, eps=cfg.eps, scale=float(hd) ** -0.5)
    vmem = 4 * RB * hd * 2 + RB * hd * 2 + 4 * RB * hd * 4 + 2 * 2 * tq * G * hd * 2 + 6 * G * tq * kc * 4
    return pl.pallas_call(
        kern,
        out_shape=jax.ShapeDtypeStruct((M, cfg.gq_heads * hd), BF16),
        grid=(cfg.batch, cfg.gq_kv, nq),
        in_specs=[pl.BlockSpec((tq, G * hd), lambda b, k, i: (b * tiles_b + q0 + i, cq + k)),
                  pl.BlockSpec((RB, hd), lambda b, k, i: (b, ck + k)),
                  pl.BlockSpec((RB, hd), lambda b, k, i: (b, cv + k)),
                  pl.BlockSpec(q_gain.shape, lambda b, k, i: (0, 0)),
                  pl.BlockSpec(k_gain.shape, lambda b, k, i: (0, 0)),
                  pl.BlockSpec((RB, hd), lambda b, k, i: (0, 0)),
                  pl.BlockSpec((RB, hd), lambda b, k, i: (0, 0))],
        out_specs=pl.BlockSpec((tq, G * hd), lambda b, k, i: (b * tiles_b + q0 + i, k)),
        scratch_shapes=[pltpu.VMEM((RB, hd), BF16)],
        compiler_params=_params(vmem, 3),
        name="gq_attention",
    )(p, p, p, q_gain, k_gain, cos, sin)


def _ctx_attn_kernel(*refs, l, groups, hd, norm, eps, scale):
    if norm:
        q_ref, k_ref, v_ref, qg_ref, kg_ref, _, o_ref = refs
    else:
        q_ref, k_ref, v_ref, _, o_ref = refs
    k = k_ref[...]
    if norm:
        k = _rms_rope(k.astype(F32), kg_ref[l:l + 1, :], None, None, eps).astype(BF16)
    n = q_ref.shape[0]
    qs = []
    for g in range(groups):
        qg = q_ref[:, g * hd:(g + 1) * hd].astype(F32)
        if norm:
            qg = _rms_rope(qg, qg_ref[l:l + 1, :] * scale, None, None, eps)
        else:
            qg = qg * scale
        qs.append(qg.astype(BF16))
    q = jnp.concatenate(qs, axis=0)
    o = _attend_chunks(q, [(k, v_ref[...], None)])
    for g in range(groups):
        o_ref[:, g * hd:(g + 1) * hd] = o[g * n:(g + 1) * n, :].astype(o_ref.dtype)


def _ctx_attention(cfg, p, y_prev, n_kv, groups, col_q, col_k, col_v, l, gains=None):
    hd = cfg.head_dim
    C = cfg.ctx_len
    tiles_b = _exact_div(cfg.ctx_len + cfg.seq, C)
    cq = _exact_div(col_q, groups * hd)
    ck = _exact_div(col_k, hd)
    cv = _exact_div(col_v, hd)
    norm = gains is not None
    in_specs = [pl.BlockSpec((C, groups * hd), lambda b, k: (b * tiles_b, cq + k)),
                pl.BlockSpec((C, hd), lambda b, k: (b * tiles_b, ck + k)),
                pl.BlockSpec((C, hd), lambda b, k: (b * tiles_b, cv + k))]
    args = [p, p, p]
    if norm:
        in_specs += [pl.BlockSpec(gains[0].shape, lambda b, k: (0, 0)),
                     pl.BlockSpec(gains[1].shape, lambda b, k: (0, 0))]
        args += list(gains)
    in_specs.append(pl.BlockSpec(memory_space=pl.ANY))
    args.append(y_prev)
    kern = functools.partial(_ctx_attn_kernel, l=l, groups=groups, hd=hd, norm=norm, eps=cfg.eps,
                             scale=float(hd) ** -0.5)
    return pl.pallas_call(
        kern,
        out_shape=jax.ShapeDtypeStruct(y_prev.shape, y_prev.dtype),
        grid=(cfg.batch, n_kv),
        in_specs=in_specs,
        out_specs=pl.BlockSpec((C, groups * hd), lambda b, k: (b * tiles_b, k)),
        input_output_aliases={len(args) - 1: 0},
        compiler_params=_params(8 << 20, 2),
        name="ctx_attention",
    )(*args)


def _lru_kernel(x_ref, gate_ref, cw_ref, cb_ref, wr_ref, br_ref, wi_ref, bi_ref, lam_ref, o_ref,
                xpad_ref, af_ref, bf_ref, ab_ref, bb_ref, *, l, ctx_len, seq, chunk, lru_c, conv_w):
    cwid = x_ref.shape[1]
    nblk = cwid // V7X_LANES
    pad = V7X_SUBLANES
    ctx_base = pad
    lat_base = ctx_base + ctx_len + pad
    zeros = jnp.zeros((pad, cwid), F32)
    xpad_ref[0:pad, :] = zeros
    xpad_ref[ctx_base + ctx_len:lat_base, :] = zeros
    xpad_ref[lat_base + seq:lat_base + seq + pad, :] = zeros
    xpad_ref[ctx_base:ctx_base + ctx_len, :] = x_ref[0:ctx_len, :].astype(F32)
    xpad_ref[lat_base:lat_base + seq, :] = x_ref[ctx_len:ctx_len + seq, :].astype(F32)

    taps = cw_ref[...]
    cbias = cb_ref[l:l + 1, :]
    lam = lam_ref[...]
    nl = -lam
    softplus = jnp.maximum(nl, 0.0) + jnp.log1p(jnp.exp(-jnp.abs(nl)))
    a_refs = (af_ref, ab_ref)
    b_refs = (bf_ref, bb_ref)

    def coeffs(src_base, dst_row, rows):
        xc = cbias
        for tap in range(conv_w):
            xc = xc + taps[tap:tap + 1, :] * xpad_ref[pl.ds(src_base + tap - conv_w // 2, rows), :]
        xcb = xc.astype(BF16)
        for d in range(2):
            rs, is_ = [], []
            for j in range(nblk):
                xj = xcb[:, j * V7X_LANES:(j + 1) * V7X_LANES]
                rs.append(jnp.dot(xj, wr_ref[d, j].astype(BF16), preferred_element_type=F32))
                is_.append(jnp.dot(xj, wi_ref[d, j].astype(BF16), preferred_element_type=F32))
            gr = jax.nn.sigmoid(jnp.concatenate(rs, axis=1) + br_ref[d:d + 1, :])
            gi = jax.nn.sigmoid(jnp.concatenate(is_, axis=1) + bi_ref[d:d + 1, :])
            log_a = -lru_c * gr * softplus[d:d + 1, :]
            a = jnp.exp(log_a)
            a_refs[d][pl.ds(dst_row, rows), :] = a
            one_minus_a2 = -jnp.tanh(log_a) * (1.0 + a * a)
            b_refs[d][pl.ds(dst_row, rows), :] = jnp.sqrt(one_minus_a2) * (gi * xc)

    coeffs(ctx_base, 0, ctx_len)
    for c in range(seq // chunk):
        coeffs(lat_base + c * chunk, ctx_len + c * chunk, chunk)

    def scan_seg(first, n, h):
        def body(i, hs):
            hf, hb = hs
            tf = first + i
            tb = first + n - 1 - i
            hf = af_ref[pl.ds(tf, 1), :] * hf + bf_ref[pl.ds(tf, 1), :]
            bf_ref[pl.ds(tf, 1), :] = hf
            hb = ab_ref[pl.ds(tb, 1), :] * hb + bb_ref[pl.ds(tb, 1), :]
            bb_ref[pl.ds(tb, 1), :] = hb
            return hf, hb
        return lax.fori_loop(0, n, body, h, unroll=8)

    h0 = jnp.zeros((1, cwid), F32)
    hs = scan_seg(0, ctx_len, (h0, h0))
    scan_seg(ctx_len, seq, hs)

    def out_body(c, carry):
        r0 = pl.multiple_of(c * ctx_len, ctx_len)
        h = bf_ref[pl.ds(r0, ctx_len), :] + bb_ref[pl.ds(r0, ctx_len), :]
        g = gate_ref[pl.ds(r0, ctx_len), :].astype(F32)
        o_ref[pl.ds(r0, ctx_len), :] = (h * jax.nn.gelu(g)).astype(o_ref.dtype)
        return carry

    lax.fori_loop(0, (ctx_len + seq) // ctx_len, out_body, 0)


def _lru_mixer(cfg, p, conv_w, conv_b, w_r, b_r, w_i, b_i, lam, l, offs):
    M = p.shape[0]
    RB = cfg.ctx_len + cfg.seq
    cw = cfg.lru_cw
    ng = _exact_div(cfg.lru_width, cw)
    blk = _exact_div(cfg.lru_width, cfg.lru_blocks)
    assert blk == V7X_LANES
    nb = _exact_div(cw, blk)
    cx = _exact_div(offs[6], cw)
    cg = _exact_div(offs[7], cw)
    _exact_div(cfg.seq, cfg.lru_chunk)
    _exact_div(RB, cfg.ctx_len)
    kern = functools.partial(_lru_kernel, l=l, ctx_len=cfg.ctx_len, seq=cfg.seq, chunk=cfg.lru_chunk,
                             lru_c=cfg.lru_c, conv_w=cfg.conv_w)
    vmem = 5 * (RB + 32) * cw * 4 + 3 * 2 * RB * cw * 2 + 12 * cfg.lru_chunk * cw * 4 + (4 << 20)
    return pl.pallas_call(
        kern,
        out_shape=jax.ShapeDtypeStruct((M, cfg.lru_width), BF16),
        grid=(cfg.batch, ng),
        in_specs=[pl.BlockSpec((RB, cw), lambda b, g: (b, cx + g)),
                  pl.BlockSpec((RB, cw), lambda b, g: (b, cg + g)),
                  pl.BlockSpec((None, cfg.conv_w, cw), lambda b, g: (l, 0, g)),
                  pl.BlockSpec((conv_b.shape[0], cw), lambda b, g: (0, g)),
                  pl.BlockSpec((None, 2, nb, blk, blk), lambda b, g: (l, 0, g, 0, 0)),
                  pl.BlockSpec((None, 2, cw), lambda b, g: (l, 0, g)),
                  pl.BlockSpec((None, 2, nb, blk, blk), lambda b, g: (l, 0, g, 0, 0)),
                  pl.BlockSpec((None, 2, cw), lambda b, g: (l, 0, g)),
                  pl.BlockSpec((None, 2, cw), lambda b, g: (l, 0, g))],
        out_specs=pl.BlockSpec((RB, cw), lambda b, g: (b, g)),
        scratch_shapes=[pltpu.VMEM((RB + 3 * V7X_SUBLANES, cw), F32)] + [pltpu.VMEM((RB, cw), F32)] * 4,
        compiler_params=_params(vmem, 2),
        name="rglru_mixer",
    )(p, p, conv_w, conv_b, w_r, b_r, w_i, b_i, lam)


def _merge_kernel(ya_ref, yb_ref, yc_ref, wa_ref, wb_ref, wc_ref, g0_ref, g1_ref, g2_ref, o_ref,
                  wab_ref, wbb_ref, wcb_ref):
    @pl.when(pl.program_id(1) == 0)
    def _():
        wab_ref[...] = wa_ref[...].astype(BF16)
        wbb_ref[...] = wb_ref[...].astype(BF16)
        wcb_ref[...] = wc_ref[...].astype(BF16)

    acc = jax.nn.sigmoid(g0_ref[...].astype(F32)) * jnp.dot(ya_ref[...], wab_ref[...], preferred_element_type=F32)
    acc += jax.nn.sigmoid(g1_ref[...].astype(F32)) * jnp.dot(yb_ref[...], wbb_ref[...], preferred_element_type=F32)
    acc += jax.nn.sigmoid(g2_ref[...].astype(F32)) * jnp.dot(yc_ref[...], wcb_ref[...], preferred_element_type=F32)
    o_ref[...] = acc.astype(o_ref.dtype)


def _merge(cfg, ya, yb, yc, p, w_na, w_gq, w_lru, l, offs):
    M = ya.shape[0]
    D = cfg.d_model
    bm, bn = cfg.mm_bm, cfg.mm_bn
    nj = _exact_div(D, bn)
    gcol = _exact_div(offs[8], bn)
    ks = (ya.shape[1], yb.shape[1], yc.shape[1])
    ksum = sum(ks)
    vmem = 2 * bm * ksum * 2 + 2 * ksum * bn * 4 + ksum * bn * 2 + 3 * 2 * bm * bn * 2 + 2 * bm * bn * 2 + 4 * bm * bn * 4
    x_specs = [pl.BlockSpec((bm, k), lambda j, i: (i, 0)) for k in ks]
    w_specs = [pl.BlockSpec((None, k, bn), lambda j, i: (l, 0, j)) for k in ks]
    g_specs = [pl.BlockSpec((bm, bn), functools.partial(lambda j, i, s: (i, gcol + s * nj + j), s=s))
               for s in range(3)]
    return pl.pallas_call(
        _merge_kernel,
        out_shape=jax.ShapeDtypeStruct((M, D), BF16),
        grid=(nj, _exact_div(M, bm)),
        in_specs=x_specs + w_specs + g_specs,
        out_specs=pl.BlockSpec((bm, bn), lambda j, i: (i, j)),
        scratch_shapes=[pltpu.VMEM((k, bn), BF16) for k in ks],
        compiler_params=_params(vmem, 2),
        name="branch_merge",
    )(ya, yb, yc, w_na, w_gq, w_lru, p, p, p)


def _forward(cfg, x, c, ctx, c_ctx, w_ada, b_ada, w_in, na_rpb, gq_q_gain, gq_k_gain, lru_conv_w,
             lru_conv_b, lru_w_r, lru_b_r, lru_w_i, lru_b_i, lru_lambda, w_br_na, w_br_gq, w_br_lru,
             w_o, ln1_g, ln1_b, ln2_g, ln2_b, ffn_w1, ffn_w3, ffn_w2, moe_router, moe_w1, moe_w3,
             moe_w2):
    B, D = cfg.batch, cfg.d_model
    offs, widths = _splits(cfg)
    in_cols = offs[-1] + widths[-1]
    bm, bn = cfg.mm_bm, cfg.mm_bn

    x_all = jnp.concatenate([ctx, x], axis=1).reshape(B * (cfg.ctx_len + cfg.seq), D)
    cond = jnp.concatenate([c, c_ctx[None, :], jnp.zeros((V7X_SUBLANES - B - 1, D), F32)], axis=0)
    b_ada3 = b_ada.reshape(cfg.depth, 1, 6 * D)
    mods = [_ada_table(cond, w_ada, b_ada3, l).reshape(V7X_SUBLANES, 6, D) for l in range(cfg.depth)]
    cos, sin = _rope_tables(cfg)

    u = _modulate(cfg, x_all, mods[0], 0, 1)
    for l in range(cfg.depth):
        need_ctx = l < cfg.depth - 1
        mod = mods[l]
        p = _matmul(u, [w_in], (l,), bm=bm, bn=bn, out_dtype=BF16, name="in_proj")
        assert p.shape[1] == in_cols
        bias = _na_bias(cfg, na_rpb[l])
        ya = _na_attention(cfg, p, bias, offs)
        yb = _gq_attention(cfg, p, gq_q_gain, gq_k_gain, cos, sin, l, offs)
        if need_ctx:
            ya = _ctx_attention(cfg, p, ya, cfg.na_heads, 1, offs[0], offs[1], offs[2], l)
            yb = _ctx_attention(cfg, p, yb, cfg.gq_kv, cfg.gq_heads // cfg.gq_kv, offs[3], offs[4],
                                offs[5], l, gains=(gq_q_gain, gq_k_gain))
        yc = _lru_mixer(cfg, p, lru_conv_w, lru_conv_b, lru_w_r, lru_b_r, lru_w_i, lru_b_i,
                        lru_lambda, l, offs)
        merged = _merge(cfg, ya, yb, yc, p, w_br_na, w_br_gq, w_br_lru, l, offs)
        y = _matmul(merged, [w_o], (l,), bm=bm, bn=bn, out_dtype=F32, name="out_proj")

        dense = l % 2 == 0
        router = None
        if not dense:
            router = jnp.pad(moe_router[l // 2], ((0, 0), (0, V7X_LANES - cfg.n_experts)))
        res = _layer_norm(cfg, x_all, y, mod, 2, ln1_g, ln1_b, l, mod_next=mod, next_rows=(3, 4),
                          router=router)
        x_all, u2 = res[0], res[1]
        if dense:
            h = _matmul(u2, [ffn_w1, ffn_w3], (l // 2,), bm=bm, bn=cfg.glu_bn, out_dtype=BF16,
                        glu=True, name="ffn_up")
            w2 = ffn_w2[l // 2].astype(BF16)[None]
            f = _matmul(h, [w2], (0,), bm=cfg.down_bm, bn=bn, out_dtype=F32, name="ffn_down")
        else:
            gates = res[2]
            h = _matmul(u2, [moe_w1, moe_w3], (l // 2,), bm=bm, bn=cfg.glu_bn, out_dtype=BF16,
                        glu=True, gates=gates, n_experts=cfg.n_experts, name="moe_up")
            f = None
            for e in range(cfg.n_experts):
                f = _matmul(h, [moe_w2], (l // 2, e), k_blk=e, bm=bm, bn=bn, out_dtype=F32,
                            add=f, name="moe_down")
        if l + 1 < cfg.depth:
            x_all, u = _layer_norm(cfg, x_all, f, mod, 5, ln2_g, ln2_b, l, mod_next=mods[l + 1],
                                   next_rows=(0, 1))
        else:
            (x_all,) = _layer_norm(cfg, x_all, f, mod, 5, ln2_g, ln2_b, l)
    out = x_all.reshape(B, cfg.ctx_len + cfg.seq, D)[:, cfg.ctx_len:, :]
    return out


def kernel(x, c, ctx, c_ctx, w_ada, b_ada, w_in, na_rpb, gq_q_gain, gq_k_gain, lru_conv_w, lru_conv_b, lru_w_r, lru_b_r, lru_w_i, lru_b_i, lru_lambda, w_br_na, w_br_gq, w_br_lru, w_o, ln1_g, ln1_b, ln2_g, ln2_b, ffn_w1, ffn_w3, ffn_w2, moe_router, moe_w1, moe_w3, moe_w2):
    return _forward(FULL_CFG, x, c, ctx, c_ctx, w_ada, b_ada, w_in, na_rpb, gq_q_gain, gq_k_gain,
                    lru_conv_w, lru_conv_b, lru_w_r, lru_b_r, lru_w_i, lru_b_i, lru_lambda, w_br_na,
                    w_br_gq, w_br_lru, w_o, ln1_g, ln1_b, ln2_g, ln2_b, ffn_w1, ffn_w3, ffn_w2,
                    moe_router, moe_w1, moe_w3, moe_w2)
```

```python
import collections
import functools

import numpy as np
import jax
import jax.numpy as jnp
from jax import lax
from jax.experimental import pallas as pl
from jax.experimental.pallas import tpu as pltpu

F32 = jnp.float32
BF16 = jnp.bfloat16

V7X_LANES = 128
V7X_SUBLANES = 8
V7X_VMEM_BYTES = 64 * 1024 * 1024
VMEM_CAP_BYTES = V7X_VMEM_BYTES - 6 * 1024 * 1024

Cfg = collections.namedtuple(
    "Cfg",
    "d_model batch seq depth ctx_len grid_w head_dim na_heads na_win_h na_win_w gq_heads gq_kv "
    "rope_theta lru_width lru_blocks conv_w lru_c d_ff n_experts d_ff_expert eps neg_inf "
    "row_tile mm_bm mm_bn glu_bn merge_bm down_bm moe_bm moe_up_bn moe_down_bn na_qrows gq_tq gq_kc lru_cw "
    "lru_chunk")

FULL_CFG = Cfg(
    d_model=4096, batch=2, seq=4096, depth=2, ctx_len=256, grid_w=64, head_dim=128,
    na_heads=12, na_win_h=8, na_win_w=16, gq_heads=16, gq_kv=4, rope_theta=10000.0,
    lru_width=1536, lru_blocks=12, conv_w=4, lru_c=8.0, d_ff=11008, n_experts=8,
    d_ff_expert=3072, eps=1e-6, neg_inf=-1e30,
    row_tile=256, mm_bm=1088, mm_bn=512, glu_bn=256, merge_bm=512, down_bm=256, moe_bm=256, moe_up_bn=512,
    moe_down_bn=1024, na_qrows=4, gq_tq=256, gq_kc=1024, lru_cw=256, lru_chunk=512)


def _splits(cfg):
    na_w = cfg.na_heads * cfg.head_dim
    gq_q = cfg.gq_heads * cfg.head_dim
    gq_kv = cfg.gq_kv * cfg.head_dim
    widths = (na_w, na_w, na_w, gq_q, gq_kv, gq_kv, cfg.lru_width, cfg.lru_width, 3 * cfg.d_model)
    offs = tuple(int(v) for v in np.cumsum((0,) + widths[:-1]))
    return offs, widths


def _exact_div(a, b):
    assert a % b == 0, (a, b)
    return a // b


def _params(vmem_bytes, n_axes):
    limit = int(min(VMEM_CAP_BYTES, max(vmem_bytes * 5 // 4 + (4 << 20), 16 << 20)))
    return pltpu.CompilerParams(dimension_semantics=("arbitrary",) * n_axes, vmem_limit_bytes=limit)


def _dot_tile(x, wb_refs, glu):
    acc = jnp.dot(x, wb_refs[0][...], preferred_element_type=F32)
    if glu:
        acc = jax.nn.silu(acc) * jnp.dot(x, wb_refs[1][...], preferred_element_type=F32)
    return acc


def _mm_kernel(*refs, n_w, cast_w, glu):
    x_ref = refs[0]
    w_refs = refs[1:1 + n_w]
    o_ref = refs[1 + n_w]
    wb_refs = refs[2 + n_w:] if cast_w else w_refs

    if cast_w:
        @pl.when(pl.program_id(1) == 0)
        def _():
            for w_ref, wb_ref in zip(w_refs, wb_refs):
                wb_ref[...] = w_ref[...].astype(BF16)

    o_ref[...] = _dot_tile(x_ref[...], wb_refs, glu).astype(o_ref.dtype)


def _gmm_kernel(te_ref, nu_ref, *refs, n_w, glu):
    x_ref = refs[0]
    w_refs = refs[1:1 + n_w]
    o_ref = refs[1 + n_w]
    wb_refs = refs[2 + n_w:]
    i = pl.program_id(1)
    prev = te_ref[jnp.maximum(i - 1, 0)]

    @pl.when(jnp.logical_or(i == 0, te_ref[i] != prev))
    def _():
        for w_ref, wb_ref in zip(w_refs, wb_refs):
            wb_ref[...] = w_ref[...].astype(BF16)

    @pl.when(i < nu_ref[0])
    def _():
        o_ref[...] = _dot_tile(x_ref[...], wb_refs, glu).astype(o_ref.dtype)

    @pl.when(i >= nu_ref[0])
    def _():
        o_ref[...] = jnp.zeros(o_ref.shape, o_ref.dtype)


def _mm_vmem_bytes(n_w, bm, bn, K, wbytes, cast_w, out_dtype):
    return (2 * bm * K * 2 + n_w * (2 * K * bn * wbytes + (K * bn * 2 if cast_w else 0))
            + 2 * bm * bn * jnp.dtype(out_dtype).itemsize + 3 * bm * bn * 4)


def _matmul(x, ws, w_lead, *, bm, bn, out_dtype, glu=False, name="mm"):
    M, K = x.shape
    N = ws[0].shape[-1]
    assert ws[0].shape[-2] == K
    cast_w = ws[0].dtype != BF16
    lead = tuple(w_lead)
    in_specs = [pl.BlockSpec((bm, K), lambda j, i: (i, 0))]
    in_specs += [pl.BlockSpec((None,) * len(lead) + (K, bn), lambda j, i: lead + (0, j)) for _ in ws]
    scratch = [pltpu.VMEM((K, bn), BF16) for _ in ws] if cast_w else []
    vmem = _mm_vmem_bytes(len(ws), bm, bn, K, ws[0].dtype.itemsize, cast_w, out_dtype)
    return pl.pallas_call(
        functools.partial(_mm_kernel, n_w=len(ws), cast_w=cast_w, glu=glu),
        out_shape=jax.ShapeDtypeStruct((M, N), out_dtype),
        grid=(_exact_div(N, bn), _exact_div(M, bm)),
        in_specs=in_specs,
        out_specs=pl.BlockSpec((bm, bn), lambda j, i: (i, j)),
        scratch_shapes=scratch,
        compiler_params=_params(vmem, 2),
        name=name,
    )(x, *ws)


def _grouped_matmul(x, ws, l, tile_expert, n_used, *, bm, bn, out_dtype, glu=False, name="gmm"):
    R, K = x.shape
    N = ws[0].shape[-1]
    assert ws[0].shape[-2] == K
    in_specs = [pl.BlockSpec((bm, K), lambda j, i, te, nu: (i, 0))]
    in_specs += [pl.BlockSpec((None, None, K, bn), lambda j, i, te, nu: (l, te[i], 0, j)) for _ in ws]
    vmem = _mm_vmem_bytes(len(ws), bm, bn, K, 4, True, out_dtype)
    grid_spec = pltpu.PrefetchScalarGridSpec(
        num_scalar_prefetch=2,
        grid=(_exact_div(N, bn), _exact_div(R, bm)),
        in_specs=in_specs,
        out_specs=pl.BlockSpec((bm, bn), lambda j, i, te, nu: (i, j)),
        scratch_shapes=[pltpu.VMEM((K, bn), BF16) for _ in ws])
    return pl.pallas_call(
        functools.partial(_gmm_kernel, n_w=len(ws), glu=glu),
        out_shape=jax.ShapeDtypeStruct((R, N), out_dtype),
        grid_spec=grid_spec,
        compiler_params=_params(vmem, 2),
        name=name,
    )(tile_expert, n_used, x, *ws)


def _row_copy(src_hbm, row, dst_vmem, r, sem):
    return pltpu.make_async_copy(src_hbm.at[pl.ds(row, 1), :], dst_vmem.at[pl.ds(r, 1), :], sem)


def _gather_kernel(src_ref, x_hbm, o_ref, buf_ref, sem):
    bm = o_ref.shape[0]

    def issue(r, c):
        _row_copy(x_hbm, src_ref[0, r], buf_ref, r, sem).start()
        return c

    def wait(r, c):
        _row_copy(x_hbm, 0, buf_ref, r, sem).wait()
        return c

    lax.fori_loop(0, bm, issue, 0, unroll=8)
    lax.fori_loop(0, bm, wait, 0, unroll=8)
    o_ref[...] = buf_ref[...].astype(o_ref.dtype)


def _gather_rows(x, src3, out_dtype):
    n_tiles, _, bm = src3.shape
    D = x.shape[1]
    vmem = bm * D * 4 + 2 * bm * D * jnp.dtype(out_dtype).itemsize + 2 * bm * D * 4
    return pl.pallas_call(
        _gather_kernel,
        out_shape=jax.ShapeDtypeStruct((n_tiles * bm, D), out_dtype),
        grid=(n_tiles,),
        in_specs=[pl.BlockSpec((None, 1, bm), lambda t: (t, 0, 0), memory_space=pltpu.SMEM),
                  pl.BlockSpec(memory_space=pl.ANY)],
        out_specs=pl.BlockSpec((bm, D), lambda t: (t, 0)),
        scratch_shapes=[pltpu.VMEM((bm, D), x.dtype), pltpu.SemaphoreType.DMA],
        compiler_params=_params(vmem, 1),
        name="moe_gather",
    )(src3, x)


def _combine_kernel(p1_ref, p2_ref, top_ref, y_hbm, o_ref, a_ref, b_ref, sems):
    bm = o_ref.shape[0]

    def issue(r, c):
        _row_copy(y_hbm, p1_ref[0, r], a_ref, r, sems.at[0]).start()
        _row_copy(y_hbm, p2_ref[0, r], b_ref, r, sems.at[1]).start()
        return c

    def wait(r, c):
        _row_copy(y_hbm, 0, a_ref, r, sems.at[0]).wait()
        _row_copy(y_hbm, 0, b_ref, r, sems.at[1]).wait()
        return c

    lax.fori_loop(0, bm, issue, 0, unroll=8)
    lax.fori_loop(0, bm, wait, 0, unroll=8)
    top = top_ref[...]
    o_ref[...] = top[:, 2:3] * a_ref[...] + top[:, 3:4] * b_ref[...]


def _combine(ys, pos1, pos2, top):
    n_tiles, _, bm = pos1.shape
    D = ys.shape[1]
    M = n_tiles * bm
    vmem = 2 * bm * D * 4 + 2 * bm * D * 4 + 2 * bm * D * 4
    idx_spec = pl.BlockSpec((None, 1, bm), lambda t: (t, 0, 0), memory_space=pltpu.SMEM)
    return pl.pallas_call(
        _combine_kernel,
        out_shape=jax.ShapeDtypeStruct((M, D), F32),
        grid=(n_tiles,),
        in_specs=[idx_spec, idx_spec,
                  pl.BlockSpec((bm, V7X_LANES), lambda t: (t, 0)),
                  pl.BlockSpec(memory_space=pl.ANY)],
        out_specs=pl.BlockSpec((bm, D), lambda t: (t, 0)),
        scratch_shapes=[pltpu.VMEM((bm, D), ys.dtype), pltpu.VMEM((bm, D), ys.dtype),
                        pltpu.SemaphoreType.DMA((2,))],
        compiler_params=_params(vmem, 1),
        name="moe_combine",
    )(pos1, pos2, top, ys)


def _route(cfg, top):
    M = top.shape[0]
    E, bm = cfg.n_experts, cfg.moe_bm
    P = 2 * M
    n_tiles = _exact_div(P, bm) + E
    ef = jnp.clip(top[:, :2].astype(jnp.int32), 0, E - 1).reshape(P)
    onehot = (ef[:, None] == jnp.arange(E, dtype=jnp.int32)[None, :]).astype(jnp.int32)
    csum = jnp.cumsum(onehot, axis=0)
    rank = jnp.sum((csum - 1) * onehot, axis=1)
    n_e = csum[-1]
    padded = ((n_e + bm - 1) // bm) * bm
    gstart = jnp.cumsum(padded) - padded
    ustart = jnp.cumsum(n_e) - n_e
    dest = (gstart[ef] + rank).reshape(M, 2)
    n_used = (jnp.sum(padded) // bm).astype(jnp.int32).reshape(1)
    tstart = jnp.arange(n_tiles, dtype=jnp.int32) * bm
    gend = gstart + padded
    te = jnp.sum((tstart[:, None] >= gend[None, :]).astype(jnp.int32), axis=1)
    te = jnp.minimum(te, E - 1)
    order = jnp.argsort(ef, stable=True).astype(jnp.int32)
    rows = jnp.arange(n_tiles * bm, dtype=jnp.int32)
    e_r = te[rows // bm]
    k = rows - gstart[e_r]
    s = jnp.clip(ustart[e_r] + k, 0, P - 1)
    src = jnp.where(k < n_e[e_r], order[s] // 2, 0)
    tiles_m = _exact_div(M, bm)
    return dict(te=te, n_used=n_used, src3=src.reshape(n_tiles, 1, bm),
                pos1=dest[:, 0].reshape(tiles_m, 1, bm), pos2=dest[:, 1].reshape(tiles_m, 1, bm))


def _ada_kernel(c_ref, w_ref, b_ref, o_ref):
    c = c_ref[...]
    s = (c * jax.nn.sigmoid(c)).astype(BF16)
    acc = jnp.dot(s, w_ref[...].astype(BF16), preferred_element_type=F32)
    o_ref[...] = acc + b_ref[...]


def _ada_table(cond, w_ada, b_ada3, l, bn=1024):
    R, D = cond.shape
    N = w_ada.shape[-1]
    vmem = 2 * D * bn * 4 + D * bn * 2 + 4 * R * bn * 4 + 2 * R * D * 4
    return pl.pallas_call(
        _ada_kernel,
        out_shape=jax.ShapeDtypeStruct((R, N), F32),
        grid=(_exact_div(N, bn),),
        in_specs=[pl.BlockSpec((R, D), lambda j: (0, 0)),
                  pl.BlockSpec((None, D, bn), lambda j: (l, 0, j)),
                  pl.BlockSpec((None, 1, bn), lambda j: (l, 0, j))],
        out_specs=pl.BlockSpec((R, bn), lambda j: (0, j)),
        compiler_params=_params(vmem, 1),
        name="ada_table",
    )(cond, w_ada, b_ada3)


def _mod_sel(cfg):
    tiles_per_batch = _exact_div(cfg.ctx_len + cfg.seq, cfg.row_tile)
    ctx_tiles = _exact_div(cfg.ctx_len, cfg.row_tile)

    def sel(i):
        return jnp.where(i % tiles_per_batch < ctx_tiles, cfg.batch, i // tiles_per_batch)
    return sel


def _modulate_kernel(x_ref, m_ref, u_ref, *, shift_row, scale_row):
    m = m_ref[...]
    u = x_ref[...] * (1.0 + m[scale_row:scale_row + 1, :]) + m[shift_row:shift_row + 1, :]
    u_ref[...] = u.astype(u_ref.dtype)


def _modulate(cfg, x_all, modtab, shift_row, scale_row):
    M, D = x_all.shape
    rt = cfg.row_tile
    sel = _mod_sel(cfg)
    vmem = 2 * rt * D * 4 + 2 * rt * D * 2 + 2 * 8 * D * 4 + 2 * rt * D * 4
    return pl.pallas_call(
        functools.partial(_modulate_kernel, shift_row=shift_row, scale_row=scale_row),
        out_shape=jax.ShapeDtypeStruct((M, D), BF16),
        grid=(_exact_div(M, rt),),
        in_specs=[pl.BlockSpec((rt, D), lambda i: (i, 0)),
                  pl.BlockSpec((None, 6, D), lambda i: (sel(i), 0, 0))],
        out_specs=pl.BlockSpec((rt, D), lambda i: (i, 0)),
        compiler_params=_params(vmem, 1),
        name="modulate",
    )(x_all, modtab)


def _ln_kernel(*refs, alpha, eps, gate_row, l, next_rows, n_experts):
    it = iter(refs)
    x_ref, y_ref, mc_ref = next(it), next(it), next(it)
    mn_ref = next(it) if next_rows is not None else None
    g_ref, b_ref = next(it), next(it)
    r_ref = next(it) if n_experts else None
    xo_ref = next(it)
    u_ref = next(it) if next_rows is not None else None
    go_ref = next(it) if n_experts else None

    gate = mc_ref[gate_row:gate_row + 1, :]
    z = alpha * x_ref[...] + gate * y_ref[...].astype(F32)
    mu = jnp.mean(z, axis=-1, keepdims=True)
    zc = z - mu
    var = jnp.mean(zc * zc, axis=-1, keepdims=True)
    xn = zc * lax.rsqrt(var + eps) * g_ref[l:l + 1, :] + b_ref[l:l + 1, :]
    xo_ref[...] = xn
    if next_rows is not None:
        shift_row, scale_row = next_rows
        mn = mn_ref[...]
        u = xn * (1.0 + mn[scale_row:scale_row + 1, :]) + mn[shift_row:shift_row + 1, :]
        u_ref[...] = u.astype(u_ref.dtype)
        if n_experts:
            logits = jnp.dot(u, r_ref[...], preferred_element_type=F32,
                             precision=lax.Precision.HIGHEST)
            lane = lax.broadcasted_iota(jnp.int32, logits.shape, 1)
            neg = jnp.float32(-jnp.inf)
            lg = jnp.where(lane < n_experts, logits, neg)
            m1 = jnp.max(lg, axis=-1, keepdims=True)
            i1 = jnp.min(jnp.where(lg == m1, lane, V7X_LANES), axis=-1, keepdims=True)
            lg2 = jnp.where(lane == i1, neg, lg)
            m2 = jnp.max(lg2, axis=-1, keepdims=True)
            i2 = jnp.min(jnp.where(lg2 == m2, lane, V7X_LANES), axis=-1, keepdims=True)
            e2 = jnp.exp(m2 - m1)
            w1 = 1.0 / (1.0 + e2)
            w2 = e2 / (1.0 + e2)
            packed = jnp.where(lane == 0, i1.astype(F32), jnp.where(lane == 1, i2.astype(F32), 0.0))
            go_ref[...] = packed + jnp.where(lane == 2, w1, 0.0) + jnp.where(lane == 3, w2, 0.0)


def _layer_norm(cfg, x_all, y, mod_cur, gate_row, ln_g, ln_b, l, mod_next=None, next_rows=None,
                router=None, u_dtype=BF16):
    M, D = x_all.shape
    rt = cfg.row_tile
    sel = _mod_sel(cfg)
    alpha = float((2 * cfg.depth) ** 0.25)
    n_experts = cfg.n_experts if router is not None else 0
    row_spec = pl.BlockSpec((rt, D), lambda i: (i, 0))
    mod_spec = pl.BlockSpec((None, 6, D), lambda i: (sel(i), 0, 0))
    par_spec = pl.BlockSpec(ln_g.shape, lambda i: (0, 0))
    in_specs = [row_spec, row_spec, mod_spec]
    args = [x_all, y, mod_cur]
    if next_rows is not None:
        in_specs.append(mod_spec)
        args.append(mod_next)
    in_specs += [par_spec, par_spec]
    args += [ln_g, ln_b]
    out_shape = [jax.ShapeDtypeStruct((M, D), F32)]
    out_specs = [row_spec]
    if next_rows is not None:
        out_shape.append(jax.ShapeDtypeStruct((M, D), u_dtype))
        out_specs.append(row_spec)
    if n_experts:
        in_specs.append(pl.BlockSpec(router.shape, lambda i: (0, 0)))
        args.append(router)
        out_shape.append(jax.ShapeDtypeStruct((M, V7X_LANES), F32))
        out_specs.append(pl.BlockSpec((rt, V7X_LANES), lambda i: (i, 0)))
    vmem = 2 * rt * D * (4 + 4 + 4 + 2) + 4 * 8 * D * 4 + 4 * rt * D * 4 + (2 * D * 128 * 4 * 4 if n_experts else 0)
    kern = functools.partial(_ln_kernel, alpha=alpha, eps=cfg.eps, gate_row=gate_row, l=l,
                             next_rows=next_rows, n_experts=n_experts)
    return pl.pallas_call(
        kern,
        out_shape=out_shape,
        grid=(_exact_div(M, rt),),
        in_specs=in_specs,
        out_specs=out_specs,
        compiler_params=_params(vmem, 1),
        name="deepnorm_ln",
    )(*args)


def _rms_rope(x, gain, cos, sin, eps):
    xn = x * lax.rsqrt(jnp.mean(x * x, axis=-1, keepdims=True) + eps) * gain
    if cos is None:
        return xn
    d = x.shape[-1]
    q = d // 4
    lane = lax.broadcasted_iota(jnp.int32, xn.shape, 1)
    fwd = pltpu.roll(xn, d - q, axis=1)
    bwd = pltpu.roll(xn, q, axis=1)
    swapped = jnp.where(lane % (2 * q) < q, fwd, bwd)
    return xn * cos + swapped * sin


def _dot_nt(a, b):
    return lax.dot_general(a, b, (((1,), (1,)), ((), ())), preferred_element_type=F32)


def _attend_chunks(q, chunks):
    m = l = acc = None
    for k, v, bias in chunks:
        s = _dot_nt(q, k)
        if bias is not None:
            s = s + bias
        cm = jnp.max(s, axis=-1, keepdims=True)
        if m is None:
            m_new = cm
            p = jnp.exp(s - m_new)
            l = jnp.sum(p, axis=-1, keepdims=True)
            acc = jnp.dot(p.astype(BF16), v, preferred_element_type=F32)
        else:
            m_new = jnp.maximum(m, cm)
            a = jnp.exp(m - m_new)
            p = jnp.exp(s - m_new)
            l = a * l + jnp.sum(p, axis=-1, keepdims=True)
            acc = a * acc + jnp.dot(p.astype(BF16), v, preferred_element_type=F32)
        m = m_new
    return acc / l


def _na_kernel(q_ref, k_ref, v_ref, bias_ref, _, o_ref, *, ctx_len, grid_w, qrows, krows, n_rows, scale):
    t = pl.program_id(2)
    ks = jnp.clip(qrows * t - qrows, 0, n_rows - krows)
    start = pl.multiple_of(ctx_len + ks * grid_w, 16)
    nk = krows * grid_w
    q = (q_ref[...].astype(F32) * scale).astype(BF16)
    k_win = k_ref[pl.ds(start, nk), :]
    v_win = v_ref[pl.ds(start, nk), :]
    k_ctx = k_ref[0:ctx_len, :]
    v_ctx = v_ref[0:ctx_len, :]
    o = _attend_chunks(q, [(k_win, v_win, bias_ref[...]), (k_ctx, v_ctx, None)])
    o_ref[...] = o.astype(o_ref.dtype)


def _na_bias(cfg, rpb):
    W = cfg.grid_w
    n_rows = _exact_div(cfg.seq, W)
    qr, kr = cfg.na_qrows, cfg.na_qrows + cfg.na_win_h
    n_t = _exact_div(n_rows, qr)
    wh, ww = cfg.na_win_h, cfg.na_win_w
    roffs, valids = [], []
    for t in (0, 1, n_t - 1):
        ks = int(np.clip(qr * t - qr, 0, n_rows - kr))
        r = qr * t + np.arange(qr)
        rs = np.clip(r - wh // 2, 0, n_rows - wh)
        kabs = ks + np.arange(kr)
        vrow = (kabs[None, :] >= rs[:, None]) & (kabs[None, :] < rs[:, None] + wh)
        roff = np.clip(kabs[None, :] - r[:, None] + (wh - 1), 0, 2 * wh - 2)
        roffs.append(roff)
        valids.append(vrow)
    roff = np.stack(roffs)
    vrow = np.stack(valids)
    cols = np.arange(W)
    cs = np.clip(cols - ww // 2, 0, W - ww)
    vcol = (cols[None, :] >= cs[:, None]) & (cols[None, :] < cs[:, None] + ww)
    coff = np.clip(cols[None, :] - cols[:, None], -(ww - 1), ww - 1) + (ww - 1)
    n_ro, n_co = 2 * wh - 1, 2 * ww - 1
    onehot = (np.arange(n_co)[:, None] == coff.reshape(1, W * W)).astype(np.float32)
    H = rpb.shape[0]
    colb = jnp.dot(rpb.astype(F32).reshape(H * n_ro, n_co), jnp.asarray(onehot),
                   precision=lax.Precision.HIGHEST).reshape(H, n_ro, W, W)
    colb = jnp.where(vcol[None, None], colb, cfg.neg_inf)
    masked = jnp.full((H, W, W), cfg.neg_inf, F32)
    variants = []
    for v in range(3):
        rows = []
        for a in range(qr):
            tiles = [colb[:, int(roff[v, a, b])] if vrow[v, a, b] else masked for b in range(kr)]
            rows.append(jnp.concatenate(tiles, axis=-1))
        variants.append(jnp.concatenate(rows, axis=1))
    return jnp.stack(variants, axis=1)


def _na_attention(cfg, p, bias, offs):
    M = p.shape[0]
    W, hd = cfg.grid_w, cfg.head_dim
    RB = cfg.ctx_len + cfg.seq
    n_rows = _exact_div(cfg.seq, W)
    qr, kr = cfg.na_qrows, cfg.na_qrows + cfg.na_win_h
    n_t = _exact_div(n_rows, qr)
    qb = qr * W
    q0 = _exact_div(cfg.ctx_len, qb)
    tiles_b = _exact_div(RB, qb)
    cq, ck, cv = (_exact_div(offs[i], hd) for i in range(3))

    def q_map(b, h, t):
        return (b * tiles_b + q0 + t, cq + h)

    def var_map(b, h, t):
        return (h, jnp.where(t == 0, 0, jnp.where(t == n_t - 1, 2, 1)), 0, 0)

    kern = functools.partial(_na_kernel, ctx_len=cfg.ctx_len, grid_w=W, qrows=qr, krows=kr,
                             n_rows=n_rows, scale=float(hd) ** -0.5)
    vmem = 4 * RB * hd * 2 + 2 * qb * kr * W * 4 + 6 * qb * (kr * W + cfg.ctx_len) * 4
    return pl.pallas_call(
        kern,
        out_shape=jax.ShapeDtypeStruct((M, cfg.na_heads * hd), BF16),
        grid=(cfg.batch, cfg.na_heads, n_t),
        in_specs=[pl.BlockSpec((qb, hd), q_map),
                  pl.BlockSpec((RB, hd), lambda b, h, t: (b, ck + h)),
                  pl.BlockSpec((RB, hd), lambda b, h, t: (b, cv + h)),
                  pl.BlockSpec((None, None, qb, kr * W), var_map),
                  pl.BlockSpec(memory_space=pl.ANY)],
        out_specs=pl.BlockSpec((qb, hd), lambda b, h, t: (b * tiles_b + q0 + t, h)),
        input_output_aliases={4: 0},
        compiler_params=_params(vmem, 3),
        name="na_attention",
    )(p, p, p, bias, jnp.zeros((M, cfg.na_heads * hd), BF16))


def _gq_kernel(q_ref, k_ref, v_ref, qg_ref, kg_ref, cos_ref, sin_ref, _, o_ref, kn_ref, *,
               l, groups, hd, tq, kc, ctx_len, seq, eps, scale):
    qi = pl.program_id(2)

    @pl.when(qi == 0)
    def _():
        kn = _rms_rope(k_ref[...].astype(F32), kg_ref[l:l + 1, :], cos_ref[...], sin_ref[...], eps)
        kn_ref[...] = kn.astype(BF16)

    row0 = pl.multiple_of(ctx_len + qi * tq, 8)
    cos = cos_ref[pl.ds(row0, tq), :]
    sin = sin_ref[pl.ds(row0, tq), :]
    gain = qg_ref[l:l + 1, :] * scale
    qs = [_rms_rope(q_ref[:, g * hd:(g + 1) * hd].astype(F32), gain, cos, sin, eps).astype(BF16)
          for g in range(groups)]
    q = jnp.concatenate(qs, axis=0)
    bounds = [(0, ctx_len)] + [(ctx_len + c * kc, kc) for c in range(seq // kc)]
    o = _attend_chunks(q, [(kn_ref[s0:s0 + n, :], v_ref[s0:s0 + n, :], None) for s0, n in bounds])
    for g in range(groups):
        o_ref[:, g * hd:(g + 1) * hd] = o[g * tq:(g + 1) * tq, :].astype(o_ref.dtype)


def _rope_tables(cfg):
    hd = cfg.head_dim
    q = hd // 4
    t = np.arange(cfg.seq)
    inv = cfg.rope_theta ** (-np.arange(q, dtype=np.float64) / q)
    ang_r = (t // cfg.grid_w)[:, None] * inv[None, :]
    ang_c = (t % cfg.grid_w)[:, None] * inv[None, :]
    cos = np.concatenate([np.cos(ang_r), np.cos(ang_r), np.cos(ang_c), np.cos(ang_c)], axis=1)
    sin = np.concatenate([-np.sin(ang_r), np.sin(ang_r), -np.sin(ang_c), np.sin(ang_c)], axis=1)
    cos = np.concatenate([np.ones((cfg.ctx_len, hd)), cos], axis=0)
    sin = np.concatenate([np.zeros((cfg.ctx_len, hd)), sin], axis=0)
    return jnp.asarray(cos, F32), jnp.asarray(sin, F32)


def _gq_attention(cfg, p, q_gain, k_gain, cos, sin, l, offs):
    M = p.shape[0]
    hd = cfg.head_dim
    RB = cfg.ctx_len + cfg.seq
    G = _exact_div(cfg.gq_heads, cfg.gq_kv)
    tq = cfg.gq_tq
    nq = _exact_div(cfg.seq, tq)
    q0 = _exact_div(cfg.ctx_len, tq)
    tiles_b = _exact_div(RB, tq)
    cq = _exact_div(offs[3], G * hd)
    ck = _exact_div(offs[4], hd)
    cv = _exact_div(offs[5], hd)
    kc = cfg.gq_kc
    _exact_div(cfg.seq, kc)
    kern = functools.partial(_gq_kernel, l=l, groups=G, hd=hd, tq=tq, kc=kc, ctx_len=cfg.ctx_len,
                             seq=cfg.seq, eps=cfg.eps, scale=float(hd) ** -0.5)
    vmem = 4 * RB * hd * 2 + RB * hd * 2 + 4 * RB * hd * 4 + 2 * 2 * tq * G * hd * 2 + 6 * G * tq * kc * 4
    return pl.pallas_call(
        kern,
        out_shape=jax.ShapeDtypeStruct((M, cfg.gq_heads * hd), BF16),
        grid=(cfg.batch, cfg.gq_kv, nq),
        in_specs=[pl.BlockSpec((tq, G * hd), lambda b, k, i: (b * tiles_b + q0 + i, cq + k)),
                  pl.BlockSpec((RB, hd), lambda b, k, i: (b, ck + k)),
                  pl.BlockSpec((RB, hd), lambda b, k, i: (b, cv + k)),
                  pl.BlockSpec(q_gain.shape, lambda b, k, i: (0, 0)),
                  pl.BlockSpec(k_gain.shape, lambda b, k, i: (0, 0)),
                  pl.BlockSpec((RB, hd), lambda b, k, i: (0, 0)),
                  pl.BlockSpec((RB, hd), lambda b, k, i: (0, 0)),
                  pl.BlockSpec(memory_space=pl.ANY)],
        out_specs=pl.BlockSpec((tq, G * hd), lambda b, k, i: (b * tiles_b + q0 + i, k)),
        scratch_shapes=[pltpu.VMEM((RB, hd), BF16)],
        input_output_aliases={7: 0},
        compiler_params=_params(vmem, 3),
        name="gq_attention",
    )(p, p, p, q_gain, k_gain, cos, sin, jnp.zeros((M, cfg.gq_heads * hd), BF16))


def _ctx_attn_kernel(*refs, l, groups, hd, norm, eps, scale):
    if norm:
        q_ref, k_ref, v_ref, qg_ref, kg_ref, _, o_ref = refs
    else:
        q_ref, k_ref, v_ref, _, o_ref = refs
    k = k_ref[...]
    if norm:
        k = _rms_rope(k.astype(F32), kg_ref[l:l + 1, :], None, None, eps).astype(BF16)
    n = q_ref.shape[0]
    qs = []
    for g in range(groups):
        qg = q_ref[:, g * hd:(g + 1) * hd].astype(F32)
        if norm:
            qg = _rms_rope(qg, qg_ref[l:l + 1, :] * scale, None, None, eps)
        else:
            qg = qg * scale
        qs.append(qg.astype(BF16))
    q = jnp.concatenate(qs, axis=0)
    o = _attend_chunks(q, [(k, v_ref[...], None)])
    for g in range(groups):
        o_ref[:, g * hd:(g + 1) * hd] = o[g * n:(g + 1) * n, :].astype(o_ref.dtype)


def _ctx_attention(cfg, p, y_prev, n_kv, groups, col_q, col_k, col_v, l, gains=None):
    hd = cfg.head_dim
    C = cfg.ctx_len
    tiles_b = _exact_div(cfg.ctx_len + cfg.seq, C)
    cq = _exact_div(col_q, groups * hd)
    ck = _exact_div(col_k, hd)
    cv = _exact_div(col_v, hd)
    norm = gains is not None
    in_specs = [pl.BlockSpec((C, groups * hd), lambda b, k: (b * tiles_b, cq + k)),
                pl.BlockSpec((C, hd), lambda b, k: (b * tiles_b, ck + k)),
                pl.BlockSpec((C, hd), lambda b, k: (b * tiles_b, cv + k))]
    args = [p, p, p]
    if norm:
        in_specs += [pl.BlockSpec(gains[0].shape, lambda b, k: (0, 0)),
                     pl.BlockSpec(gains[1].shape, lambda b, k: (0, 0))]
        args += list(gains)
    in_specs.append(pl.BlockSpec(memory_space=pl.ANY))
    args.append(y_prev)
    kern = functools.partial(_ctx_attn_kernel, l=l, groups=groups, hd=hd, norm=norm, eps=cfg.eps,
                             scale=float(hd) ** -0.5)
    return pl.pallas_call(
        kern,
        out_shape=jax.ShapeDtypeStruct(y_prev.shape, y_prev.dtype),
        grid=(cfg.batch, n_kv),
        in_specs=in_specs,
        out_specs=pl.BlockSpec((C, groups * hd), lambda b, k: (b * tiles_b, k)),
        input_output_aliases={len(args) - 1: 0},
        compiler_params=_params(8 << 20, 2),
        name="ctx_attention",
    )(*args)


def _lru_kernel(x_ref, gate_ref, cw_ref, cb_ref, wr_ref, br_ref, wi_ref, bi_ref, lam_ref, o_ref,
                xpad_ref, af_ref, bf_ref, ab_ref, bb_ref, *, l, ctx_len, seq, chunk, lru_c, conv_w):
    cwid = x_ref.shape[1]
    nblk = cwid // V7X_LANES
    pad = V7X_SUBLANES
    ctx_base = pad
    lat_base = ctx_base + ctx_len + pad
    zeros = jnp.zeros((pad, cwid), F32)
    xpad_ref[0:pad, :] = zeros
    xpad_ref[ctx_base + ctx_len:lat_base, :] = zeros
    xpad_ref[lat_base + seq:lat_base + seq + pad, :] = zeros
    xpad_ref[ctx_base:ctx_base + ctx_len, :] = x_ref[0:ctx_len, :].astype(F32)
    xpad_ref[lat_base:lat_base + seq, :] = x_ref[ctx_len:ctx_len + seq, :].astype(F32)

    taps = cw_ref[...]
    cbias = cb_ref[l:l + 1, :]
    lam = lam_ref[...]
    nl = -lam
    softplus = jnp.maximum(nl, 0.0) + jnp.log1p(jnp.exp(-jnp.abs(nl)))
    a_refs = (af_ref, ab_ref)
    b_refs = (bf_ref, bb_ref)

    def coeffs(src_base, dst_row, rows):
        xc = cbias
        for tap in range(conv_w):
            xc = xc + taps[tap:tap + 1, :] * xpad_ref[pl.ds(src_base + tap - conv_w // 2, rows), :]
        xcb = xc.astype(BF16)
        for d in range(2):
            rs, is_ = [], []
            for j in range(nblk):
                xj = xcb[:, j * V7X_LANES:(j + 1) * V7X_LANES]
                rs.append(jnp.dot(xj, wr_ref[d, j].astype(BF16), preferred_element_type=F32))
                is_.append(jnp.dot(xj, wi_ref[d, j].astype(BF16), preferred_element_type=F32))
            gr = jax.nn.sigmoid(jnp.concatenate(rs, axis=1) + br_ref[d:d + 1, :])
            gi = jax.nn.sigmoid(jnp.concatenate(is_, axis=1) + bi_ref[d:d + 1, :])
            log_a = -lru_c * gr * softplus[d:d + 1, :]
            a = jnp.exp(log_a)
            a_refs[d][pl.ds(dst_row, rows), :] = a
            one_minus_a2 = -jnp.tanh(log_a) * (1.0 + a * a)
            b_refs[d][pl.ds(dst_row, rows), :] = jnp.sqrt(one_minus_a2) * (gi * xc)

    coeffs(ctx_base, 0, ctx_len)
    for c in range(seq // chunk):
        coeffs(lat_base + c * chunk, ctx_len + c * chunk, chunk)

    def scan_seg(first, n, h):
        def body(i, hs):
            hf, hb = hs
            tf = first + i
            tb = first + n - 1 - i
            hf = af_ref[pl.ds(tf, 1), :] * hf + bf_ref[pl.ds(tf, 1), :]
            bf_ref[pl.ds(tf, 1), :] = hf
            hb = ab_ref[pl.ds(tb, 1), :] * hb + bb_ref[pl.ds(tb, 1), :]
            bb_ref[pl.ds(tb, 1), :] = hb
            return hf, hb
        return lax.fori_loop(0, n, body, h, unroll=8)

    h0 = jnp.zeros((1, cwid), F32)
    hs = scan_seg(0, ctx_len, (h0, h0))
    scan_seg(ctx_len, seq, hs)

    def out_body(c, carry):
        r0 = pl.multiple_of(c * ctx_len, ctx_len)
        h = bf_ref[pl.ds(r0, ctx_len), :] + bb_ref[pl.ds(r0, ctx_len), :]
        g = gate_ref[pl.ds(r0, ctx_len), :].astype(F32)
        o_ref[pl.ds(r0, ctx_len), :] = (h * jax.nn.gelu(g)).astype(o_ref.dtype)
        return carry

    lax.fori_loop(0, (ctx_len + seq) // ctx_len, out_body, 0)


def _lru_mixer(cfg, p, conv_w, conv_b, w_r, b_r, w_i, b_i, lam, l, offs):
    M = p.shape[0]
    RB = cfg.ctx_len + cfg.seq
    cw = cfg.lru_cw
    ng = _exact_div(cfg.lru_width, cw)
    blk = _exact_div(cfg.lru_width, cfg.lru_blocks)
    assert blk == V7X_LANES
    nb = _exact_div(cw, blk)
    cx = _exact_div(offs[6], cw)
    cg = _exact_div(offs[7], cw)
    _exact_div(cfg.seq, cfg.lru_chunk)
    _exact_div(RB, cfg.ctx_len)
    kern = functools.partial(_lru_kernel, l=l, ctx_len=cfg.ctx_len, seq=cfg.seq, chunk=cfg.lru_chunk,
                             lru_c=cfg.lru_c, conv_w=cfg.conv_w)
    vmem = 5 * (RB + 32) * cw * 4 + 3 * 2 * RB * cw * 2 + 12 * cfg.lru_chunk * cw * 4 + (4 << 20)
    return pl.pallas_call(
        kern,
        out_shape=jax.ShapeDtypeStruct((M, cfg.lru_width), BF16),
        grid=(cfg.batch, ng),
        in_specs=[pl.BlockSpec((RB, cw), lambda b, g: (b, cx + g)),
                  pl.BlockSpec((RB, cw), lambda b, g: (b, cg + g)),
                  pl.BlockSpec((None, cfg.conv_w, cw), lambda b, g: (l, 0, g)),
                  pl.BlockSpec((conv_b.shape[0], cw), lambda b, g: (0, g)),
                  pl.BlockSpec((None, 2, nb, blk, blk), lambda b, g: (l, 0, g, 0, 0)),
                  pl.BlockSpec((None, 2, cw), lambda b, g: (l, 0, g)),
                  pl.BlockSpec((None, 2, nb, blk, blk), lambda b, g: (l, 0, g, 0, 0)),
                  pl.BlockSpec((None, 2, cw), lambda b, g: (l, 0, g)),
                  pl.BlockSpec((None, 2, cw), lambda b, g: (l, 0, g))],
        out_specs=pl.BlockSpec((RB, cw), lambda b, g: (b, g)),
        scratch_shapes=[pltpu.VMEM((RB + 3 * V7X_SUBLANES, cw), F32)] + [pltpu.VMEM((RB, cw), F32)] * 4,
        compiler_params=_params(vmem, 2),
        name="rglru_mixer",
    )(p, p, conv_w, conv_b, w_r, b_r, w_i, b_i, lam)


def _merge_kernel(ya_ref, yb_ref, yc_ref, wa_ref, wb_ref, wc_ref, g0_ref, g1_ref, g2_ref, o_ref,
                  wab_ref, wbb_ref, wcb_ref):
    @pl.when(pl.program_id(1) == 0)
    def _():
        wab_ref[...] = wa_ref[...].astype(BF16)
        wbb_ref[...] = wb_ref[...].astype(BF16)
        wcb_ref[...] = wc_ref[...].astype(BF16)

    acc = jax.nn.sigmoid(g0_ref[...].astype(F32)) * jnp.dot(ya_ref[...], wab_ref[...], preferred_element_type=F32)
    acc += jax.nn.sigmoid(g1_ref[...].astype(F32)) * jnp.dot(yb_ref[...], wbb_ref[...], preferred_element_type=F32)
    acc += jax.nn.sigmoid(g2_ref[...].astype(F32)) * jnp.dot(yc_ref[...], wcb_ref[...], preferred_element_type=F32)
    o_ref[...] = acc.astype(o_ref.dtype)


def _merge(cfg, ya, yb, yc, p, w_na, w_gq, w_lru, l, offs):
    M = ya.shape[0]
    D = cfg.d_model
    bm, bn = cfg.merge_bm, cfg.mm_bn
    nj = _exact_div(D, bn)
    gcol = _exact_div(offs[8], bn)
    ks = (ya.shape[1], yb.shape[1], yc.shape[1])
    ksum = sum(ks)
    vmem = 2 * bm * ksum * 2 + 2 * ksum * bn * 4 + ksum * bn * 2 + 3 * 2 * bm * bn * 2 + 2 * bm * bn * 2 + 4 * bm * bn * 4
    x_specs = [pl.BlockSpec((bm, k), lambda j, i: (i, 0)) for k in ks]
    w_specs = [pl.BlockSpec((None, k, bn), lambda j, i: (l, 0, j)) for k in ks]
    g_specs = [pl.BlockSpec((bm, bn), functools.partial(lambda j, i, s: (i, gcol + s * nj + j), s=s))
               for s in range(3)]
    return pl.pallas_call(
        _merge_kernel,
        out_shape=jax.ShapeDtypeStruct((M, D), BF16),
        grid=(nj, _exact_div(M, bm)),
        in_specs=x_specs + w_specs + g_specs,
        out_specs=pl.BlockSpec((bm, bn), lambda j, i: (i, j)),
        scratch_shapes=[pltpu.VMEM((k, bn), BF16) for k in ks],
        compiler_params=_params(vmem, 2),
        name="branch_merge",
    )(ya, yb, yc, w_na, w_gq, w_lru, p, p, p)


def _forward(cfg, x, c, ctx, c_ctx, w_ada, b_ada, w_in, na_rpb, gq_q_gain, gq_k_gain, lru_conv_w,
             lru_conv_b, lru_w_r, lru_b_r, lru_w_i, lru_b_i, lru_lambda, w_br_na, w_br_gq, w_br_lru,
             w_o, ln1_g, ln1_b, ln2_g, ln2_b, ffn_w1, ffn_w3, ffn_w2, moe_router, moe_w1, moe_w3,
             moe_w2):
    B, D = cfg.batch, cfg.d_model
    offs, widths = _splits(cfg)
    in_cols = offs[-1] + widths[-1]
    bm, bn = cfg.mm_bm, cfg.mm_bn

    x_all = jnp.concatenate([ctx, x], axis=1).reshape(B * (cfg.ctx_len + cfg.seq), D)
    cond = jnp.concatenate([c, c_ctx[None, :], jnp.zeros((V7X_SUBLANES - B - 1, D), F32)], axis=0)
    b_ada3 = b_ada.reshape(cfg.depth, 1, 6 * D)
    mods = [_ada_table(cond, w_ada, b_ada3, l).reshape(V7X_SUBLANES, 6, D) for l in range(cfg.depth)]
    cos, sin = _rope_tables(cfg)

    u = _modulate(cfg, x_all, mods[0], 0, 1)
    for l in range(cfg.depth):
        need_ctx = l < cfg.depth - 1
        mod = mods[l]
        p = _matmul(u, [w_in], (l,), bm=bm, bn=bn, out_dtype=BF16, name="in_proj")
        assert p.shape[1] == in_cols
        bias = _na_bias(cfg, na_rpb[l])
        ya = _na_attention(cfg, p, bias, offs)
        yb = _gq_attention(cfg, p, gq_q_gain, gq_k_gain, cos, sin, l, offs)
        if need_ctx:
            ya = _ctx_attention(cfg, p, ya, cfg.na_heads, 1, offs[0], offs[1], offs[2], l)
            yb = _ctx_attention(cfg, p, yb, cfg.gq_kv, cfg.gq_heads // cfg.gq_kv, offs[3], offs[4],
                                offs[5], l, gains=(gq_q_gain, gq_k_gain))
        yc = _lru_mixer(cfg, p, lru_conv_w, lru_conv_b, lru_w_r, lru_b_r, lru_w_i, lru_b_i,
                        lru_lambda, l, offs)
        merged = _merge(cfg, ya, yb, yc, p, w_br_na, w_br_gq, w_br_lru, l, offs)
        y = _matmul(merged, [w_o], (l,), bm=bm, bn=bn, out_dtype=F32, name="out_proj")

        dense = l % 2 == 0
        router = None
        if not dense:
            router = jnp.pad(moe_router[l // 2], ((0, 0), (0, V7X_LANES - cfg.n_experts)))
        res = _layer_norm(cfg, x_all, y, mod, 2, ln1_g, ln1_b, l, mod_next=mod, next_rows=(3, 4),
                          router=router, u_dtype=BF16 if dense else F32)
        x_all, u2 = res[0], res[1]
        if dense:
            h = _matmul(u2, [ffn_w1, ffn_w3], (l // 2,), bm=bm, bn=cfg.glu_bn, out_dtype=BF16,
                        glu=True, name="ffn_up")
            w2 = ffn_w2[l // 2].astype(BF16)[None]
            f = _matmul(h, [w2], (0,), bm=cfg.down_bm, bn=bn, out_dtype=F32, name="ffn_down")
        else:
            top = res[2]
            rt = _route(cfg, top)
            xs = _gather_rows(u2, rt["src3"], BF16)
            h = _grouped_matmul(xs, [moe_w1, moe_w3], l // 2, rt["te"], rt["n_used"], bm=cfg.moe_bm,
                                bn=cfg.moe_up_bn, out_dtype=BF16, glu=True, name="moe_up")
            ys = _grouped_matmul(h, [moe_w2], l // 2, rt["te"], rt["n_used"], bm=cfg.moe_bm,
                                 bn=cfg.moe_down_bn, out_dtype=F32, name="moe_down")
            f = _combine(ys, rt["pos1"], rt["pos2"], top)
        if l + 1 < cfg.depth:
            x_all, u = _layer_norm(cfg, x_all, f, mod, 5, ln2_g, ln2_b, l, mod_next=mods[l + 1],
                                   next_rows=(0, 1))
        else:
            (x_all,) = _layer_norm(cfg, x_all, f, mod, 5, ln2_g, ln2_b, l)
    out = x_all.reshape(B, cfg.ctx_len + cfg.seq, D)[:, cfg.ctx_len:, :]
    return out


def kernel(x, c, ctx, c_ctx, w_ada, b_ada, w_in, na_rpb, gq_q_gain, gq_k_gain, lru_conv_w, lru_conv_b, lru_w_r, lru_b_r, lru_w_i, lru_b_i, lru_lambda, w_br_na, w_br_gq, w_br_lru, w_o, ln1_g, ln1_b, ln2_g, ln2_b, ffn_w1, ffn_w3, ffn_w2, moe_router, moe_w1, moe_w3, moe_w2):
    return _forward(FULL_CFG, x, c, ctx, c_ctx, w_ada, b_ada, w_in, na_rpb, gq_q_gain, gq_k_gain,
                    lru_conv_w, lru_conv_b, lru_w_r, lru_b_r, lru_w_i, lru_b_i, lru_lambda, w_br_na,
                    w_br_gq, w_br_lru, w_o, ln1_g, ln1_b, ln2_g, ln2_b, ffn_w1, ffn_w3, ffn_w2,
                    moe_router, moe_w1, moe_w3, moe_w2)
```

```python
import collections
import functools

import numpy as np
import jax
import jax.numpy as jnp
from jax import lax
from jax.experimental import pallas as pl
from jax.experimental.pallas import tpu as pltpu

F32 = jnp.float32
BF16 = jnp.bfloat16
LOG2_E = 1.4426950408889634

V7X_LANES = 128
V7X_SUBLANES = 8
V7X_VMEM_BYTES = 64 * 1024 * 1024
VMEM_CAP_BYTES = V7X_VMEM_BYTES - 6 * 1024 * 1024

Cfg = collections.namedtuple(
    "Cfg",
    "d_model batch seq depth ctx_len grid_w head_dim na_heads na_win_h na_win_w gq_heads gq_kv "
    "rope_theta lru_width lru_blocks conv_w lru_c d_ff n_experts d_ff_expert eps neg_inf "
    "row_tile mm_bm mm_bn glu_bn merge_bm down_bm moe_bm moe_up_bn moe_down_bn na_qrows na_hps gq_tq gq_kc lru_cw "
    "lru_chunk")

FULL_CFG = Cfg(
    d_model=4096, batch=2, seq=4096, depth=2, ctx_len=256, grid_w=64, head_dim=128,
    na_heads=12, na_win_h=8, na_win_w=16, gq_heads=16, gq_kv=4, rope_theta=10000.0,
    lru_width=1536, lru_blocks=12, conv_w=4, lru_c=8.0, d_ff=11008, n_experts=8,
    d_ff_expert=3072, eps=1e-6, neg_inf=-1e30,
    row_tile=256, mm_bm=1088, mm_bn=512, glu_bn=256, merge_bm=512, down_bm=256, moe_bm=256, moe_up_bn=512,
    moe_down_bn=1024, na_qrows=4, na_hps=2, gq_tq=256, gq_kc=1024, lru_cw=256, lru_chunk=512)


def _splits(cfg):
    na_w = cfg.na_heads * cfg.head_dim
    gq_q = cfg.gq_heads * cfg.head_dim
    gq_kv = cfg.gq_kv * cfg.head_dim
    widths = (na_w, na_w, na_w, gq_q, gq_kv, gq_kv, cfg.lru_width, cfg.lru_width, 3 * cfg.d_model)
    offs = tuple(int(v) for v in np.cumsum((0,) + widths[:-1]))
    return offs, widths


def _exact_div(a, b):
    assert a % b == 0, (a, b)
    return a // b


def _params(vmem_bytes, n_axes):
    limit = int(min(VMEM_CAP_BYTES, max(vmem_bytes * 5 // 4 + (4 << 20), 16 << 20)))
    return pltpu.CompilerParams(dimension_semantics=("arbitrary",) * n_axes, vmem_limit_bytes=limit)


def _dot_tile(x, wb_refs, glu):
    acc = jnp.dot(x, wb_refs[0][...], preferred_element_type=F32)
    if glu:
        acc = jax.nn.silu(acc) * jnp.dot(x, wb_refs[1][...], preferred_element_type=F32)
    return acc


def _mm_kernel(*refs, n_w, cast_w, glu):
    x_ref = refs[0]
    w_refs = refs[1:1 + n_w]
    o_ref = refs[1 + n_w]
    wb_refs = refs[2 + n_w:] if cast_w else w_refs

    if cast_w:
        @pl.when(pl.program_id(1) == 0)
        def _():
            for w_ref, wb_ref in zip(w_refs, wb_refs):
                wb_ref[...] = w_ref[...].astype(BF16)

    o_ref[...] = _dot_tile(x_ref[...], wb_refs, glu).astype(o_ref.dtype)


def _gmm_kernel(te_ref, nu_ref, *refs, n_w, glu):
    x_ref = refs[0]
    w_refs = refs[1:1 + n_w]
    o_ref = refs[1 + n_w]
    wb_refs = refs[2 + n_w:]
    i = pl.program_id(1)
    prev = te_ref[jnp.maximum(i - 1, 0)]

    @pl.when(jnp.logical_or(i == 0, te_ref[i] != prev))
    def _():
        for w_ref, wb_ref in zip(w_refs, wb_refs):
            wb_ref[...] = w_ref[...].astype(BF16)

    @pl.when(i < nu_ref[0])
    def _():
        o_ref[...] = _dot_tile(x_ref[...], wb_refs, glu).astype(o_ref.dtype)

    @pl.when(i >= nu_ref[0])
    def _():
        o_ref[...] = jnp.zeros(o_ref.shape, o_ref.dtype)


def _mm_vmem_bytes(n_w, bm, bn, K, wbytes, cast_w, out_dtype):
    return (2 * bm * K * 2 + n_w * (2 * K * bn * wbytes + (K * bn * 2 if cast_w else 0))
            + 2 * bm * bn * jnp.dtype(out_dtype).itemsize + 3 * bm * bn * 4)


def _matmul(x, ws, w_lead, *, bm, bn, out_dtype, glu=False, name="mm"):
    M, K = x.shape
    N = ws[0].shape[-1]
    assert ws[0].shape[-2] == K
    cast_w = ws[0].dtype != BF16
    lead = tuple(w_lead)
    in_specs = [pl.BlockSpec((bm, K), lambda j, i: (i, 0))]
    in_specs += [pl.BlockSpec((None,) * len(lead) + (K, bn), lambda j, i: lead + (0, j)) for _ in ws]
    scratch = [pltpu.VMEM((K, bn), BF16) for _ in ws] if cast_w else []
    vmem = _mm_vmem_bytes(len(ws), bm, bn, K, ws[0].dtype.itemsize, cast_w, out_dtype)
    return pl.pallas_call(
        functools.partial(_mm_kernel, n_w=len(ws), cast_w=cast_w, glu=glu),
        out_shape=jax.ShapeDtypeStruct((M, N), out_dtype),
        grid=(_exact_div(N, bn), _exact_div(M, bm)),
        in_specs=in_specs,
        out_specs=pl.BlockSpec((bm, bn), lambda j, i: (i, j)),
        scratch_shapes=scratch,
        compiler_params=_params(vmem, 2),
        name=name,
    )(x, *ws)


def _grouped_matmul(x, ws, l, tile_expert, n_used, *, bm, bn, out_dtype, glu=False, name="gmm"):
    R, K = x.shape
    N = ws[0].shape[-1]
    assert ws[0].shape[-2] == K
    in_specs = [pl.BlockSpec((bm, K), lambda j, i, te, nu: (i, 0))]
    in_specs += [pl.BlockSpec((None, None, K, bn), lambda j, i, te, nu: (l, te[i], 0, j)) for _ in ws]
    vmem = _mm_vmem_bytes(len(ws), bm, bn, K, 4, True, out_dtype)
    grid_spec = pltpu.PrefetchScalarGridSpec(
        num_scalar_prefetch=2,
        grid=(_exact_div(N, bn), _exact_div(R, bm)),
        in_specs=in_specs,
        out_specs=pl.BlockSpec((bm, bn), lambda j, i, te, nu: (i, j)),
        scratch_shapes=[pltpu.VMEM((K, bn), BF16) for _ in ws])
    return pl.pallas_call(
        functools.partial(_gmm_kernel, n_w=len(ws), glu=glu),
        out_shape=jax.ShapeDtypeStruct((R, N), out_dtype),
        grid_spec=grid_spec,
        compiler_params=_params(vmem, 2),
        name=name,
    )(tile_expert, n_used, x, *ws)


def _row_copy(src_hbm, row, dst_vmem, r, sem):
    return pltpu.make_async_copy(src_hbm.at[pl.ds(row, 1), :], dst_vmem.at[pl.ds(r, 1), :], sem)


def _gather_kernel(src_ref, x_hbm, o_ref, buf_ref, sem):
    bm = o_ref.shape[0]

    def issue(r, c):
        _row_copy(x_hbm, src_ref[0, r], buf_ref, r, sem).start()
        return c

    def wait(r, c):
        _row_copy(x_hbm, 0, buf_ref, r, sem).wait()
        return c

    lax.fori_loop(0, bm, issue, 0, unroll=8)
    lax.fori_loop(0, bm, wait, 0, unroll=8)
    o_ref[...] = buf_ref[...].astype(o_ref.dtype)


def _gather_rows(x, src3, out_dtype):
    n_tiles, _, bm = src3.shape
    D = x.shape[1]
    vmem = bm * D * 4 + 2 * bm * D * jnp.dtype(out_dtype).itemsize + 2 * bm * D * 4
    return pl.pallas_call(
        _gather_kernel,
        out_shape=jax.ShapeDtypeStruct((n_tiles * bm, D), out_dtype),
        grid=(n_tiles,),
        in_specs=[pl.BlockSpec((None, 1, bm), lambda t: (t, 0, 0), memory_space=pltpu.SMEM),
                  pl.BlockSpec(memory_space=pl.ANY)],
        out_specs=pl.BlockSpec((bm, D), lambda t: (t, 0)),
        scratch_shapes=[pltpu.VMEM((bm, D), x.dtype), pltpu.SemaphoreType.DMA],
        compiler_params=_params(vmem, 1),
        name="moe_gather",
    )(src3, x)


def _combine_kernel(p1_ref, p2_ref, top_ref, y_hbm, o_ref, a_ref, b_ref, sems):
    bm = o_ref.shape[0]

    def issue(r, c):
        _row_copy(y_hbm, p1_ref[0, r], a_ref, r, sems.at[0]).start()
        _row_copy(y_hbm, p2_ref[0, r], b_ref, r, sems.at[1]).start()
        return c

    def wait(r, c):
        _row_copy(y_hbm, 0, a_ref, r, sems.at[0]).wait()
        _row_copy(y_hbm, 0, b_ref, r, sems.at[1]).wait()
        return c

    lax.fori_loop(0, bm, issue, 0, unroll=8)
    lax.fori_loop(0, bm, wait, 0, unroll=8)
    top = top_ref[...]
    o_ref[...] = top[:, 2:3] * a_ref[...] + top[:, 3:4] * b_ref[...]


def _combine(ys, pos1, pos2, top):
    n_tiles, _, bm = pos1.shape
    D = ys.shape[1]
    M = n_tiles * bm
    vmem = 2 * bm * D * 4 + 2 * bm * D * 4 + 2 * bm * D * 4
    idx_spec = pl.BlockSpec((None, 1, bm), lambda t: (t, 0, 0), memory_space=pltpu.SMEM)
    return pl.pallas_call(
        _combine_kernel,
        out_shape=jax.ShapeDtypeStruct((M, D), F32),
        grid=(n_tiles,),
        in_specs=[idx_spec, idx_spec,
                  pl.BlockSpec((bm, V7X_LANES), lambda t: (t, 0)),
                  pl.BlockSpec(memory_space=pl.ANY)],
        out_specs=pl.BlockSpec((bm, D), lambda t: (t, 0)),
        scratch_shapes=[pltpu.VMEM((bm, D), ys.dtype), pltpu.VMEM((bm, D), ys.dtype),
                        pltpu.SemaphoreType.DMA((2,))],
        compiler_params=_params(vmem, 1),
        name="moe_combine",
    )(pos1, pos2, top, ys)


def _route(cfg, top):
    M = top.shape[0]
    E, bm = cfg.n_experts, cfg.moe_bm
    P = 2 * M
    n_tiles = _exact_div(P, bm) + E
    ef = jnp.clip(top[:, :2].astype(jnp.int32), 0, E - 1).reshape(P)
    onehot = (ef[:, None] == jnp.arange(E, dtype=jnp.int32)[None, :]).astype(jnp.int32)
    csum = jnp.cumsum(onehot, axis=0)
    rank = jnp.sum((csum - 1) * onehot, axis=1)
    n_e = csum[-1]
    padded = ((n_e + bm - 1) // bm) * bm
    gstart = jnp.cumsum(padded) - padded
    ustart = jnp.cumsum(n_e) - n_e
    dest = (gstart[ef] + rank).reshape(M, 2)
    n_used = (jnp.sum(padded) // bm).astype(jnp.int32).reshape(1)
    tstart = jnp.arange(n_tiles, dtype=jnp.int32) * bm
    gend = gstart + padded
    te = jnp.sum((tstart[:, None] >= gend[None, :]).astype(jnp.int32), axis=1)
    te = jnp.minimum(te, E - 1)
    order = jnp.argsort(ef, stable=True).astype(jnp.int32)
    rows = jnp.arange(n_tiles * bm, dtype=jnp.int32)
    e_r = te[rows // bm]
    k = rows - gstart[e_r]
    s = jnp.clip(ustart[e_r] + k, 0, P - 1)
    src = jnp.where(k < n_e[e_r], order[s] // 2, 0)
    tiles_m = _exact_div(M, bm)
    return dict(te=te, n_used=n_used, src3=src.reshape(n_tiles, 1, bm),
                pos1=dest[:, 0].reshape(tiles_m, 1, bm), pos2=dest[:, 1].reshape(tiles_m, 1, bm))


def _ada_kernel(c_ref, w_ref, b_ref, o_ref):
    c = c_ref[...]
    s = (c * jax.nn.sigmoid(c)).astype(BF16)
    acc = jnp.dot(s, w_ref[...].astype(BF16), preferred_element_type=F32)
    o_ref[...] = acc + b_ref[...]


def _ada_table(cond, w_ada, b_ada3, l, bn=1024):
    R, D = cond.shape
    N = w_ada.shape[-1]
    vmem = 2 * D * bn * 4 + D * bn * 2 + 4 * R * bn * 4 + 2 * R * D * 4
    return pl.pallas_call(
        _ada_kernel,
        out_shape=jax.ShapeDtypeStruct((R, N), F32),
        grid=(_exact_div(N, bn),),
        in_specs=[pl.BlockSpec((R, D), lambda j: (0, 0)),
                  pl.BlockSpec((None, D, bn), lambda j: (l, 0, j)),
                  pl.BlockSpec((None, 1, bn), lambda j: (l, 0, j))],
        out_specs=pl.BlockSpec((R, bn), lambda j: (0, j)),
        compiler_params=_params(vmem, 1),
        name="ada_table",
    )(cond, w_ada, b_ada3)


def _mod_sel(cfg):
    tiles_per_batch = _exact_div(cfg.ctx_len + cfg.seq, cfg.row_tile)
    ctx_tiles = _exact_div(cfg.ctx_len, cfg.row_tile)

    def sel(i):
        return jnp.where(i % tiles_per_batch < ctx_tiles, cfg.batch, i // tiles_per_batch)
    return sel


def _modulate_kernel(x_ref, m_ref, u_ref, *, shift_row, scale_row):
    m = m_ref[...]
    u = x_ref[...] * (1.0 + m[scale_row:scale_row + 1, :]) + m[shift_row:shift_row + 1, :]
    u_ref[...] = u.astype(u_ref.dtype)


def _modulate(cfg, x_all, modtab, shift_row, scale_row):
    M, D = x_all.shape
    rt = cfg.row_tile
    sel = _mod_sel(cfg)
    vmem = 2 * rt * D * 4 + 2 * rt * D * 2 + 2 * 8 * D * 4 + 2 * rt * D * 4
    return pl.pallas_call(
        functools.partial(_modulate_kernel, shift_row=shift_row, scale_row=scale_row),
        out_shape=jax.ShapeDtypeStruct((M, D), BF16),
        grid=(_exact_div(M, rt),),
        in_specs=[pl.BlockSpec((rt, D), lambda i: (i, 0)),
                  pl.BlockSpec((None, 6, D), lambda i: (sel(i), 0, 0))],
        out_specs=pl.BlockSpec((rt, D), lambda i: (i, 0)),
        compiler_params=_params(vmem, 1),
        name="modulate",
    )(x_all, modtab)


def _ln_kernel(*refs, alpha, eps, gate_row, l, next_rows, n_experts):
    it = iter(refs)
    x_ref, y_ref, mc_ref = next(it), next(it), next(it)
    mn_ref = next(it) if next_rows is not None else None
    g_ref, b_ref = next(it), next(it)
    r_ref = next(it) if n_experts else None
    xo_ref = next(it)
    u_ref = next(it) if next_rows is not None else None
    go_ref = next(it) if n_experts else None

    gate = mc_ref[gate_row:gate_row + 1, :]
    z = alpha * x_ref[...] + gate * y_ref[...].astype(F32)
    mu = jnp.mean(z, axis=-1, keepdims=True)
    zc = z - mu
    var = jnp.mean(zc * zc, axis=-1, keepdims=True)
    xn = zc * lax.rsqrt(var + eps) * g_ref[l:l + 1, :] + b_ref[l:l + 1, :]
    xo_ref[...] = xn
    if next_rows is not None:
        shift_row, scale_row = next_rows
        mn = mn_ref[...]
        u = xn * (1.0 + mn[scale_row:scale_row + 1, :]) + mn[shift_row:shift_row + 1, :]
        u_ref[...] = u.astype(u_ref.dtype)
        if n_experts:
            logits = jnp.dot(u, r_ref[...], preferred_element_type=F32,
                             precision=lax.Precision.HIGHEST)
            lane = lax.broadcasted_iota(jnp.int32, logits.shape, 1)
            neg = jnp.float32(-jnp.inf)
            lg = jnp.where(lane < n_experts, logits, neg)
            m1 = jnp.max(lg, axis=-1, keepdims=True)
            i1 = jnp.min(jnp.where(lg == m1, lane, V7X_LANES), axis=-1, keepdims=True)
            lg2 = jnp.where(lane == i1, neg, lg)
            m2 = jnp.max(lg2, axis=-1, keepdims=True)
            i2 = jnp.min(jnp.where(lg2 == m2, lane, V7X_LANES), axis=-1, keepdims=True)
            e2 = jnp.exp(m2 - m1)
            w1 = 1.0 / (1.0 + e2)
            w2 = e2 / (1.0 + e2)
            packed = jnp.where(lane == 0, i1.astype(F32), jnp.where(lane == 1, i2.astype(F32), 0.0))
            go_ref[...] = packed + jnp.where(lane == 2, w1, 0.0) + jnp.where(lane == 3, w2, 0.0)


def _layer_norm(cfg, x_all, y, mod_cur, gate_row, ln_g, ln_b, l, mod_next=None, next_rows=None,
                router=None, u_dtype=BF16, latent_only=False):
    M, D = x_all.shape
    rt = cfg.row_tile
    alpha = float((2 * cfg.depth) ** 0.25)
    n_experts = cfg.n_experts if router is not None else 0
    if latent_only:
        assert next_rows is None and router is None
        lat_tiles = _exact_div(cfg.seq, rt)
        tiles_b = _exact_div(cfg.ctx_len + cfg.seq, rt)
        ctx_tiles = _exact_div(cfg.ctx_len, rt)
        n_tiles = cfg.batch * lat_tiles
        in_row = pl.BlockSpec((rt, D), lambda i: ((i // lat_tiles) * tiles_b + ctx_tiles + i % lat_tiles, 0))
        mod_spec = pl.BlockSpec((None, 6, D), lambda i: (i // lat_tiles, 0, 0))
        out_rows = cfg.batch * cfg.seq
    else:
        sel = _mod_sel(cfg)
        n_tiles = _exact_div(M, rt)
        in_row = pl.BlockSpec((rt, D), lambda i: (i, 0))
        mod_spec = pl.BlockSpec((None, 6, D), lambda i: (sel(i), 0, 0))
        out_rows = M
    row_spec = pl.BlockSpec((rt, D), lambda i: (i, 0))
    par_spec = pl.BlockSpec(ln_g.shape, lambda i: (0, 0))
    in_specs = [in_row, in_row, mod_spec]
    args = [x_all, y, mod_cur]
    if next_rows is not None:
        in_specs.append(mod_spec)
        args.append(mod_next)
    in_specs += [par_spec, par_spec]
    args += [ln_g, ln_b]
    out_shape = [jax.ShapeDtypeStruct((out_rows, D), F32)]
    out_specs = [row_spec]
    if next_rows is not None:
        out_shape.append(jax.ShapeDtypeStruct((M, D), u_dtype))
        out_specs.append(row_spec)
    if n_experts:
        in_specs.append(pl.BlockSpec(router.shape, lambda i: (0, 0)))
        args.append(router)
        out_shape.append(jax.ShapeDtypeStruct((M, V7X_LANES), F32))
        out_specs.append(pl.BlockSpec((rt, V7X_LANES), lambda i: (i, 0)))
    vmem = 2 * rt * D * (4 + 4 + 4 + 2) + 4 * 8 * D * 4 + 4 * rt * D * 4 + (2 * D * 128 * 4 * 4 if n_experts else 0)
    kern = functools.partial(_ln_kernel, alpha=alpha, eps=cfg.eps, gate_row=gate_row, l=l,
                             next_rows=next_rows, n_experts=n_experts)
    return pl.pallas_call(
        kern,
        out_shape=out_shape,
        grid=(n_tiles,),
        in_specs=in_specs,
        out_specs=out_specs,
        compiler_params=_params(vmem, 1),
        name="deepnorm_ln",
    )(*args)


def _rms_rope(x, gain, cos, sin, eps):
    xn = x * lax.rsqrt(jnp.mean(x * x, axis=-1, keepdims=True) + eps) * gain
    if cos is None:
        return xn
    d = x.shape[-1]
    q = d // 4
    lane = lax.broadcasted_iota(jnp.int32, xn.shape, 1)
    fwd = pltpu.roll(xn, d - q, axis=1)
    bwd = pltpu.roll(xn, q, axis=1)
    swapped = jnp.where(lane % (2 * q) < q, fwd, bwd)
    return xn * cos + swapped * sin


def _dot_nt(a, b):
    return lax.dot_general(a, b, (((1,), (1,)), ((), ())), preferred_element_type=F32)


def _attend_chunks(q, chunks):
    m = l = acc = None
    for k, v, bias in chunks:
        s = _dot_nt(q, k)
        if bias is not None:
            s = s + bias
        cm = jnp.max(s, axis=-1, keepdims=True)
        if m is None:
            m_new = cm
            p = jnp.exp(s - m_new)
            l = jnp.sum(p, axis=-1, keepdims=True)
            acc = jnp.dot(p.astype(BF16), v, preferred_element_type=F32)
        else:
            m_new = jnp.maximum(m, cm)
            a = jnp.exp(m - m_new)
            p = jnp.exp(s - m_new)
            l = a * l + jnp.sum(p, axis=-1, keepdims=True)
            acc = a * acc + jnp.dot(p.astype(BF16), v, preferred_element_type=F32)
        m = m_new
    return acc / l


def _attend_chunks_base2(q, kt_ref, vx_ref, bounds, hd):
    m = acc = None
    for s0, n in bounds:
        s = jnp.dot(q, kt_ref[:, s0:s0 + n], preferred_element_type=F32)
        cm = jnp.max(s, axis=-1, keepdims=True)
        if m is None:
            m_new = cm
            acc = jnp.dot(jnp.exp2(s - m_new).astype(BF16), vx_ref[s0:s0 + n, :], preferred_element_type=F32)
        else:
            m_new = jnp.maximum(m, cm)
            pv = jnp.dot(jnp.exp2(s - m_new).astype(BF16), vx_ref[s0:s0 + n, :], preferred_element_type=F32)
            acc = jnp.exp2(m - m_new) * acc + pv
        m = m_new
    return acc[:, :hd] / acc[:, hd:2 * hd]


def _na_kernel(q_ref, k_ref, v_ref, bias_ref, _, o_ref, *, ctx_len, grid_w, qrows, krows, n_rows, scale,
               hd, heads):
    t = pl.program_id(2)
    ks = jnp.clip(qrows * t - qrows, 0, n_rows - krows)
    start = pl.multiple_of(ctx_len + ks * grid_w, 16)
    nk = krows * grid_w
    for h in range(heads):
        cols = slice(h * hd, (h + 1) * hd)
        q = (q_ref[:, cols].astype(F32) * scale).astype(BF16)
        k_win = k_ref[pl.ds(start, nk), cols]
        v_win = v_ref[pl.ds(start, nk), cols]
        k_ctx = k_ref[0:ctx_len, cols]
        v_ctx = v_ref[0:ctx_len, cols]
        o = _attend_chunks(q, [(k_win, v_win, bias_ref[h]), (k_ctx, v_ctx, None)])
        o_ref[:, cols] = o.astype(o_ref.dtype)


def _na_bias(cfg, rpb):
    W = cfg.grid_w
    n_rows = _exact_div(cfg.seq, W)
    qr, kr = cfg.na_qrows, cfg.na_qrows + cfg.na_win_h
    n_t = _exact_div(n_rows, qr)
    wh, ww = cfg.na_win_h, cfg.na_win_w
    roffs, valids = [], []
    for t in (0, 1, n_t - 1):
        ks = int(np.clip(qr * t - qr, 0, n_rows - kr))
        r = qr * t + np.arange(qr)
        rs = np.clip(r - wh // 2, 0, n_rows - wh)
        kabs = ks + np.arange(kr)
        vrow = (kabs[None, :] >= rs[:, None]) & (kabs[None, :] < rs[:, None] + wh)
        roff = np.clip(kabs[None, :] - r[:, None] + (wh - 1), 0, 2 * wh - 2)
        roffs.append(roff)
        valids.append(vrow)
    roff = np.stack(roffs)
    vrow = np.stack(valids)
    cols = np.arange(W)
    cs = np.clip(cols - ww // 2, 0, W - ww)
    vcol = (cols[None, :] >= cs[:, None]) & (cols[None, :] < cs[:, None] + ww)
    coff = np.clip(cols[None, :] - cols[:, None], -(ww - 1), ww - 1) + (ww - 1)
    n_ro, n_co = 2 * wh - 1, 2 * ww - 1
    onehot = (np.arange(n_co)[:, None] == coff.reshape(1, W * W)).astype(np.float32)
    H = rpb.shape[0]
    colb = jnp.dot(rpb.astype(F32).reshape(H * n_ro, n_co), jnp.asarray(onehot),
                   precision=lax.Precision.HIGHEST).reshape(H, n_ro, W, W)
    colb = jnp.where(vcol[None, None], colb, cfg.neg_inf)
    masked = jnp.full((H, W, W), cfg.neg_inf, F32)
    variants = []
    for v in range(3):
        rows = []
        for a in range(qr):
            tiles = [colb[:, int(roff[v, a, b])] if vrow[v, a, b] else masked for b in range(kr)]
            rows.append(jnp.concatenate(tiles, axis=-1))
        variants.append(jnp.concatenate(rows, axis=1))
    return jnp.stack(variants, axis=1)


def _na_attention(cfg, p, bias, offs):
    M = p.shape[0]
    W, hd = cfg.grid_w, cfg.head_dim
    RB = cfg.ctx_len + cfg.seq
    n_rows = _exact_div(cfg.seq, W)
    qr, kr = cfg.na_qrows, cfg.na_qrows + cfg.na_win_h
    n_t = _exact_div(n_rows, qr)
    qb = qr * W
    q0 = _exact_div(cfg.ctx_len, qb)
    tiles_b = _exact_div(RB, qb)
    hps = cfg.na_hps
    hw = hps * hd
    cq, ck, cv = (_exact_div(offs[i], hw) for i in range(3))

    def q_map(b, h, t):
        return (b * tiles_b + q0 + t, cq + h)

    def var_map(b, h, t):
        return (h, jnp.where(t == 0, 0, jnp.where(t == n_t - 1, 2, 1)), 0, 0)

    kern = functools.partial(_na_kernel, ctx_len=cfg.ctx_len, grid_w=W, qrows=qr, krows=kr,
                             n_rows=n_rows, scale=float(hd) ** -0.5, hd=hd, heads=hps)
    vmem = hps * (4 * RB * hd * 2 + 2 * qb * kr * W * 4 + 6 * qb * (kr * W + cfg.ctx_len) * 4)
    return pl.pallas_call(
        kern,
        out_shape=jax.ShapeDtypeStruct((M, cfg.na_heads * hd), BF16),
        grid=(cfg.batch, _exact_div(cfg.na_heads, hps), n_t),
        in_specs=[pl.BlockSpec((qb, hw), q_map),
                  pl.BlockSpec((RB, hw), lambda b, h, t: (b, ck + h)),
                  pl.BlockSpec((RB, hw), lambda b, h, t: (b, cv + h)),
                  pl.BlockSpec((hps, None, qb, kr * W), var_map),
                  pl.BlockSpec(memory_space=pl.ANY)],
        out_specs=pl.BlockSpec((qb, hw), lambda b, h, t: (b * tiles_b + q0 + t, h)),
        input_output_aliases={4: 0},
        compiler_params=_params(vmem, 3),
        name="na_attention",
    )(p, p, p, bias, jnp.zeros((M, cfg.na_heads * hd), BF16))


def _gq_kernel(q_ref, k_ref, v_ref, qg_ref, kg_ref, cos_ref, sin_ref, _, o_ref, kt_ref, vx_ref, *,
               l, groups, hd, tq, kc, ctx_len, seq, eps, scale):
    qi = pl.program_id(2)

    @pl.when(qi == 0)
    def _():
        kn = _rms_rope(k_ref[...].astype(F32), kg_ref[l:l + 1, :], cos_ref[...], sin_ref[...], eps)
        kt_ref[...] = kn.T.astype(BF16)
        vx_ref[:, 0:hd] = v_ref[...]
        vx_ref[:, hd:2 * hd] = jnp.ones((v_ref.shape[0], hd), BF16)

    row0 = pl.multiple_of(ctx_len + qi * tq, 8)
    cos = cos_ref[pl.ds(row0, tq), :]
    sin = sin_ref[pl.ds(row0, tq), :]
    gain = qg_ref[l:l + 1, :] * (scale * LOG2_E)
    qs = [_rms_rope(q_ref[:, g * hd:(g + 1) * hd].astype(F32), gain, cos, sin, eps).astype(BF16)
          for g in range(groups)]
    q = jnp.concatenate(qs, axis=0)
    bounds = [(0, ctx_len)] + [(ctx_len + c * kc, kc) for c in range(seq // kc)]
    o = _attend_chunks_base2(q, kt_ref, vx_ref, bounds, hd)
    for g in range(groups):
        o_ref[:, g * hd:(g + 1) * hd] = o[g * tq:(g + 1) * tq, :].astype(o_ref.dtype)


def _rope_tables(cfg):
    hd = cfg.head_dim
    q = hd // 4
    t = np.arange(cfg.seq)
    inv = cfg.rope_theta ** (-np.arange(q, dtype=np.float64) / q)
    ang_r = (t // cfg.grid_w)[:, None] * inv[None, :]
    ang_c = (t % cfg.grid_w)[:, None] * inv[None, :]
    cos = np.concatenate([np.cos(ang_r), np.cos(ang_r), np.cos(ang_c), np.cos(ang_c)], axis=1)
    sin = np.concatenate([-np.sin(ang_r), np.sin(ang_r), -np.sin(ang_c), np.sin(ang_c)], axis=1)
    cos = np.concatenate([np.ones((cfg.ctx_len, hd)), cos], axis=0)
    sin = np.concatenate([np.zeros((cfg.ctx_len, hd)), sin], axis=0)
    return jnp.asarray(cos, F32), jnp.asarray(sin, F32)


def _gq_attention(cfg, p, q_gain, k_gain, cos, sin, l, offs):
    M = p.shape[0]
    hd = cfg.head_dim
    RB = cfg.ctx_len + cfg.seq
    G = _exact_div(cfg.gq_heads, cfg.gq_kv)
    tq = cfg.gq_tq
    nq = _exact_div(cfg.seq, tq)
    q0 = _exact_div(cfg.ctx_len, tq)
    tiles_b = _exact_div(RB, tq)
    cq = _exact_div(offs[3], G * hd)
    ck = _exact_div(offs[4], hd)
    cv = _exact_div(offs[5], hd)
    kc = cfg.gq_kc
    _exact_div(cfg.seq, kc)
    kern = functools.partial(_gq_kernel, l=l, groups=G, hd=hd, tq=tq, kc=kc, ctx_len=cfg.ctx_len,
                             seq=cfg.seq, eps=cfg.eps, scale=float(hd) ** -0.5)
    vmem = 4 * RB * hd * 2 + RB * hd * 2 + 4 * RB * hd * 4 + 2 * 2 * tq * G * hd * 2 + 6 * G * tq * kc * 4
    return pl.pallas_call(
        kern,
        out_shape=jax.ShapeDtypeStruct((M, cfg.gq_heads * hd), BF16),
        grid=(cfg.batch, cfg.gq_kv, nq),
        in_specs=[pl.BlockSpec((tq, G * hd), lambda b, k, i: (b * tiles_b + q0 + i, cq + k)),
                  pl.BlockSpec((RB, hd), lambda b, k, i: (b, ck + k)),
                  pl.BlockSpec((RB, hd), lambda b, k, i: (b, cv + k)),
                  pl.BlockSpec(q_gain.shape, lambda b, k, i: (0, 0)),
                  pl.BlockSpec(k_gain.shape, lambda b, k, i: (0, 0)),
                  pl.BlockSpec((RB, hd), lambda b, k, i: (0, 0)),
                  pl.BlockSpec((RB, hd), lambda b, k, i: (0, 0)),
                  pl.BlockSpec(memory_space=pl.ANY)],
        out_specs=pl.BlockSpec((tq, G * hd), lambda b, k, i: (b * tiles_b + q0 + i, k)),
        scratch_shapes=[pltpu.VMEM((hd, RB), BF16), pltpu.VMEM((RB, 2 * hd), BF16)],
        input_output_aliases={7: 0},
        compiler_params=_params(vmem, 3),
        name="gq_attention",
    )(p, p, p, q_gain, k_gain, cos, sin, jnp.zeros((M, cfg.gq_heads * hd), BF16))


def _ctx_attn_kernel(*refs, l, groups, hd, norm, eps, scale):
    if norm:
        q_ref, k_ref, v_ref, qg_ref, kg_ref, _, o_ref = refs
    else:
        q_ref, k_ref, v_ref, _, o_ref = refs
    k = k_ref[...]
    if norm:
        k = _rms_rope(k.astype(F32), kg_ref[l:l + 1, :], None, None, eps).astype(BF16)
    n = q_ref.shape[0]
    qs = []
    for g in range(groups):
        qg = q_ref[:, g * hd:(g + 1) * hd].astype(F32)
        if norm:
            qg = _rms_rope(qg, qg_ref[l:l + 1, :] * scale, None, None, eps)
        else:
            qg = qg * scale
        qs.append(qg.astype(BF16))
    q = jnp.concatenate(qs, axis=0)
    o = _attend_chunks(q, [(k, v_ref[...], None)])
    for g in range(groups):
        o_ref[:, g * hd:(g + 1) * hd] = o[g * n:(g + 1) * n, :].astype(o_ref.dtype)


def _ctx_attention(cfg, p, y_prev, n_kv, groups, col_q, col_k, col_v, l, gains=None):
    hd = cfg.head_dim
    C = cfg.ctx_len
    tiles_b = _exact_div(cfg.ctx_len + cfg.seq, C)
    cq = _exact_div(col_q, groups * hd)
    ck = _exact_div(col_k, hd)
    cv = _exact_div(col_v, hd)
    norm = gains is not None
    in_specs = [pl.BlockSpec((C, groups * hd), lambda b, k: (b * tiles_b, cq + k)),
                pl.BlockSpec((C, hd), lambda b, k: (b * tiles_b, ck + k)),
                pl.BlockSpec((C, hd), lambda b, k: (b * tiles_b, cv + k))]
    args = [p, p, p]
    if norm:
        in_specs += [pl.BlockSpec(gains[0].shape, lambda b, k: (0, 0)),
                     pl.BlockSpec(gains[1].shape, lambda b, k: (0, 0))]
        args += list(gains)
    in_specs.append(pl.BlockSpec(memory_space=pl.ANY))
    args.append(y_prev)
    kern = functools.partial(_ctx_attn_kernel, l=l, groups=groups, hd=hd, norm=norm, eps=cfg.eps,
                             scale=float(hd) ** -0.5)
    return pl.pallas_call(
        kern,
        out_shape=jax.ShapeDtypeStruct(y_prev.shape, y_prev.dtype),
        grid=(cfg.batch, n_kv),
        in_specs=in_specs,
        out_specs=pl.BlockSpec((C, groups * hd), lambda b, k: (b * tiles_b, k)),
        input_output_aliases={len(args) - 1: 0},
        compiler_params=_params(8 << 20, 2),
        name="ctx_attention",
    )(*args)


def _tile_scan(a, b, reverse):
    n = a.shape[0]
    rows = lax.broadcasted_iota(jnp.int32, a.shape, 0)
    d = 1
    while d < n:
        shift = n - d if reverse else d
        keep = rows < n - d if reverse else rows >= d
        a_prev = jnp.where(keep, pltpu.roll(a, shift, axis=0), 1.0)
        b_prev = jnp.where(keep, pltpu.roll(b, shift, axis=0), 0.0)
        b = a * b_prev + b
        a = a * a_prev
        d *= 2
    return a, b


def _lru_kernel(x_ref, gate_ref, cw_ref, cb_ref, wr_ref, br_ref, wi_ref, bi_ref, lam_ref, o_ref,
                xpad_ref, af_ref, bf_ref, ab_ref, bb_ref, *, l, ctx_len, seq, chunk, lru_c, conv_w):
    cwid = x_ref.shape[1]
    nblk = cwid // V7X_LANES
    pad = V7X_SUBLANES
    ctx_base = pad
    lat_base = ctx_base + ctx_len + pad
    zeros = jnp.zeros((pad, cwid), F32)
    xpad_ref[0:pad, :] = zeros
    xpad_ref[ctx_base + ctx_len:lat_base, :] = zeros
    xpad_ref[lat_base + seq:lat_base + seq + pad, :] = zeros
    xpad_ref[ctx_base:ctx_base + ctx_len, :] = x_ref[0:ctx_len, :].astype(F32)
    xpad_ref[lat_base:lat_base + seq, :] = x_ref[ctx_len:ctx_len + seq, :].astype(F32)

    taps = cw_ref[...]
    cbias = cb_ref[l:l + 1, :]
    lam = lam_ref[...]
    nl = -lam
    softplus = jnp.maximum(nl, 0.0) + jnp.log1p(jnp.exp(-jnp.abs(nl)))
    a_refs = (af_ref, ab_ref)
    b_refs = (bf_ref, bb_ref)

    def coeffs(src_base, dst_row, rows):
        xc = cbias
        for tap in range(conv_w):
            xc = xc + taps[tap:tap + 1, :] * xpad_ref[pl.ds(src_base + tap - conv_w // 2, rows), :]
        xcb = xc.astype(BF16)
        for d in range(2):
            rs, is_ = [], []
            for j in range(nblk):
                xj = xcb[:, j * V7X_LANES:(j + 1) * V7X_LANES]
                rs.append(jnp.dot(xj, wr_ref[d, j].astype(BF16), preferred_element_type=F32))
                is_.append(jnp.dot(xj, wi_ref[d, j].astype(BF16), preferred_element_type=F32))
            gr = jax.nn.sigmoid(jnp.concatenate(rs, axis=1) + br_ref[d:d + 1, :])
            gi = jax.nn.sigmoid(jnp.concatenate(is_, axis=1) + bi_ref[d:d + 1, :])
            log_a = -lru_c * gr * softplus[d:d + 1, :]
            a = jnp.exp(log_a)
            a_refs[d][pl.ds(dst_row, rows), :] = a
            one_minus_a2 = -jnp.tanh(log_a) * (1.0 + a * a)
            b_refs[d][pl.ds(dst_row, rows), :] = jnp.sqrt(one_minus_a2) * (gi * xc)

    coeffs(ctx_base, 0, ctx_len)
    for c in range(seq // chunk):
        coeffs(lat_base + c * chunk, ctx_len + c * chunk, chunk)

    nsub = V7X_SUBLANES

    def scan_seg(first, n, h):
        def body(i, hs):
            hf, hb = hs
            rf = pl.multiple_of(first + nsub * i, nsub)
            rb = pl.multiple_of(first + n - nsub - nsub * i, nsub)
            ca, cb = _tile_scan(af_ref[pl.ds(rf, nsub), :], bf_ref[pl.ds(rf, nsub), :], False)
            tile_f = ca * hf + cb
            bf_ref[pl.ds(rf, nsub), :] = tile_f
            ca, cb = _tile_scan(ab_ref[pl.ds(rb, nsub), :], bb_ref[pl.ds(rb, nsub), :], True)
            tile_b = ca * hb + cb
            bb_ref[pl.ds(rb, nsub), :] = tile_b
            return tile_f[nsub - 1:nsub, :], tile_b[0:1, :]
        return lax.fori_loop(0, n // nsub, body, h, unroll=2)

    h0 = jnp.zeros((1, cwid), F32)
    hs = scan_seg(0, ctx_len, (h0, h0))
    scan_seg(ctx_len, seq, hs)

    def out_body(c, carry):
        r0 = pl.multiple_of(c * ctx_len, ctx_len)
        h = bf_ref[pl.ds(r0, ctx_len), :] + bb_ref[pl.ds(r0, ctx_len), :]
        g = gate_ref[pl.ds(r0, ctx_len), :].astype(F32)
        o_ref[pl.ds(r0, ctx_len), :] = (h * jax.nn.gelu(g)).astype(o_ref.dtype)
        return carry

    lax.fori_loop(0, (ctx_len + seq) // ctx_len, out_body, 0)


def _lru_mixer(cfg, p, conv_w, conv_b, w_r, b_r, w_i, b_i, lam, l, offs):
    M = p.shape[0]
    RB = cfg.ctx_len + cfg.seq
    cw = cfg.lru_cw
    ng = _exact_div(cfg.lru_width, cw)
    blk = _exact_div(cfg.lru_width, cfg.lru_blocks)
    assert blk == V7X_LANES
    nb = _exact_div(cw, blk)
    cx = _exact_div(offs[6], cw)
    cg = _exact_div(offs[7], cw)
    _exact_div(cfg.seq, cfg.lru_chunk)
    _exact_div(RB, cfg.ctx_len)
    kern = functools.partial(_lru_kernel, l=l, ctx_len=cfg.ctx_len, seq=cfg.seq, chunk=cfg.lru_chunk,
                             lru_c=cfg.lru_c, conv_w=cfg.conv_w)
    vmem = 5 * (RB + 32) * cw * 4 + 3 * 2 * RB * cw * 2 + 12 * cfg.lru_chunk * cw * 4 + (4 << 20)
    return pl.pallas_call(
        kern,
        out_shape=jax.ShapeDtypeStruct((M, cfg.lru_width), BF16),
        grid=(cfg.batch, ng),
        in_specs=[pl.BlockSpec((RB, cw), lambda b, g: (b, cx + g)),
                  pl.BlockSpec((RB, cw), lambda b, g: (b, cg + g)),
                  pl.BlockSpec((None, cfg.conv_w, cw), lambda b, g: (l, 0, g)),
                  pl.BlockSpec((conv_b.shape[0], cw), lambda b, g: (0, g)),
                  pl.BlockSpec((None, 2, nb, blk, blk), lambda b, g: (l, 0, g, 0, 0)),
                  pl.BlockSpec((None, 2, cw), lambda b, g: (l, 0, g)),
                  pl.BlockSpec((None, 2, nb, blk, blk), lambda b, g: (l, 0, g, 0, 0)),
                  pl.BlockSpec((None, 2, cw), lambda b, g: (l, 0, g)),
                  pl.BlockSpec((None, 2, cw), lambda b, g: (l, 0, g))],
        out_specs=pl.BlockSpec((RB, cw), lambda b, g: (b, g)),
        scratch_shapes=[pltpu.VMEM((RB + 3 * V7X_SUBLANES, cw), F32)] + [pltpu.VMEM((RB, cw), F32)] * 4,
        compiler_params=_params(vmem, 2),
        name="rglru_mixer",
    )(p, p, conv_w, conv_b, w_r, b_r, w_i, b_i, lam)


def _merge_kernel(ya_ref, yb_ref, yc_ref, wa_ref, wb_ref, wc_ref, g0_ref, g1_ref, g2_ref, o_ref,
                  wab_ref, wbb_ref, wcb_ref):
    @pl.when(pl.program_id(1) == 0)
    def _():
        wab_ref[...] = wa_ref[...].astype(BF16)
        wbb_ref[...] = wb_ref[...].astype(BF16)
        wcb_ref[...] = wc_ref[...].astype(BF16)

    acc = jax.nn.sigmoid(g0_ref[...].astype(F32)) * jnp.dot(ya_ref[...], wab_ref[...], preferred_element_type=F32)
    acc += jax.nn.sigmoid(g1_ref[...].astype(F32)) * jnp.dot(yb_ref[...], wbb_ref[...], preferred_element_type=F32)
    acc += jax.nn.sigmoid(g2_ref[...].astype(F32)) * jnp.dot(yc_ref[...], wcb_ref[...], preferred_element_type=F32)
    o_ref[...] = acc.astype(o_ref.dtype)


def _merge(cfg, ya, yb, yc, p, w_na, w_gq, w_lru, l, offs):
    M = ya.shape[0]
    D = cfg.d_model
    bm, bn = cfg.merge_bm, cfg.mm_bn
    nj = _exact_div(D, bn)
    gcol = _exact_div(offs[8], bn)
    ks = (ya.shape[1], yb.shape[1], yc.shape[1])
    ksum = sum(ks)
    vmem = 2 * bm * ksum * 2 + 2 * ksum * bn * 4 + ksum * bn * 2 + 3 * 2 * bm * bn * 2 + 2 * bm * bn * 2 + 4 * bm * bn * 4
    x_specs = [pl.BlockSpec((bm, k), lambda j, i: (i, 0)) for k in ks]
    w_specs = [pl.BlockSpec((None, k, bn), lambda j, i: (l, 0, j)) for k in ks]
    g_specs = [pl.BlockSpec((bm, bn), functools.partial(lambda j, i, s: (i, gcol + s * nj + j), s=s))
               for s in range(3)]
    return pl.pallas_call(
        _merge_kernel,
        out_shape=jax.ShapeDtypeStruct((M, D), BF16),
        grid=(nj, _exact_div(M, bm)),
        in_specs=x_specs + w_specs + g_specs,
        out_specs=pl.BlockSpec((bm, bn), lambda j, i: (i, j)),
        scratch_shapes=[pltpu.VMEM((k, bn), BF16) for k in ks],
        compiler_params=_params(vmem, 2),
        name="branch_merge",
    )(ya, yb, yc, w_na, w_gq, w_lru, p, p, p)


def _forward(cfg, x, c, ctx, c_ctx, w_ada, b_ada, w_in, na_rpb, gq_q_gain, gq_k_gain, lru_conv_w,
             lru_conv_b, lru_w_r, lru_b_r, lru_w_i, lru_b_i, lru_lambda, w_br_na, w_br_gq, w_br_lru,
             w_o, ln1_g, ln1_b, ln2_g, ln2_b, ffn_w1, ffn_w3, ffn_w2, moe_router, moe_w1, moe_w3,
             moe_w2):
    B, D = cfg.batch, cfg.d_model
    offs, widths = _splits(cfg)
    in_cols = offs[-1] + widths[-1]
    bm, bn = cfg.mm_bm, cfg.mm_bn

    x_all = jnp.concatenate([ctx, x], axis=1).reshape(B * (cfg.ctx_len + cfg.seq), D)
    cond = jnp.concatenate([c, c_ctx[None, :], jnp.zeros((V7X_SUBLANES - B - 1, D), F32)], axis=0)
    b_ada3 = b_ada.reshape(cfg.depth, 1, 6 * D)
    mods = [_ada_table(cond, w_ada, b_ada3, l).reshape(V7X_SUBLANES, 6, D) for l in range(cfg.depth)]
    cos, sin = _rope_tables(cfg)

    u = _modulate(cfg, x_all, mods[0], 0, 1)
    for l in range(cfg.depth):
        need_ctx = l < cfg.depth - 1
        mod = mods[l]
        p = _matmul(u, [w_in], (l,), bm=bm, bn=bn, out_dtype=BF16, name="in_proj")
        assert p.shape[1] == in_cols
        bias = _na_bias(cfg, na_rpb[l])
        ya = _na_attention(cfg, p, bias, offs)
        yb = _gq_attention(cfg, p, gq_q_gain, gq_k_gain, cos, sin, l, offs)
        if need_ctx:
            ya = _ctx_attention(cfg, p, ya, cfg.na_heads, 1, offs[0], offs[1], offs[2], l)
            yb = _ctx_attention(cfg, p, yb, cfg.gq_kv, cfg.gq_heads // cfg.gq_kv, offs[3], offs[4],
                                offs[5], l, gains=(gq_q_gain, gq_k_gain))
        yc = _lru_mixer(cfg, p, lru_conv_w, lru_conv_b, lru_w_r, lru_b_r, lru_w_i, lru_b_i,
                        lru_lambda, l, offs)
        merged = _merge(cfg, ya, yb, yc, p, w_br_na, w_br_gq, w_br_lru, l, offs)
        y = _matmul(merged, [w_o], (l,), bm=bm, bn=bn, out_dtype=F32, name="out_proj")

        dense = l % 2 == 0
        router = None
        if not dense:
            router = jnp.pad(moe_router[l // 2], ((0, 0), (0, V7X_LANES - cfg.n_experts)))
        res = _layer_norm(cfg, x_all, y, mod, 2, ln1_g, ln1_b, l, mod_next=mod, next_rows=(3, 4),
                          router=router, u_dtype=BF16 if dense else F32)
        x_all, u2 = res[0], res[1]
        if dense:
            h = _matmul(u2, [ffn_w1, ffn_w3], (l // 2,), bm=bm, bn=cfg.glu_bn, out_dtype=BF16,
                        glu=True, name="ffn_up")
            w2 = ffn_w2[l // 2].astype(BF16)[None]
            f = _matmul(h, [w2], (0,), bm=cfg.down_bm, bn=bn, out_dtype=F32, name="ffn_down")
        else:
            top = res[2]
            rt = _route(cfg, top)
            xs = _gather_rows(u2, rt["src3"], BF16)
            h = _grouped_matmul(xs, [moe_w1, moe_w3], l // 2, rt["te"], rt["n_used"], bm=cfg.moe_bm,
                                bn=cfg.moe_up_bn, out_dtype=BF16, glu=True, name="moe_up")
            ys = _grouped_matmul(h, [moe_w2], l // 2, rt["te"], rt["n_used"], bm=cfg.moe_bm,
                                 bn=cfg.moe_down_bn, out_dtype=F32, name="moe_down")
            f = _combine(ys, rt["pos1"], rt["pos2"], top)
        if l + 1 < cfg.depth:
            x_all, u = _layer_norm(cfg, x_all, f, mod, 5, ln2_g, ln2_b, l, mod_next=mods[l + 1],
                                   next_rows=(0, 1))
        else:
            (x_lat,) = _layer_norm(cfg, x_all, f, mod, 5, ln2_g, ln2_b, l, latent_only=True)
    return x_lat.reshape(B, cfg.seq, D)


def kernel(x, c, ctx, c_ctx, w_ada, b_ada, w_in, na_rpb, gq_q_gain, gq_k_gain, lru_conv_w, lru_conv_b, lru_w_r, lru_b_r, lru_w_i, lru_b_i, lru_lambda, w_br_na, w_br_gq, w_br_lru, w_o, ln1_g, ln1_b, ln2_g, ln2_b, ffn_w1, ffn_w3, ffn_w2, moe_router, moe_w1, moe_w3, moe_w2):
    return _forward(FULL_CFG, x, c, ctx, c_ctx, w_ada, b_ada, w_in, na_rpb, gq_q_gain, gq_k_gain,
                    lru_conv_w, lru_conv_b, lru_w_r, lru_b_r, lru_w_i, lru_b_i, lru_lambda, w_br_na,
                    w_br_gq, w_br_lru, w_o, ln1_g, ln1_b, ln2_g, ln2_b, ffn_w1, ffn_w3, ffn_w2,
                    moe_router, moe_w1, moe_w3, moe_w2)
```

```python
import collections
import functools

import numpy as np
import jax
import jax.numpy as jnp
from jax import lax
from jax.experimental import pallas as pl
from jax.experimental.pallas import tpu as pltpu

F32 = jnp.float32
BF16 = jnp.bfloat16
LOG2_E = 1.4426950408889634

V7X_LANES = 128
V7X_SUBLANES = 8
V7X_VMEM_BYTES = 64 * 1024 * 1024
VMEM_CAP_BYTES = V7X_VMEM_BYTES - 6 * 1024 * 1024

Cfg = collections.namedtuple(
    "Cfg",
    "d_model batch seq depth ctx_len grid_w head_dim na_heads na_win_h na_win_w gq_heads gq_kv "
    "rope_theta lru_width lru_blocks conv_w lru_c d_ff n_experts d_ff_expert eps neg_inf "
    "row_tile mm_bm mm_bn glu_bn merge_bm down_bm moe_bm moe_up_bn moe_down_bn na_qrows na_hps gq_tq gq_kc lru_cw "
    "lru_chunk")

FULL_CFG = Cfg(
    d_model=4096, batch=2, seq=4096, depth=2, ctx_len=256, grid_w=64, head_dim=128,
    na_heads=12, na_win_h=8, na_win_w=16, gq_heads=16, gq_kv=4, rope_theta=10000.0,
    lru_width=1536, lru_blocks=12, conv_w=4, lru_c=8.0, d_ff=11008, n_experts=8,
    d_ff_expert=3072, eps=1e-6, neg_inf=-1e30,
    row_tile=256, mm_bm=1088, mm_bn=512, glu_bn=256, merge_bm=512, down_bm=256, moe_bm=256, moe_up_bn=512,
    moe_down_bn=1024, na_qrows=4, na_hps=2, gq_tq=256, gq_kc=1024, lru_cw=256, lru_chunk=512)


def _splits(cfg):
    na_w = cfg.na_heads * cfg.head_dim
    gq_q = cfg.gq_heads * cfg.head_dim
    gq_kv = cfg.gq_kv * cfg.head_dim
    widths = (na_w, na_w, na_w, gq_q, gq_kv, gq_kv, cfg.lru_width, cfg.lru_width, 3 * cfg.d_model)
    offs = tuple(int(v) for v in np.cumsum((0,) + widths[:-1]))
    return offs, widths


def _exact_div(a, b):
    assert a % b == 0, (a, b)
    return a // b


def _params(vmem_bytes, n_axes):
    limit = int(min(VMEM_CAP_BYTES, max(vmem_bytes * 5 // 4 + (4 << 20), 16 << 20)))
    return pltpu.CompilerParams(dimension_semantics=("arbitrary",) * n_axes, vmem_limit_bytes=limit)


def _dot_tile(x, wb_refs, glu):
    acc = jnp.dot(x, wb_refs[0][...], preferred_element_type=F32)
    if glu:
        acc = jax.nn.silu(acc) * jnp.dot(x, wb_refs[1][...], preferred_element_type=F32)
    return acc


def _mm_kernel(*refs, n_w, cast_w, glu):
    x_ref = refs[0]
    w_refs = refs[1:1 + n_w]
    o_ref = refs[1 + n_w]
    wb_refs = refs[2 + n_w:] if cast_w else w_refs

    if cast_w:
        @pl.when(pl.program_id(1) == 0)
        def _():
            for w_ref, wb_ref in zip(w_refs, wb_refs):
                wb_ref[...] = w_ref[...].astype(BF16)

    o_ref[...] = _dot_tile(x_ref[...], wb_refs, glu).astype(o_ref.dtype)


def _gmm_kernel(te_ref, nu_ref, *refs, n_w, glu):
    x_ref = refs[0]
    w_refs = refs[1:1 + n_w]
    o_ref = refs[1 + n_w]
    wb_refs = refs[2 + n_w:]
    i = pl.program_id(1)
    prev = te_ref[jnp.maximum(i - 1, 0)]

    @pl.when(jnp.logical_or(i == 0, te_ref[i] != prev))
    def _():
        for w_ref, wb_ref in zip(w_refs, wb_refs):
            wb_ref[...] = w_ref[...].astype(BF16)

    @pl.when(i < nu_ref[0])
    def _():
        o_ref[...] = _dot_tile(x_ref[...], wb_refs, glu).astype(o_ref.dtype)

    @pl.when(i >= nu_ref[0])
    def _():
        o_ref[...] = jnp.zeros(o_ref.shape, o_ref.dtype)


def _mm_vmem_bytes(n_w, bm, bn, K, wbytes, cast_w, out_dtype):
    return (2 * bm * K * 2 + n_w * (2 * K * bn * wbytes + (K * bn * 2 if cast_w else 0))
            + 2 * bm * bn * jnp.dtype(out_dtype).itemsize + 3 * bm * bn * 4)


def _matmul(x, ws, w_lead, *, bm, bn, out_dtype, glu=False, name="mm"):
    M, K = x.shape
    N = ws[0].shape[-1]
    assert ws[0].shape[-2] == K
    cast_w = ws[0].dtype != BF16
    lead = tuple(w_lead)
    in_specs = [pl.BlockSpec((bm, K), lambda j, i: (i, 0))]
    in_specs += [pl.BlockSpec((None,) * len(lead) + (K, bn), lambda j, i: lead + (0, j)) for _ in ws]
    scratch = [pltpu.VMEM((K, bn), BF16) for _ in ws] if cast_w else []
    vmem = _mm_vmem_bytes(len(ws), bm, bn, K, ws[0].dtype.itemsize, cast_w, out_dtype)
    return pl.pallas_call(
        functools.partial(_mm_kernel, n_w=len(ws), cast_w=cast_w, glu=glu),
        out_shape=jax.ShapeDtypeStruct((M, N), out_dtype),
        grid=(_exact_div(N, bn), _exact_div(M, bm)),
        in_specs=in_specs,
        out_specs=pl.BlockSpec((bm, bn), lambda j, i: (i, j)),
        scratch_shapes=scratch,
        compiler_params=_params(vmem, 2),
        name=name,
    )(x, *ws)


def _grouped_matmul(x, ws, l, tile_expert, n_used, *, bm, bn, out_dtype, glu=False, name="gmm"):
    R, K = x.shape
    N = ws[0].shape[-1]
    assert ws[0].shape[-2] == K
    in_specs = [pl.BlockSpec((bm, K), lambda j, i, te, nu: (i, 0))]
    in_specs += [pl.BlockSpec((None, None, K, bn), lambda j, i, te, nu: (l, te[i], 0, j)) for _ in ws]
    vmem = _mm_vmem_bytes(len(ws), bm, bn, K, 4, True, out_dtype)
    grid_spec = pltpu.PrefetchScalarGridSpec(
        num_scalar_prefetch=2,
        grid=(_exact_div(N, bn), _exact_div(R, bm)),
        in_specs=in_specs,
        out_specs=pl.BlockSpec((bm, bn), lambda j, i, te, nu: (i, j)),
        scratch_shapes=[pltpu.VMEM((K, bn), BF16) for _ in ws])
    return pl.pallas_call(
        functools.partial(_gmm_kernel, n_w=len(ws), glu=glu),
        out_shape=jax.ShapeDtypeStruct((R, N), out_dtype),
        grid_spec=grid_spec,
        compiler_params=_params(vmem, 2),
        name=name,
    )(tile_expert, n_used, x, *ws)


def _row_copy(src_hbm, row, dst_vmem, r, sem):
    return pltpu.make_async_copy(src_hbm.at[pl.ds(row, 1), :], dst_vmem.at[pl.ds(r, 1), :], sem)


def _gather_kernel(src_ref, x_hbm, o_ref, buf_ref, sem):
    bm = o_ref.shape[0]

    def issue(r, c):
        _row_copy(x_hbm, src_ref[0, r], buf_ref, r, sem).start()
        return c

    def wait(r, c):
        _row_copy(x_hbm, 0, buf_ref, r, sem).wait()
        return c

    lax.fori_loop(0, bm, issue, 0, unroll=8)
    lax.fori_loop(0, bm, wait, 0, unroll=8)
    o_ref[...] = buf_ref[...].astype(o_ref.dtype)


def _gather_rows(x, src3, out_dtype):
    n_tiles, _, bm = src3.shape
    D = x.shape[1]
    vmem = bm * D * 4 + 2 * bm * D * jnp.dtype(out_dtype).itemsize + 2 * bm * D * 4
    return pl.pallas_call(
        _gather_kernel,
        out_shape=jax.ShapeDtypeStruct((n_tiles * bm, D), out_dtype),
        grid=(n_tiles,),
        in_specs=[pl.BlockSpec((None, 1, bm), lambda t: (t, 0, 0), memory_space=pltpu.SMEM),
                  pl.BlockSpec(memory_space=pl.ANY)],
        out_specs=pl.BlockSpec((bm, D), lambda t: (t, 0)),
        scratch_shapes=[pltpu.VMEM((bm, D), x.dtype), pltpu.SemaphoreType.DMA],
        compiler_params=_params(vmem, 1),
        name="moe_gather",
    )(src3, x)


def _combine_kernel(p1_ref, p2_ref, top_ref, y_hbm, o_ref, a_ref, b_ref, sems):
    bm = o_ref.shape[0]

    def issue(r, c):
        _row_copy(y_hbm, p1_ref[0, r], a_ref, r, sems.at[0]).start()
        _row_copy(y_hbm, p2_ref[0, r], b_ref, r, sems.at[1]).start()
        return c

    def wait(r, c):
        _row_copy(y_hbm, 0, a_ref, r, sems.at[0]).wait()
        _row_copy(y_hbm, 0, b_ref, r, sems.at[1]).wait()
        return c

    lax.fori_loop(0, bm, issue, 0, unroll=8)
    lax.fori_loop(0, bm, wait, 0, unroll=8)
    top = top_ref[...]
    o_ref[...] = (top[:, 2:3] * a_ref[...] + top[:, 3:4] * b_ref[...]).astype(o_ref.dtype)


def _combine(ys, pos1, pos2, top, out_dtype):
    n_tiles, _, bm = pos1.shape
    D = ys.shape[1]
    M = n_tiles * bm
    vmem = 2 * bm * D * 4 + 2 * bm * D * 4 + 2 * bm * D * 4
    idx_spec = pl.BlockSpec((None, 1, bm), lambda t: (t, 0, 0), memory_space=pltpu.SMEM)
    return pl.pallas_call(
        _combine_kernel,
        out_shape=jax.ShapeDtypeStruct((M, D), out_dtype),
        grid=(n_tiles,),
        in_specs=[idx_spec, idx_spec,
                  pl.BlockSpec((bm, V7X_LANES), lambda t: (t, 0)),
                  pl.BlockSpec(memory_space=pl.ANY)],
        out_specs=pl.BlockSpec((bm, D), lambda t: (t, 0)),
        scratch_shapes=[pltpu.VMEM((bm, D), ys.dtype), pltpu.VMEM((bm, D), ys.dtype),
                        pltpu.SemaphoreType.DMA((2,))],
        compiler_params=_params(vmem, 1),
        name="moe_combine",
    )(pos1, pos2, top, ys)


def _route(cfg, top, latent_only):
    T = top.shape[0]
    E, bm = cfg.n_experts, cfg.moe_bm
    P = 2 * T
    n_tiles = _exact_div(P, bm) + E
    experts = jnp.arange(E, dtype=jnp.int32)
    ef = jnp.clip(top[:, :2].astype(jnp.int32), 0, E - 1).reshape(P)
    onehot = (ef[:, None] == experts[None, :]).astype(jnp.int32)
    csum = jnp.cumsum(onehot, axis=0)
    n_e = csum[-1]
    padded = ((n_e + bm - 1) // bm) * bm
    gstart = jnp.cumsum(padded) - padded
    ustart = jnp.cumsum(n_e) - n_e
    dest = jnp.sum((csum - 1 + gstart[None, :]) * onehot, axis=1).reshape(T, 2)
    n_used = (jnp.sum(padded) // bm).astype(jnp.int32).reshape(1)
    tstart = jnp.arange(n_tiles, dtype=jnp.int32) * bm
    gend = gstart + padded
    te = jnp.minimum(jnp.sum((tstart[:, None] >= gend[None, :]).astype(jnp.int32), axis=1), E - 1)
    tile_oh = (te[:, None] == experts[None, :]).astype(jnp.int32)
    tile_n = jnp.sum(tile_oh * n_e[None, :], axis=1)
    k = ((tstart - jnp.sum(tile_oh * gstart[None, :], axis=1))[:, None]
         + jnp.arange(bm, dtype=jnp.int32)[None, :])
    s = jnp.clip(jnp.sum(tile_oh * ustart[None, :], axis=1)[:, None] + k, 0, P - 1)
    order = jnp.argsort(ef, stable=True).astype(jnp.int32)
    tok = jnp.where(k < tile_n[:, None], order[s] // 2, 0)
    if latent_only:
        tok = (tok // cfg.seq) * (cfg.ctx_len + cfg.seq) + cfg.ctx_len + tok % cfg.seq
    tiles_t = _exact_div(T, bm)
    return dict(te=te, n_used=n_used, src3=tok.reshape(n_tiles, 1, bm),
                pos1=dest[:, 0].reshape(tiles_t, 1, bm), pos2=dest[:, 1].reshape(tiles_t, 1, bm))


def _ada_kernel(c_ref, w_ref, b_ref, o_ref):
    c = c_ref[...]
    s = (c * jax.nn.sigmoid(c)).astype(BF16)
    acc = jnp.dot(s, w_ref[...].astype(BF16), preferred_element_type=F32)
    o_ref[...] = acc + b_ref[...]


def _ada_table(cond, w_ada, b_ada3, l, bn=1024):
    R, D = cond.shape
    N = w_ada.shape[-1]
    vmem = 2 * D * bn * 4 + D * bn * 2 + 4 * R * bn * 4 + 2 * R * D * 4
    return pl.pallas_call(
        _ada_kernel,
        out_shape=jax.ShapeDtypeStruct((R, N), F32),
        grid=(_exact_div(N, bn),),
        in_specs=[pl.BlockSpec((R, D), lambda j: (0, 0)),
                  pl.BlockSpec((None, D, bn), lambda j: (l, 0, j)),
                  pl.BlockSpec((None, 1, bn), lambda j: (l, 0, j))],
        out_specs=pl.BlockSpec((R, bn), lambda j: (0, j)),
        compiler_params=_params(vmem, 1),
        name="ada_table",
    )(cond, w_ada, b_ada3)


def _mod_sel(cfg):
    tiles_per_batch = _exact_div(cfg.ctx_len + cfg.seq, cfg.row_tile)
    ctx_tiles = _exact_div(cfg.ctx_len, cfg.row_tile)

    def sel(i):
        return jnp.where(i % tiles_per_batch < ctx_tiles, cfg.batch, i // tiles_per_batch)
    return sel


def _modulate_kernel(x_ref, m_ref, u_ref, *, shift_row, scale_row):
    m = m_ref[...]
    u = x_ref[...] * (1.0 + m[scale_row:scale_row + 1, :]) + m[shift_row:shift_row + 1, :]
    u_ref[...] = u.astype(u_ref.dtype)


def _modulate(cfg, x_all, modtab, shift_row, scale_row):
    M, D = x_all.shape
    rt = cfg.row_tile
    sel = _mod_sel(cfg)
    vmem = 2 * rt * D * 4 + 2 * rt * D * 2 + 2 * 8 * D * 4 + 2 * rt * D * 4
    return pl.pallas_call(
        functools.partial(_modulate_kernel, shift_row=shift_row, scale_row=scale_row),
        out_shape=jax.ShapeDtypeStruct((M, D), BF16),
        grid=(_exact_div(M, rt),),
        in_specs=[pl.BlockSpec((rt, D), lambda i: (i, 0)),
                  pl.BlockSpec((None, 6, D), lambda i: (sel(i), 0, 0))],
        out_specs=pl.BlockSpec((rt, D), lambda i: (i, 0)),
        compiler_params=_params(vmem, 1),
        name="modulate",
    )(x_all, modtab)


def _ln_kernel(*refs, alpha, eps, gate_row, l, next_rows, n_experts):
    it = iter(refs)
    x_ref, y_ref, mc_ref = next(it), next(it), next(it)
    mn_ref = next(it) if next_rows is not None else None
    g_ref, b_ref = next(it), next(it)
    r_ref = next(it) if n_experts else None
    xo_ref = next(it)
    u_ref = next(it) if next_rows is not None else None
    go_ref = next(it) if n_experts else None

    gate = mc_ref[gate_row:gate_row + 1, :]
    z = alpha * x_ref[...] + gate * y_ref[...].astype(F32)
    mu = jnp.mean(z, axis=-1, keepdims=True)
    zc = z - mu
    var = jnp.mean(zc * zc, axis=-1, keepdims=True)
    xn = zc * lax.rsqrt(var + eps) * g_ref[l:l + 1, :] + b_ref[l:l + 1, :]
    xo_ref[...] = xn
    if next_rows is not None:
        shift_row, scale_row = next_rows
        mn = mn_ref[...]
        u = xn * (1.0 + mn[scale_row:scale_row + 1, :]) + mn[shift_row:shift_row + 1, :]
        u_ref[...] = u.astype(u_ref.dtype)
        if n_experts:
            logits = jnp.dot(u, r_ref[...], preferred_element_type=F32,
                             precision=lax.Precision.HIGHEST)
            lane = lax.broadcasted_iota(jnp.int32, logits.shape, 1)
            neg = jnp.float32(-jnp.inf)
            lg = jnp.where(lane < n_experts, logits, neg)
            m1 = jnp.max(lg, axis=-1, keepdims=True)
            i1 = jnp.min(jnp.where(lg == m1, lane, V7X_LANES), axis=-1, keepdims=True)
            lg2 = jnp.where(lane == i1, neg, lg)
            m2 = jnp.max(lg2, axis=-1, keepdims=True)
            i2 = jnp.min(jnp.where(lg2 == m2, lane, V7X_LANES), axis=-1, keepdims=True)
            e2 = jnp.exp(m2 - m1)
            w1 = 1.0 / (1.0 + e2)
            w2 = e2 / (1.0 + e2)
            packed = jnp.where(lane == 0, i1.astype(F32), jnp.where(lane == 1, i2.astype(F32), 0.0))
            go_ref[...] = packed + jnp.where(lane == 2, w1, 0.0) + jnp.where(lane == 3, w2, 0.0)


def _layer_norm(cfg, x_all, y, mod_cur, gate_row, ln_g, ln_b, l, mod_next=None, next_rows=None,
                router=None, u_dtype=BF16, latent_only=False, y_dense=False):
    M, D = x_all.shape
    rt = cfg.row_tile
    alpha = float((2 * cfg.depth) ** 0.25)
    n_experts = cfg.n_experts if router is not None else 0
    if latent_only:
        assert next_rows is None and router is None
        lat_tiles = _exact_div(cfg.seq, rt)
        tiles_b = _exact_div(cfg.ctx_len + cfg.seq, rt)
        ctx_tiles = _exact_div(cfg.ctx_len, rt)
        n_tiles = cfg.batch * lat_tiles
        in_row = pl.BlockSpec((rt, D), lambda i: ((i // lat_tiles) * tiles_b + ctx_tiles + i % lat_tiles, 0))
        mod_spec = pl.BlockSpec((None, 6, D), lambda i: (i // lat_tiles, 0, 0))
        out_rows = cfg.batch * cfg.seq
    else:
        sel = _mod_sel(cfg)
        n_tiles = _exact_div(M, rt)
        in_row = pl.BlockSpec((rt, D), lambda i: (i, 0))
        mod_spec = pl.BlockSpec((None, 6, D), lambda i: (sel(i), 0, 0))
        out_rows = M
    row_spec = pl.BlockSpec((rt, D), lambda i: (i, 0))
    par_spec = pl.BlockSpec(ln_g.shape, lambda i: (0, 0))
    assert latent_only or not y_dense
    in_specs = [in_row, row_spec if y_dense else in_row, mod_spec]
    args = [x_all, y, mod_cur]
    if next_rows is not None:
        in_specs.append(mod_spec)
        args.append(mod_next)
    in_specs += [par_spec, par_spec]
    args += [ln_g, ln_b]
    out_shape = [jax.ShapeDtypeStruct((out_rows, D), F32)]
    out_specs = [row_spec]
    if next_rows is not None:
        out_shape.append(jax.ShapeDtypeStruct((M, D), u_dtype))
        out_specs.append(row_spec)
    if n_experts:
        in_specs.append(pl.BlockSpec(router.shape, lambda i: (0, 0)))
        args.append(router)
        out_shape.append(jax.ShapeDtypeStruct((M, V7X_LANES), F32))
        out_specs.append(pl.BlockSpec((rt, V7X_LANES), lambda i: (i, 0)))
    vmem = 2 * rt * D * (4 + 4 + 4 + 2) + 4 * 8 * D * 4 + 4 * rt * D * 4 + (2 * D * 128 * 4 * 4 if n_experts else 0)
    kern = functools.partial(_ln_kernel, alpha=alpha, eps=cfg.eps, gate_row=gate_row, l=l,
                             next_rows=next_rows, n_experts=n_experts)
    return pl.pallas_call(
        kern,
        out_shape=out_shape,
        grid=(n_tiles,),
        in_specs=in_specs,
        out_specs=out_specs,
        compiler_params=_params(vmem, 1),
        name="deepnorm_ln",
    )(*args)


def _rms_rope(x, gain, cos, sin, eps):
    xn = x * lax.rsqrt(jnp.mean(x * x, axis=-1, keepdims=True) + eps) * gain
    if cos is None:
        return xn
    d = x.shape[-1]
    q = d // 4
    lane = lax.broadcasted_iota(jnp.int32, xn.shape, 1)
    fwd = pltpu.roll(xn, d - q, axis=1)
    bwd = pltpu.roll(xn, q, axis=1)
    swapped = jnp.where(lane % (2 * q) < q, fwd, bwd)
    return xn * cos + swapped * sin


def _dot_nt(a, b):
    return lax.dot_general(a, b, (((1,), (1,)), ((), ())), preferred_element_type=F32)


def _attend_chunks(q, chunks):
    m = l = acc = None
    for k, v, bias in chunks:
        s = _dot_nt(q, k)
        if bias is not None:
            s = s + bias
        cm = jnp.max(s, axis=-1, keepdims=True)
        if m is None:
            m_new = cm
            p = jnp.exp(s - m_new)
            l = jnp.sum(p, axis=-1, keepdims=True)
            acc = jnp.dot(p.astype(BF16), v, preferred_element_type=F32)
        else:
            m_new = jnp.maximum(m, cm)
            a = jnp.exp(m - m_new)
            p = jnp.exp(s - m_new)
            l = a * l + jnp.sum(p, axis=-1, keepdims=True)
            acc = a * acc + jnp.dot(p.astype(BF16), v, preferred_element_type=F32)
        m = m_new
    return acc / l


def _attend_chunks_base2(q, kt_ref, vx_ref, bounds, hd):
    m = acc = None
    for s0, n in bounds:
        s = jnp.dot(q, kt_ref[:, s0:s0 + n], preferred_element_type=F32)
        cm = jnp.max(s, axis=-1, keepdims=True)
        if m is None:
            m_new = cm
            acc = jnp.dot(jnp.exp2(s - m_new).astype(BF16), vx_ref[s0:s0 + n, :], preferred_element_type=F32)
        else:
            m_new = jnp.maximum(m, cm)
            pv = jnp.dot(jnp.exp2(s - m_new).astype(BF16), vx_ref[s0:s0 + n, :], preferred_element_type=F32)
            acc = jnp.exp2(m - m_new) * acc + pv
        m = m_new
    return acc[:, :hd] / acc[:, hd:2 * hd]


def _na_kernel(q_ref, k_ref, v_ref, bias_ref, _, o_ref, *, ctx_len, grid_w, qrows, krows, n_rows, scale,
               hd, heads):
    t = pl.program_id(2)
    ks = jnp.clip(qrows * t - qrows, 0, n_rows - krows)
    start = pl.multiple_of(ctx_len + ks * grid_w, 16)
    nk = krows * grid_w
    for h in range(heads):
        cols = slice(h * hd, (h + 1) * hd)
        q = (q_ref[:, cols].astype(F32) * scale).astype(BF16)
        k_win = k_ref[pl.ds(start, nk), cols]
        v_win = v_ref[pl.ds(start, nk), cols]
        k_ctx = k_ref[0:ctx_len, cols]
        v_ctx = v_ref[0:ctx_len, cols]
        o = _attend_chunks(q, [(k_win, v_win, bias_ref[h]), (k_ctx, v_ctx, None)])
        o_ref[:, cols] = o.astype(o_ref.dtype)


def _na_bias(cfg, rpb):
    W = cfg.grid_w
    n_rows = _exact_div(cfg.seq, W)
    qr, kr = cfg.na_qrows, cfg.na_qrows + cfg.na_win_h
    n_t = _exact_div(n_rows, qr)
    wh, ww = cfg.na_win_h, cfg.na_win_w
    roffs, valids = [], []
    for t in (0, 1, n_t - 1):
        ks = int(np.clip(qr * t - qr, 0, n_rows - kr))
        r = qr * t + np.arange(qr)
        rs = np.clip(r - wh // 2, 0, n_rows - wh)
        kabs = ks + np.arange(kr)
        vrow = (kabs[None, :] >= rs[:, None]) & (kabs[None, :] < rs[:, None] + wh)
        roff = np.clip(kabs[None, :] - r[:, None] + (wh - 1), 0, 2 * wh - 2)
        roffs.append(roff)
        valids.append(vrow)
    roff = np.stack(roffs)
    vrow = np.stack(valids)
    cols = np.arange(W)
    cs = np.clip(cols - ww // 2, 0, W - ww)
    vcol = (cols[None, :] >= cs[:, None]) & (cols[None, :] < cs[:, None] + ww)
    coff = np.clip(cols[None, :] - cols[:, None], -(ww - 1), ww - 1) + (ww - 1)
    n_ro, n_co = 2 * wh - 1, 2 * ww - 1
    onehot = (np.arange(n_co)[:, None] == coff.reshape(1, W * W)).astype(np.float32)
    H = rpb.shape[0]
    colb = jnp.dot(rpb.astype(F32).reshape(H * n_ro, n_co), jnp.asarray(onehot),
                   precision=lax.Precision.HIGHEST).reshape(H, n_ro, W, W)
    colb = jnp.where(vcol[None, None], colb, cfg.neg_inf)
    masked = jnp.full((H, W, W), cfg.neg_inf, F32)
    variants = []
    for v in range(3):
        rows = []
        for a in range(qr):
            tiles = [colb[:, int(roff[v, a, b])] if vrow[v, a, b] else masked for b in range(kr)]
            rows.append(jnp.concatenate(tiles, axis=-1))
        variants.append(jnp.concatenate(rows, axis=1))
    return jnp.stack(variants, axis=1)


def _na_attention(cfg, p, bias, offs):
    M = p.shape[0]
    W, hd = cfg.grid_w, cfg.head_dim
    RB = cfg.ctx_len + cfg.seq
    n_rows = _exact_div(cfg.seq, W)
    qr, kr = cfg.na_qrows, cfg.na_qrows + cfg.na_win_h
    n_t = _exact_div(n_rows, qr)
    qb = qr * W
    q0 = _exact_div(cfg.ctx_len, qb)
    tiles_b = _exact_div(RB, qb)
    hps = cfg.na_hps
    hw = hps * hd
    cq, ck, cv = (_exact_div(offs[i], hw) for i in range(3))

    def q_map(b, h, t):
        return (b * tiles_b + q0 + t, cq + h)

    def var_map(b, h, t):
        return (h, jnp.where(t == 0, 0, jnp.where(t == n_t - 1, 2, 1)), 0, 0)

    kern = functools.partial(_na_kernel, ctx_len=cfg.ctx_len, grid_w=W, qrows=qr, krows=kr,
                             n_rows=n_rows, scale=float(hd) ** -0.5, hd=hd, heads=hps)
    vmem = hps * (4 * RB * hd * 2 + 2 * qb * kr * W * 4 + 6 * qb * (kr * W + cfg.ctx_len) * 4)
    return pl.pallas_call(
        kern,
        out_shape=jax.ShapeDtypeStruct((M, cfg.na_heads * hd), BF16),
        grid=(cfg.batch, _exact_div(cfg.na_heads, hps), n_t),
        in_specs=[pl.BlockSpec((qb, hw), q_map),
                  pl.BlockSpec((RB, hw), lambda b, h, t: (b, ck + h)),
                  pl.BlockSpec((RB, hw), lambda b, h, t: (b, cv + h)),
                  pl.BlockSpec((hps, None, qb, kr * W), var_map),
                  pl.BlockSpec(memory_space=pl.ANY)],
        out_specs=pl.BlockSpec((qb, hw), lambda b, h, t: (b * tiles_b + q0 + t, h)),
        input_output_aliases={4: 0},
        compiler_params=_params(vmem, 3),
        name="na_attention",
    )(p, p, p, bias, jnp.zeros((M, cfg.na_heads * hd), BF16))


def _gq_kernel(q_ref, k_ref, v_ref, qg_ref, kg_ref, cos_ref, sin_ref, _, o_ref, kt_ref, vx_ref, qn_ref, *,
               l, groups, hd, tq, kc, ctx_len, seq, eps, scale):
    qi = pl.program_id(2)

    @pl.when(qi == 0)
    def _():
        cos = cos_ref[...]
        sin = sin_ref[...]
        kn = _rms_rope(k_ref[...].astype(F32), kg_ref[l:l + 1, :], cos, sin, eps)
        kt_ref[...] = kn.T.astype(BF16)
        vx_ref[:, 0:hd] = v_ref[...]
        vx_ref[:, hd:2 * hd] = jnp.ones((v_ref.shape[0], hd), BF16)
        gain = qg_ref[l:l + 1, :] * (scale * LOG2_E)
        for g in range(groups):
            qg = _rms_rope(q_ref[:, g * hd:(g + 1) * hd].astype(F32), gain, cos, sin, eps)
            qn_ref[g] = qg.astype(BF16)

    row0 = pl.multiple_of(ctx_len + qi * tq, 16)
    q = jnp.concatenate([qn_ref[g, pl.ds(row0, tq), :] for g in range(groups)], axis=0)
    bounds = [(0, ctx_len)] + [(ctx_len + c * kc, kc) for c in range(seq // kc)]
    o = _attend_chunks_base2(q, kt_ref, vx_ref, bounds, hd)
    for g in range(groups):
        o_ref[:, g * hd:(g + 1) * hd] = o[g * tq:(g + 1) * tq, :].astype(o_ref.dtype)


def _rope_tables(cfg):
    hd = cfg.head_dim
    q = hd // 4
    t = np.arange(cfg.seq)
    inv = cfg.rope_theta ** (-np.arange(q, dtype=np.float64) / q)
    ang_r = (t // cfg.grid_w)[:, None] * inv[None, :]
    ang_c = (t % cfg.grid_w)[:, None] * inv[None, :]
    cos = np.concatenate([np.cos(ang_r), np.cos(ang_r), np.cos(ang_c), np.cos(ang_c)], axis=1)
    sin = np.concatenate([-np.sin(ang_r), np.sin(ang_r), -np.sin(ang_c), np.sin(ang_c)], axis=1)
    cos = np.concatenate([np.ones((cfg.ctx_len, hd)), cos], axis=0)
    sin = np.concatenate([np.zeros((cfg.ctx_len, hd)), sin], axis=0)
    return jnp.asarray(cos, F32), jnp.asarray(sin, F32)


def _gq_attention(cfg, p, q_gain, k_gain, cos, sin, l, offs):
    M = p.shape[0]
    hd = cfg.head_dim
    RB = cfg.ctx_len + cfg.seq
    G = _exact_div(cfg.gq_heads, cfg.gq_kv)
    tq = cfg.gq_tq
    nq = _exact_div(cfg.seq, tq)
    q0 = _exact_div(cfg.ctx_len, tq)
    tiles_b = _exact_div(RB, tq)
    cq = _exact_div(offs[3], G * hd)
    ck = _exact_div(offs[4], hd)
    cv = _exact_div(offs[5], hd)
    kc = cfg.gq_kc
    _exact_div(cfg.seq, kc)
    kern = functools.partial(_gq_kernel, l=l, groups=G, hd=hd, tq=tq, kc=kc, ctx_len=cfg.ctx_len,
                             seq=cfg.seq, eps=cfg.eps, scale=float(hd) ** -0.5)
    vmem = (4 * RB * hd * 2 + 2 * RB * G * hd * 2 + (3 + G) * RB * hd * 2 + 4 * RB * hd * 4
            + 6 * RB * hd * 4 + 2 * tq * G * hd * 2 + 6 * G * tq * kc * 4)
    return pl.pallas_call(
        kern,
        out_shape=jax.ShapeDtypeStruct((M, cfg.gq_heads * hd), BF16),
        grid=(cfg.batch, cfg.gq_kv, nq),
        in_specs=[pl.BlockSpec((RB, G * hd), lambda b, k, i: (b, cq + k)),
                  pl.BlockSpec((RB, hd), lambda b, k, i: (b, ck + k)),
                  pl.BlockSpec((RB, hd), lambda b, k, i: (b, cv + k)),
                  pl.BlockSpec(q_gain.shape, lambda b, k, i: (0, 0)),
                  pl.BlockSpec(k_gain.shape, lambda b, k, i: (0, 0)),
                  pl.BlockSpec((RB, hd), lambda b, k, i: (0, 0)),
                  pl.BlockSpec((RB, hd), lambda b, k, i: (0, 0)),
                  pl.BlockSpec(memory_space=pl.ANY)],
        out_specs=pl.BlockSpec((tq, G * hd), lambda b, k, i: (b * tiles_b + q0 + i, k)),
        scratch_shapes=[pltpu.VMEM((hd, RB), BF16), pltpu.VMEM((RB, 2 * hd), BF16),
                        pltpu.VMEM((G, RB, hd), BF16)],
        input_output_aliases={7: 0},
        compiler_params=_params(vmem, 3),
        name="gq_attention",
    )(p, p, p, q_gain, k_gain, cos, sin, jnp.zeros((M, cfg.gq_heads * hd), BF16))


def _ctx_attn_kernel(*refs, l, groups, hd, norm, eps, scale):
    if norm:
        q_ref, k_ref, v_ref, qg_ref, kg_ref, _, o_ref = refs
    else:
        q_ref, k_ref, v_ref, _, o_ref = refs
    k = k_ref[...]
    if norm:
        k = _rms_rope(k.astype(F32), kg_ref[l:l + 1, :], None, None, eps).astype(BF16)
    n = q_ref.shape[0]
    qs = []
    for g in range(groups):
        qg = q_ref[:, g * hd:(g + 1) * hd].astype(F32)
        if norm:
            qg = _rms_rope(qg, qg_ref[l:l + 1, :] * scale, None, None, eps)
        else:
            qg = qg * scale
        qs.append(qg.astype(BF16))
    q = jnp.concatenate(qs, axis=0)
    o = _attend_chunks(q, [(k, v_ref[...], None)])
    for g in range(groups):
        o_ref[:, g * hd:(g + 1) * hd] = o[g * n:(g + 1) * n, :].astype(o_ref.dtype)


def _ctx_attention(cfg, p, y_prev, n_kv, groups, col_q, col_k, col_v, l, gains=None):
    hd = cfg.head_dim
    C = cfg.ctx_len
    tiles_b = _exact_div(cfg.ctx_len + cfg.seq, C)
    cq = _exact_div(col_q, groups * hd)
    ck = _exact_div(col_k, hd)
    cv = _exact_div(col_v, hd)
    norm = gains is not None
    in_specs = [pl.BlockSpec((C, groups * hd), lambda b, k: (b * tiles_b, cq + k)),
                pl.BlockSpec((C, hd), lambda b, k: (b * tiles_b, ck + k)),
                pl.BlockSpec((C, hd), lambda b, k: (b * tiles_b, cv + k))]
    args = [p, p, p]
    if norm:
        in_specs += [pl.BlockSpec(gains[0].shape, lambda b, k: (0, 0)),
                     pl.BlockSpec(gains[1].shape, lambda b, k: (0, 0))]
        args += list(gains)
    in_specs.append(pl.BlockSpec(memory_space=pl.ANY))
    args.append(y_prev)
    kern = functools.partial(_ctx_attn_kernel, l=l, groups=groups, hd=hd, norm=norm, eps=cfg.eps,
                             scale=float(hd) ** -0.5)
    return pl.pallas_call(
        kern,
        out_shape=jax.ShapeDtypeStruct(y_prev.shape, y_prev.dtype),
        grid=(cfg.batch, n_kv),
        in_specs=in_specs,
        out_specs=pl.BlockSpec((C, groups * hd), lambda b, k: (b * tiles_b, k)),
        input_output_aliases={len(args) - 1: 0},
        compiler_params=_params(8 << 20, 2),
        name="ctx_attention",
    )(*args)


def _tile_scan(a, b, reverse):
    n = a.shape[0]
    rows = lax.broadcasted_iota(jnp.int32, a.shape, 0)
    d = 1
    while d < n:
        shift = n - d if reverse else d
        keep = rows < n - d if reverse else rows >= d
        a_prev = jnp.where(keep, pltpu.roll(a, shift, axis=0), 1.0)
        b_prev = jnp.where(keep, pltpu.roll(b, shift, axis=0), 0.0)
        b = a * b_prev + b
        a = a * a_prev
        d *= 2
    return a, b


def _lru_kernel(x_ref, gate_ref, cw_ref, cb_ref, wr_ref, br_ref, wi_ref, bi_ref, lam_ref, o_ref,
                xpad_ref, af_ref, bf_ref, ab_ref, bb_ref, *, l, ctx_len, seq, chunk, lru_c, conv_w):
    cwid = x_ref.shape[1]
    nblk = cwid // V7X_LANES
    pad = V7X_SUBLANES
    ctx_base = pad
    lat_base = ctx_base + ctx_len + pad
    zeros = jnp.zeros((pad, cwid), F32)
    xpad_ref[0:pad, :] = zeros
    xpad_ref[ctx_base + ctx_len:lat_base, :] = zeros
    xpad_ref[lat_base + seq:lat_base + seq + pad, :] = zeros
    xpad_ref[ctx_base:ctx_base + ctx_len, :] = x_ref[0:ctx_len, :].astype(F32)
    xpad_ref[lat_base:lat_base + seq, :] = x_ref[ctx_len:ctx_len + seq, :].astype(F32)

    taps = cw_ref[...]
    cbias = cb_ref[l:l + 1, :]
    lam = lam_ref[...]
    nl = -lam
    softplus = jnp.maximum(nl, 0.0) + jnp.log1p(jnp.exp(-jnp.abs(nl)))
    a_refs = (af_ref, ab_ref)
    b_refs = (bf_ref, bb_ref)

    def coeffs(src_base, dst_row, rows):
        xc = cbias
        for tap in range(conv_w):
            xc = xc + taps[tap:tap + 1, :] * xpad_ref[pl.ds(src_base + tap - conv_w // 2, rows), :]
        xcb = xc.astype(BF16)
        for d in range(2):
            rs, is_ = [], []
            for j in range(nblk):
                xj = xcb[:, j * V7X_LANES:(j + 1) * V7X_LANES]
                rs.append(jnp.dot(xj, wr_ref[d, j].astype(BF16), preferred_element_type=F32))
                is_.append(jnp.dot(xj, wi_ref[d, j].astype(BF16), preferred_element_type=F32))
            gr = jax.nn.sigmoid(jnp.concatenate(rs, axis=1) + br_ref[d:d + 1, :])
            gi = jax.nn.sigmoid(jnp.concatenate(is_, axis=1) + bi_ref[d:d + 1, :])
            log_a = -lru_c * gr * softplus[d:d + 1, :]
            a = jnp.exp(log_a)
            a_refs[d][pl.ds(dst_row, rows), :] = a
            one_minus_a2 = -jnp.tanh(log_a) * (1.0 + a * a)
            b_refs[d][pl.ds(dst_row, rows), :] = jnp.sqrt(one_minus_a2) * (gi * xc)

    coeffs(ctx_base, 0, ctx_len)
    for c in range(seq // chunk):
        coeffs(lat_base + c * chunk, ctx_len + c * chunk, chunk)

    nsub = V7X_SUBLANES

    def scan_seg(first, n, h):
        def body(i, hs):
            hf, hb = hs
            rf = pl.multiple_of(first + nsub * i, nsub)
            rb = pl.multiple_of(first + n - nsub - nsub * i, nsub)
            ca, cb = _tile_scan(af_ref[pl.ds(rf, nsub), :], bf_ref[pl.ds(rf, nsub), :], False)
            tile_f = ca * hf + cb
            bf_ref[pl.ds(rf, nsub), :] = tile_f
            ca, cb = _tile_scan(ab_ref[pl.ds(rb, nsub), :], bb_ref[pl.ds(rb, nsub), :], True)
            tile_b = ca * hb + cb
            bb_ref[pl.ds(rb, nsub), :] = tile_b
            return tile_f[nsub - 1:nsub, :], tile_b[0:1, :]
        return lax.fori_loop(0, n // nsub, body, h, unroll=2)

    h0 = jnp.zeros((1, cwid), F32)
    hs = scan_seg(0, ctx_len, (h0, h0))
    scan_seg(ctx_len, seq, hs)

    def out_body(c, carry):
        r0 = pl.multiple_of(c * ctx_len, ctx_len)
        h = bf_ref[pl.ds(r0, ctx_len), :] + bb_ref[pl.ds(r0, ctx_len), :]
        g = gate_ref[pl.ds(r0, ctx_len), :].astype(F32)
        o_ref[pl.ds(r0, ctx_len), :] = (h * jax.nn.gelu(g)).astype(o_ref.dtype)
        return carry

    lax.fori_loop(0, (ctx_len + seq) // ctx_len, out_body, 0)


def _lru_mixer(cfg, p, conv_w, conv_b, w_r, b_r, w_i, b_i, lam, l, offs):
    M = p.shape[0]
    RB = cfg.ctx_len + cfg.seq
    cw = cfg.lru_cw
    ng = _exact_div(cfg.lru_width, cw)
    blk = _exact_div(cfg.lru_width, cfg.lru_blocks)
    assert blk == V7X_LANES
    nb = _exact_div(cw, blk)
    cx = _exact_div(offs[6], cw)
    cg = _exact_div(offs[7], cw)
    _exact_div(cfg.seq, cfg.lru_chunk)
    _exact_div(RB, cfg.ctx_len)
    kern = functools.partial(_lru_kernel, l=l, ctx_len=cfg.ctx_len, seq=cfg.seq, chunk=cfg.lru_chunk,
                             lru_c=cfg.lru_c, conv_w=cfg.conv_w)
    vmem = 5 * (RB + 32) * cw * 4 + 3 * 2 * RB * cw * 2 + 12 * cfg.lru_chunk * cw * 4 + (4 << 20)
    return pl.pallas_call(
        kern,
        out_shape=jax.ShapeDtypeStruct((M, cfg.lru_width), BF16),
        grid=(cfg.batch, ng),
        in_specs=[pl.BlockSpec((RB, cw), lambda b, g: (b, cx + g)),
                  pl.BlockSpec((RB, cw), lambda b, g: (b, cg + g)),
                  pl.BlockSpec((None, cfg.conv_w, cw), lambda b, g: (l, 0, g)),
                  pl.BlockSpec((conv_b.shape[0], cw), lambda b, g: (0, g)),
                  pl.BlockSpec((None, 2, nb, blk, blk), lambda b, g: (l, 0, g, 0, 0)),
                  pl.BlockSpec((None, 2, cw), lambda b, g: (l, 0, g)),
                  pl.BlockSpec((None, 2, nb, blk, blk), lambda b, g: (l, 0, g, 0, 0)),
                  pl.BlockSpec((None, 2, cw), lambda b, g: (l, 0, g)),
                  pl.BlockSpec((None, 2, cw), lambda b, g: (l, 0, g))],
        out_specs=pl.BlockSpec((RB, cw), lambda b, g: (b, g)),
        scratch_shapes=[pltpu.VMEM((RB + 3 * V7X_SUBLANES, cw), F32)] + [pltpu.VMEM((RB, cw), F32)] * 4,
        compiler_params=_params(vmem, 2),
        name="rglru_mixer",
    )(p, p, conv_w, conv_b, w_r, b_r, w_i, b_i, lam)


def _merge_kernel(ya_ref, yb_ref, yc_ref, wa_ref, wb_ref, wc_ref, g0_ref, g1_ref, g2_ref, o_ref,
                  wab_ref, wbb_ref, wcb_ref):
    @pl.when(pl.program_id(1) == 0)
    def _():
        wab_ref[...] = wa_ref[...].astype(BF16)
        wbb_ref[...] = wb_ref[...].astype(BF16)
        wcb_ref[...] = wc_ref[...].astype(BF16)

    acc = jax.nn.sigmoid(g0_ref[...].astype(F32)) * jnp.dot(ya_ref[...], wab_ref[...], preferred_element_type=F32)
    acc += jax.nn.sigmoid(g1_ref[...].astype(F32)) * jnp.dot(yb_ref[...], wbb_ref[...], preferred_element_type=F32)
    acc += jax.nn.sigmoid(g2_ref[...].astype(F32)) * jnp.dot(yc_ref[...], wcb_ref[...], preferred_element_type=F32)
    o_ref[...] = acc.astype(o_ref.dtype)


def _merge(cfg, ya, yb, yc, p, w_na, w_gq, w_lru, l, offs):
    M = ya.shape[0]
    D = cfg.d_model
    bm, bn = cfg.merge_bm, cfg.mm_bn
    nj = _exact_div(D, bn)
    gcol = _exact_div(offs[8], bn)
    ks = (ya.shape[1], yb.shape[1], yc.shape[1])
    ksum = sum(ks)
    vmem = 2 * bm * ksum * 2 + 2 * ksum * bn * 4 + ksum * bn * 2 + 3 * 2 * bm * bn * 2 + 2 * bm * bn * 2 + 4 * bm * bn * 4
    x_specs = [pl.BlockSpec((bm, k), lambda j, i: (i, 0)) for k in ks]
    w_specs = [pl.BlockSpec((None, k, bn), lambda j, i: (l, 0, j)) for k in ks]
    g_specs = [pl.BlockSpec((bm, bn), functools.partial(lambda j, i, s: (i, gcol + s * nj + j), s=s))
               for s in range(3)]
    return pl.pallas_call(
        _merge_kernel,
        out_shape=jax.ShapeDtypeStruct((M, D), BF16),
        grid=(nj, _exact_div(M, bm)),
        in_specs=x_specs + w_specs + g_specs,
        out_specs=pl.BlockSpec((bm, bn), lambda j, i: (i, j)),
        scratch_shapes=[pltpu.VMEM((k, bn), BF16) for k in ks],
        compiler_params=_params(vmem, 2),
        name="branch_merge",
    )(ya, yb, yc, w_na, w_gq, w_lru, p, p, p)


def _forward(cfg, x, c, ctx, c_ctx, w_ada, b_ada, w_in, na_rpb, gq_q_gain, gq_k_gain, lru_conv_w,
             lru_conv_b, lru_w_r, lru_b_r, lru_w_i, lru_b_i, lru_lambda, w_br_na, w_br_gq, w_br_lru,
             w_o, ln1_g, ln1_b, ln2_g, ln2_b, ffn_w1, ffn_w3, ffn_w2, moe_router, moe_w1, moe_w3,
             moe_w2):
    B, D = cfg.batch, cfg.d_model
    offs, widths = _splits(cfg)
    in_cols = offs[-1] + widths[-1]
    bm, bn = cfg.mm_bm, cfg.mm_bn

    x_all = jnp.concatenate([ctx, x], axis=1).reshape(B * (cfg.ctx_len + cfg.seq), D)
    cond = jnp.concatenate([c, c_ctx[None, :], jnp.zeros((V7X_SUBLANES - B - 1, D), F32)], axis=0)
    b_ada3 = b_ada.reshape(cfg.depth, 1, 6 * D)
    mods = [_ada_table(cond, w_ada, b_ada3, l).reshape(V7X_SUBLANES, 6, D) for l in range(cfg.depth)]
    cos, sin = _rope_tables(cfg)

    u = _modulate(cfg, x_all, mods[0], 0, 1)
    for l in range(cfg.depth):
        need_ctx = l < cfg.depth - 1
        mod = mods[l]
        p = _matmul(u, [w_in], (l,), bm=bm, bn=bn, out_dtype=BF16, name="in_proj")
        assert p.shape[1] == in_cols
        bias = _na_bias(cfg, na_rpb[l])
        ya = _na_attention(cfg, p, bias, offs)
        yb = _gq_attention(cfg, p, gq_q_gain, gq_k_gain, cos, sin, l, offs)
        if need_ctx:
            ya = _ctx_attention(cfg, p, ya, cfg.na_heads, 1, offs[0], offs[1], offs[2], l)
            yb = _ctx_attention(cfg, p, yb, cfg.gq_kv, cfg.gq_heads // cfg.gq_kv, offs[3], offs[4],
                                offs[5], l, gains=(gq_q_gain, gq_k_gain))
        yc = _lru_mixer(cfg, p, lru_conv_w, lru_conv_b, lru_w_r, lru_b_r, lru_w_i, lru_b_i,
                        lru_lambda, l, offs)
        merged = _merge(cfg, ya, yb, yc, p, w_br_na, w_br_gq, w_br_lru, l, offs)
        y = _matmul(merged, [w_o], (l,), bm=bm, bn=bn, out_dtype=BF16, name="out_proj")

        dense = l % 2 == 0
        router = None
        if not dense:
            router = jnp.pad(moe_router[l // 2], ((0, 0), (0, V7X_LANES - cfg.n_experts)))
        res = _layer_norm(cfg, x_all, y, mod, 2, ln1_g, ln1_b, l, mod_next=mod, next_rows=(3, 4),
                          router=router, u_dtype=BF16 if dense else F32)
        x_all, u2 = res[0], res[1]
        if dense:
            h = _matmul(u2, [ffn_w1, ffn_w3], (l // 2,), bm=bm, bn=cfg.glu_bn, out_dtype=BF16,
                        glu=True, name="ffn_up")
            w2 = ffn_w2[l // 2].astype(BF16)[None]
            f = _matmul(h, [w2], (0,), bm=cfg.down_bm, bn=bn, out_dtype=BF16, name="ffn_down")
        else:
            top = res[2]
            if not need_ctx:
                top = top.reshape(B, cfg.ctx_len + cfg.seq, V7X_LANES)[:, cfg.ctx_len:, :]
                top = top.reshape(B * cfg.seq, V7X_LANES)
            rt = _route(cfg, top, latent_only=not need_ctx)
            xs = _gather_rows(u2, rt["src3"], BF16)
            h = _grouped_matmul(xs, [moe_w1, moe_w3], l // 2, rt["te"], rt["n_used"], bm=cfg.moe_bm,
                                bn=cfg.moe_up_bn, out_dtype=BF16, glu=True, name="moe_up")
            ys = _grouped_matmul(h, [moe_w2], l // 2, rt["te"], rt["n_used"], bm=cfg.moe_bm,
                                 bn=cfg.moe_down_bn, out_dtype=F32, name="moe_down")
            f = _combine(ys, rt["pos1"], rt["pos2"], top, BF16)
        f_dense = (not dense) and (not need_ctx)
        if l + 1 < cfg.depth:
            assert not f_dense
            x_all, u = _layer_norm(cfg, x_all, f, mod, 5, ln2_g, ln2_b, l, mod_next=mods[l + 1],
                                   next_rows=(0, 1))
        else:
            (x_lat,) = _layer_norm(cfg, x_all, f, mod, 5, ln2_g, ln2_b, l, latent_only=True,
                                   y_dense=f_dense)
    return x_lat.reshape(B, cfg.seq, D)


def kernel(x, c, ctx, c_ctx, w_ada, b_ada, w_in, na_rpb, gq_q_gain, gq_k_gain, lru_conv_w, lru_conv_b, lru_w_r, lru_b_r, lru_w_i, lru_b_i, lru_lambda, w_br_na, w_br_gq, w_br_lru, w_o, ln1_g, ln1_b, ln2_g, ln2_b, ffn_w1, ffn_w3, ffn_w2, moe_router, moe_w1, moe_w3, moe_w2):
    return _forward(FULL_CFG, x, c, ctx, c_ctx, w_ada, b_ada, w_in, na_rpb, gq_q_gain, gq_k_gain,
                    lru_conv_w, lru_conv_b, lru_w_r, lru_b_r, lru_w_i, lru_b_i, lru_lambda, w_br_na,
                    w_br_gq, w_br_lru, w_o, ln1_g, ln1_b, ln2_g, ln2_b, ffn_w1, ffn_w3, ffn_w2,
                    moe_router, moe_w1, moe_w3, moe_w2)
```

```python
import collections
import functools

import numpy as np
import jax
import jax.numpy as jnp
from jax import lax
from jax.experimental import pallas as pl
from jax.experimental.pallas import tpu as pltpu

F32 = jnp.float32
BF16 = jnp.bfloat16
LOG2_E = 1.4426950408889634

V7X_LANES = 128
V7X_SUBLANES = 8
V7X_VMEM_BYTES = 64 * 1024 * 1024
VMEM_CAP_BYTES = V7X_VMEM_BYTES - 6 * 1024 * 1024

Cfg = collections.namedtuple(
    "Cfg",
    "d_model batch seq depth ctx_len grid_w head_dim na_heads na_win_h na_win_w gq_heads gq_kv "
    "rope_theta lru_width lru_blocks conv_w lru_c d_ff n_experts d_ff_expert eps neg_inf "
    "row_tile mm_bm mm_bn glu_bn merge_bm merge_bn down_bm moe_bm moe_up_bn moe_down_bn na_qrows na_hps gq_tq gq_kc lru_cw "
    "lru_chunk")

FULL_CFG = Cfg(
    d_model=4096, batch=2, seq=4096, depth=2, ctx_len=256, grid_w=64, head_dim=128,
    na_heads=12, na_win_h=8, na_win_w=16, gq_heads=16, gq_kv=4, rope_theta=10000.0,
    lru_width=1536, lru_blocks=12, conv_w=4, lru_c=8.0, d_ff=11008, n_experts=8,
    d_ff_expert=3072, eps=1e-6, neg_inf=-1e30,
    row_tile=256, mm_bm=1088, mm_bn=512, glu_bn=256, merge_bm=1088, merge_bn=256, down_bm=256,
    moe_bm=512, moe_up_bn=512, moe_down_bn=1024, na_qrows=4, na_hps=4, gq_tq=256, gq_kc=1024, lru_cw=256, lru_chunk=512)


def _splits(cfg):
    na_w = cfg.na_heads * cfg.head_dim
    gq_q = cfg.gq_heads * cfg.head_dim
    gq_kv = cfg.gq_kv * cfg.head_dim
    widths = (na_w, na_w, na_w, gq_q, gq_kv, gq_kv, cfg.lru_width, cfg.lru_width, 3 * cfg.d_model)
    offs = tuple(int(v) for v in np.cumsum((0,) + widths[:-1]))
    return offs, widths


def _exact_div(a, b):
    assert a % b == 0, (a, b)
    return a // b


def _params(vmem_bytes, n_axes):
    limit = int(min(VMEM_CAP_BYTES, max(vmem_bytes * 5 // 4 + (4 << 20), 16 << 20)))
    return pltpu.CompilerParams(dimension_semantics=("arbitrary",) * n_axes, vmem_limit_bytes=limit)


def _dot_tile(x, wb_refs, glu):
    acc = jnp.dot(x, wb_refs[0][...], preferred_element_type=F32)
    if glu:
        acc = jax.nn.silu(acc) * jnp.dot(x, wb_refs[1][...], preferred_element_type=F32)
    return acc


def _mm_kernel(*refs, n_w, cast_w, glu):
    x_ref = refs[0]
    w_refs = refs[1:1 + n_w]
    o_ref = refs[1 + n_w]
    wb_refs = refs[2 + n_w:] if cast_w else w_refs

    if cast_w:
        @pl.when(pl.program_id(1) == 0)
        def _():
            for w_ref, wb_ref in zip(w_refs, wb_refs):
                wb_ref[...] = w_ref[...].astype(BF16)

    o_ref[...] = _dot_tile(x_ref[...], wb_refs, glu).astype(o_ref.dtype)


def _gmm_kernel(te_ref, nu_ref, *refs, n_w, glu):
    x_ref = refs[0]
    w_refs = refs[1:1 + n_w]
    o_ref = refs[1 + n_w]
    wb_refs = refs[2 + n_w:]
    i = pl.program_id(1)
    prev = te_ref[jnp.maximum(i - 1, 0)]

    @pl.when(jnp.logical_or(i == 0, te_ref[i] != prev))
    def _():
        for w_ref, wb_ref in zip(w_refs, wb_refs):
            wb_ref[...] = w_ref[...].astype(BF16)

    @pl.when(i < nu_ref[0])
    def _():
        o_ref[...] = _dot_tile(x_ref[...], wb_refs, glu).astype(o_ref.dtype)

    @pl.when(i >= nu_ref[0])
    def _():
        o_ref[...] = jnp.zeros(o_ref.shape, o_ref.dtype)


def _mm_vmem_bytes(n_w, bm, bn, K, wbytes, cast_w, out_dtype):
    return (2 * bm * K * 2 + n_w * (2 * K * bn * wbytes + (K * bn * 2 if cast_w else 0))
            + 2 * bm * bn * jnp.dtype(out_dtype).itemsize + 3 * bm * bn * 4)


def _matmul(x, ws, w_lead, *, bm, bn, out_dtype, glu=False, name="mm"):
    M, K = x.shape
    N = ws[0].shape[-1]
    assert ws[0].shape[-2] == K
    cast_w = ws[0].dtype != BF16
    lead = tuple(w_lead)
    in_specs = [pl.BlockSpec((bm, K), lambda j, i: (i, 0))]
    in_specs += [pl.BlockSpec((None,) * len(lead) + (K, bn), lambda j, i: lead + (0, j)) for _ in ws]
    scratch = [pltpu.VMEM((K, bn), BF16) for _ in ws] if cast_w else []
    vmem = _mm_vmem_bytes(len(ws), bm, bn, K, ws[0].dtype.itemsize, cast_w, out_dtype)
    return pl.pallas_call(
        functools.partial(_mm_kernel, n_w=len(ws), cast_w=cast_w, glu=glu),
        out_shape=jax.ShapeDtypeStruct((M, N), out_dtype),
        grid=(_exact_div(N, bn), _exact_div(M, bm)),
        in_specs=in_specs,
        out_specs=pl.BlockSpec((bm, bn), lambda j, i: (i, j)),
        scratch_shapes=scratch,
        compiler_params=_params(vmem, 2),
        name=name,
    )(x, *ws)


def _grouped_matmul(x, ws, l, tile_expert, n_used, *, bm, bn, out_dtype, glu=False, name="gmm"):
    R, K = x.shape
    N = ws[0].shape[-1]
    assert ws[0].shape[-2] == K
    in_specs = [pl.BlockSpec((bm, K), lambda j, i, te, nu: (i, 0))]
    in_specs += [pl.BlockSpec((None, None, K, bn), lambda j, i, te, nu: (l, te[i], 0, j)) for _ in ws]
    vmem = _mm_vmem_bytes(len(ws), bm, bn, K, 4, True, out_dtype)
    grid_spec = pltpu.PrefetchScalarGridSpec(
        num_scalar_prefetch=2,
        grid=(_exact_div(N, bn), _exact_div(R, bm)),
        in_specs=in_specs,
        out_specs=pl.BlockSpec((bm, bn), lambda j, i, te, nu: (i, j)),
        scratch_shapes=[pltpu.VMEM((K, bn), BF16) for _ in ws])
    return pl.pallas_call(
        functools.partial(_gmm_kernel, n_w=len(ws), glu=glu),
        out_shape=jax.ShapeDtypeStruct((R, N), out_dtype),
        grid_spec=grid_spec,
        compiler_params=_params(vmem, 2),
        name=name,
    )(tile_expert, n_used, x, *ws)


def _row_copy(src_hbm, row, dst_vmem, r, sem):
    return pltpu.make_async_copy(src_hbm.at[pl.ds(row, 1), :], dst_vmem.at[pl.ds(r, 1), :], sem)


def _gather_kernel(src_ref, x_hbm, o_ref, buf_ref, sem):
    bm = o_ref.shape[0]

    def issue(i, c):
        for prio in range(2):
            r = 2 * i + prio
            _row_copy(x_hbm, src_ref[0, r], buf_ref, r, sem).start(priority=prio)
        return c

    def wait(r, c):
        _row_copy(x_hbm, 0, buf_ref, r, sem).wait()
        return c

    lax.fori_loop(0, bm // 2, issue, 0, unroll=4)
    lax.fori_loop(0, bm, wait, 0, unroll=8)
    o_ref[...] = buf_ref[...].astype(o_ref.dtype)


def _gather_rows(x, src3, out_dtype):
    n_tiles, _, bm = src3.shape
    D = x.shape[1]
    vmem = bm * D * 4 + 2 * bm * D * jnp.dtype(out_dtype).itemsize + 2 * bm * D * 4
    return pl.pallas_call(
        _gather_kernel,
        out_shape=jax.ShapeDtypeStruct((n_tiles * bm, D), out_dtype),
        grid=(n_tiles,),
        in_specs=[pl.BlockSpec((None, 1, bm), lambda t: (t, 0, 0), memory_space=pltpu.SMEM),
                  pl.BlockSpec(memory_space=pl.ANY)],
        out_specs=pl.BlockSpec((bm, D), lambda t: (t, 0)),
        scratch_shapes=[pltpu.VMEM((bm, D), x.dtype), pltpu.SemaphoreType.DMA],
        compiler_params=_params(vmem, 1),
        name="moe_gather",
    )(src3, x)


def _combine_kernel(p1_ref, p2_ref, top_ref, y_hbm, o_ref, a_ref, b_ref, sems):
    bm = o_ref.shape[0]

    def issue(r, c):
        _row_copy(y_hbm, p1_ref[0, r], a_ref, r, sems.at[0]).start(priority=0)
        _row_copy(y_hbm, p2_ref[0, r], b_ref, r, sems.at[1]).start(priority=1)
        return c

    def wait(r, c):
        _row_copy(y_hbm, 0, a_ref, r, sems.at[0]).wait()
        _row_copy(y_hbm, 0, b_ref, r, sems.at[1]).wait()
        return c

    lax.fori_loop(0, bm, issue, 0, unroll=8)
    lax.fori_loop(0, bm, wait, 0, unroll=8)
    top = top_ref[...]
    o_ref[...] = (top[:, 2:3] * a_ref[...] + top[:, 3:4] * b_ref[...]).astype(o_ref.dtype)


def _combine(ys, pos1, pos2, top, out_dtype):
    n_tiles, _, bm = pos1.shape
    D = ys.shape[1]
    M = n_tiles * bm
    vmem = 2 * bm * D * 4 + 2 * bm * D * 4 + 2 * bm * D * 4
    idx_spec = pl.BlockSpec((None, 1, bm), lambda t: (t, 0, 0), memory_space=pltpu.SMEM)
    return pl.pallas_call(
        _combine_kernel,
        out_shape=jax.ShapeDtypeStruct((M, D), out_dtype),
        grid=(n_tiles,),
        in_specs=[idx_spec, idx_spec,
                  pl.BlockSpec((bm, V7X_LANES), lambda t: (t, 0)),
                  pl.BlockSpec(memory_space=pl.ANY)],
        out_specs=pl.BlockSpec((bm, D), lambda t: (t, 0)),
        scratch_shapes=[pltpu.VMEM((bm, D), ys.dtype), pltpu.VMEM((bm, D), ys.dtype),
                        pltpu.SemaphoreType.DMA((2,))],
        compiler_params=_params(vmem, 1),
        name="moe_combine",
    )(pos1, pos2, top, ys)


def _route(cfg, top, latent_only):
    T = top.shape[0]
    E, bm = cfg.n_experts, cfg.moe_bm
    P = 2 * T
    n_tiles = _exact_div(P, bm) + E
    experts = jnp.arange(E, dtype=jnp.int32)
    ef = jnp.clip(top[:, :2].astype(jnp.int32), 0, E - 1).reshape(P)
    onehot = (ef[:, None] == experts[None, :]).astype(jnp.int32)
    csum = jnp.cumsum(onehot, axis=0)
    n_e = csum[-1]
    padded = ((n_e + bm - 1) // bm) * bm
    gstart = jnp.cumsum(padded) - padded
    ustart = jnp.cumsum(n_e) - n_e
    dest = jnp.sum((csum - 1 + gstart[None, :]) * onehot, axis=1).reshape(T, 2)
    n_used = (jnp.sum(padded) // bm).astype(jnp.int32).reshape(1)
    tstart = jnp.arange(n_tiles, dtype=jnp.int32) * bm
    gend = gstart + padded
    te = jnp.minimum(jnp.sum((tstart[:, None] >= gend[None, :]).astype(jnp.int32), axis=1), E - 1)
    tile_oh = (te[:, None] == experts[None, :]).astype(jnp.int32)
    tile_n = jnp.sum(tile_oh * n_e[None, :], axis=1)
    k = ((tstart - jnp.sum(tile_oh * gstart[None, :], axis=1))[:, None]
         + jnp.arange(bm, dtype=jnp.int32)[None, :])
    s = jnp.clip(jnp.sum(tile_oh * ustart[None, :], axis=1)[:, None] + k, 0, P - 1)
    order = jnp.argsort(ef, stable=True).astype(jnp.int32)
    tok = jnp.where(k < tile_n[:, None], order[s] // 2, 0)
    if latent_only:
        tok = (tok // cfg.seq) * (cfg.ctx_len + cfg.seq) + cfg.ctx_len + tok % cfg.seq
    tiles_t = _exact_div(T, bm)
    return dict(te=te, n_used=n_used, src3=tok.reshape(n_tiles, 1, bm),
                pos1=dest[:, 0].reshape(tiles_t, 1, bm), pos2=dest[:, 1].reshape(tiles_t, 1, bm))


def _ada_kernel(c_ref, w_ref, b_ref, o_ref):
    c = c_ref[...]
    s = (c * jax.nn.sigmoid(c)).astype(BF16)
    acc = jnp.dot(s, w_ref[...].astype(BF16), preferred_element_type=F32)
    o_ref[...] = acc + b_ref[...]


def _ada_table(cond, w_ada, b_ada3, l, bn=1024):
    R, D = cond.shape
    N = w_ada.shape[-1]
    vmem = 2 * D * bn * 4 + D * bn * 2 + 4 * R * bn * 4 + 2 * R * D * 4
    return pl.pallas_call(
        _ada_kernel,
        out_shape=jax.ShapeDtypeStruct((R, N), F32),
        grid=(_exact_div(N, bn),),
        in_specs=[pl.BlockSpec((R, D), lambda j: (0, 0)),
                  pl.BlockSpec((None, D, bn), lambda j: (l, 0, j)),
                  pl.BlockSpec((None, 1, bn), lambda j: (l, 0, j))],
        out_specs=pl.BlockSpec((R, bn), lambda j: (0, j)),
        compiler_params=_params(vmem, 1),
        name="ada_table",
    )(cond, w_ada, b_ada3)


def _mod_sel(cfg):
    tiles_per_batch = _exact_div(cfg.ctx_len + cfg.seq, cfg.row_tile)
    ctx_tiles = _exact_div(cfg.ctx_len, cfg.row_tile)

    def sel(i):
        return jnp.where(i % tiles_per_batch < ctx_tiles, cfg.batch, i // tiles_per_batch)
    return sel


def _modulate_kernel(x_ref, m_ref, u_ref, *, shift_row, scale_row):
    m = m_ref[...]
    u = x_ref[...] * (1.0 + m[scale_row:scale_row + 1, :]) + m[shift_row:shift_row + 1, :]
    u_ref[...] = u.astype(u_ref.dtype)


def _modulate(cfg, x_all, modtab, shift_row, scale_row):
    M, D = x_all.shape
    rt = cfg.row_tile
    sel = _mod_sel(cfg)
    vmem = 2 * rt * D * 4 + 2 * rt * D * 2 + 2 * 8 * D * 4 + 2 * rt * D * 4
    return pl.pallas_call(
        functools.partial(_modulate_kernel, shift_row=shift_row, scale_row=scale_row),
        out_shape=jax.ShapeDtypeStruct((M, D), BF16),
        grid=(_exact_div(M, rt),),
        in_specs=[pl.BlockSpec((rt, D), lambda i: (i, 0)),
                  pl.BlockSpec((None, 6, D), lambda i: (sel(i), 0, 0))],
        out_specs=pl.BlockSpec((rt, D), lambda i: (i, 0)),
        compiler_params=_params(vmem, 1),
        name="modulate",
    )(x_all, modtab)


def _ln_kernel(*refs, alpha, eps, gate_row, l, next_rows, n_experts):
    it = iter(refs)
    x_ref, y_ref, mc_ref = next(it), next(it), next(it)
    mn_ref = next(it) if next_rows is not None else None
    g_ref, b_ref = next(it), next(it)
    r_ref = next(it) if n_experts else None
    xo_ref = next(it)
    u_ref = next(it) if next_rows is not None else None
    go_ref = next(it) if n_experts else None

    gate = mc_ref[gate_row:gate_row + 1, :]
    z = alpha * x_ref[...] + gate * y_ref[...].astype(F32)
    mu = jnp.mean(z, axis=-1, keepdims=True)
    zc = z - mu
    var = jnp.mean(zc * zc, axis=-1, keepdims=True)
    xn = zc * lax.rsqrt(var + eps) * g_ref[l:l + 1, :] + b_ref[l:l + 1, :]
    xo_ref[...] = xn
    if next_rows is not None:
        shift_row, scale_row = next_rows
        mn = mn_ref[...]
        u = xn * (1.0 + mn[scale_row:scale_row + 1, :]) + mn[shift_row:shift_row + 1, :]
        u_ref[...] = u.astype(u_ref.dtype)
        if n_experts:
            logits = jnp.dot(u, r_ref[...], preferred_element_type=F32,
                             precision=lax.Precision.HIGHEST)
            lane = lax.broadcasted_iota(jnp.int32, logits.shape, 1)
            neg = jnp.float32(-jnp.inf)
            lg = jnp.where(lane < n_experts, logits, neg)
            m1 = jnp.max(lg, axis=-1, keepdims=True)
            i1 = jnp.min(jnp.where(lg == m1, lane, V7X_LANES), axis=-1, keepdims=True)
            lg2 = jnp.where(lane == i1, neg, lg)
            m2 = jnp.max(lg2, axis=-1, keepdims=True)
            i2 = jnp.min(jnp.where(lg2 == m2, lane, V7X_LANES), axis=-1, keepdims=True)
            e2 = jnp.exp(m2 - m1)
            w1 = 1.0 / (1.0 + e2)
            w2 = e2 / (1.0 + e2)
            packed = jnp.where(lane == 0, i1.astype(F32), jnp.where(lane == 1, i2.astype(F32), 0.0))
            go_ref[...] = packed + jnp.where(lane == 2, w1, 0.0) + jnp.where(lane == 3, w2, 0.0)


def _layer_norm(cfg, x_all, y, mod_cur, gate_row, ln_g, ln_b, l, mod_next=None, next_rows=None,
                router=None, u_dtype=BF16, latent_only=False, y_dense=False):
    M, D = x_all.shape
    rt = cfg.row_tile
    alpha = float((2 * cfg.depth) ** 0.25)
    n_experts = cfg.n_experts if router is not None else 0
    if latent_only:
        assert next_rows is None and router is None
        lat_tiles = _exact_div(cfg.seq, rt)
        tiles_b = _exact_div(cfg.ctx_len + cfg.seq, rt)
        ctx_tiles = _exact_div(cfg.ctx_len, rt)
        n_tiles = cfg.batch * lat_tiles
        in_row = pl.BlockSpec((rt, D), lambda i: ((i // lat_tiles) * tiles_b + ctx_tiles + i % lat_tiles, 0))
        mod_spec = pl.BlockSpec((None, 6, D), lambda i: (i // lat_tiles, 0, 0))
        out_rows = cfg.batch * cfg.seq
    else:
        sel = _mod_sel(cfg)
        n_tiles = _exact_div(M, rt)
        in_row = pl.BlockSpec((rt, D), lambda i: (i, 0))
        mod_spec = pl.BlockSpec((None, 6, D), lambda i: (sel(i), 0, 0))
        out_rows = M
    row_spec = pl.BlockSpec((rt, D), lambda i: (i, 0))
    par_spec = pl.BlockSpec(ln_g.shape, lambda i: (0, 0))
    assert latent_only or not y_dense
    in_specs = [in_row, row_spec if y_dense else in_row, mod_spec]
    args = [x_all, y, mod_cur]
    if next_rows is not None:
        in_specs.append(mod_spec)
        args.append(mod_next)
    in_specs += [par_spec, par_spec]
    args += [ln_g, ln_b]
    out_shape = [jax.ShapeDtypeStruct((out_rows, D), F32)]
    out_specs = [row_spec]
    if next_rows is not None:
        out_shape.append(jax.ShapeDtypeStruct((M, D), u_dtype))
        out_specs.append(row_spec)
    if n_experts:
        in_specs.append(pl.BlockSpec(router.shape, lambda i: (0, 0)))
        args.append(router)
        out_shape.append(jax.ShapeDtypeStruct((M, V7X_LANES), F32))
        out_specs.append(pl.BlockSpec((rt, V7X_LANES), lambda i: (i, 0)))
    vmem = 2 * rt * D * (4 + 4 + 4 + 2) + 4 * 8 * D * 4 + 4 * rt * D * 4 + (2 * D * 128 * 4 * 4 if n_experts else 0)
    kern = functools.partial(_ln_kernel, alpha=alpha, eps=cfg.eps, gate_row=gate_row, l=l,
                             next_rows=next_rows, n_experts=n_experts)
    return pl.pallas_call(
        kern,
        out_shape=out_shape,
        grid=(n_tiles,),
        in_specs=in_specs,
        out_specs=out_specs,
        compiler_params=_params(vmem, 1),
        name="deepnorm_ln",
    )(*args)


def _rms_rope(x, gain, cos, sin, eps):
    xn = x * lax.rsqrt(jnp.mean(x * x, axis=-1, keepdims=True) + eps) * gain
    if cos is None:
        return xn
    d = x.shape[-1]
    q = d // 4
    lane = lax.broadcasted_iota(jnp.int32, xn.shape, 1)
    fwd = pltpu.roll(xn, d - q, axis=1)
    bwd = pltpu.roll(xn, q, axis=1)
    swapped = jnp.where(lane % (2 * q) < q, fwd, bwd)
    return xn * cos + swapped * sin


def _dot_nt(a, b):
    return lax.dot_general(a, b, (((1,), (1,)), ((), ())), preferred_element_type=F32)


def _attend_chunks(q, chunks):
    m = l = acc = None
    for k, v, bias in chunks:
        s = _dot_nt(q, k)
        if bias is not None:
            s = s + bias
        cm = jnp.max(s, axis=-1, keepdims=True)
        if m is None:
            m_new = cm
            p = jnp.exp(s - m_new)
            l = jnp.sum(p, axis=-1, keepdims=True)
            acc = jnp.dot(p.astype(BF16), v, preferred_element_type=F32)
        else:
            m_new = jnp.maximum(m, cm)
            a = jnp.exp(m - m_new)
            p = jnp.exp(s - m_new)
            l = a * l + jnp.sum(p, axis=-1, keepdims=True)
            acc = a * acc + jnp.dot(p.astype(BF16), v, preferred_element_type=F32)
        m = m_new
    return acc / l


def _attend_chunks_base2(q, kt_ref, vx_ref, bounds, hd):
    m = acc = None
    for s0, n in bounds:
        s = jnp.dot(q, kt_ref[:, s0:s0 + n], preferred_element_type=F32)
        cm = jnp.max(s, axis=-1, keepdims=True)
        if m is None:
            m_new = cm
            acc = jnp.dot(jnp.exp2(s - m_new).astype(BF16), vx_ref[s0:s0 + n, :], preferred_element_type=F32)
        else:
            m_new = jnp.maximum(m, cm)
            pv = jnp.dot(jnp.exp2(s - m_new).astype(BF16), vx_ref[s0:s0 + n, :], preferred_element_type=F32)
            acc = jnp.exp2(m - m_new) * acc + pv
        m = m_new
    return acc[:, :hd] / acc[:, hd:2 * hd]


def _na_kernel(q_ref, k_ref, v_ref, bias_ref, _, o_ref, *, ctx_len, grid_w, qrows, krows, n_rows, scale,
               hd, heads):
    t = pl.program_id(2)
    ks = jnp.clip(qrows * t - qrows, 0, n_rows - krows)
    start = pl.multiple_of(ctx_len + ks * grid_w, 16)
    nk = krows * grid_w
    for h in range(heads):
        cols = slice(h * hd, (h + 1) * hd)
        q = (q_ref[:, cols].astype(F32) * scale).astype(BF16)
        k_win = k_ref[pl.ds(start, nk), cols]
        v_win = v_ref[pl.ds(start, nk), cols]
        k_ctx = k_ref[0:ctx_len, cols]
        v_ctx = v_ref[0:ctx_len, cols]
        o = _attend_chunks(q, [(k_win, v_win, bias_ref[h]), (k_ctx, v_ctx, None)])
        o_ref[:, cols] = o.astype(o_ref.dtype)


def _na_bias(cfg, rpb):
    W = cfg.grid_w
    n_rows = _exact_div(cfg.seq, W)
    qr, kr = cfg.na_qrows, cfg.na_qrows + cfg.na_win_h
    n_t = _exact_div(n_rows, qr)
    wh, ww = cfg.na_win_h, cfg.na_win_w
    roffs, valids = [], []
    for t in (0, 1, n_t - 1):
        ks = int(np.clip(qr * t - qr, 0, n_rows - kr))
        r = qr * t + np.arange(qr)
        rs = np.clip(r - wh // 2, 0, n_rows - wh)
        kabs = ks + np.arange(kr)
        vrow = (kabs[None, :] >= rs[:, None]) & (kabs[None, :] < rs[:, None] + wh)
        roff = np.clip(kabs[None, :] - r[:, None] + (wh - 1), 0, 2 * wh - 2)
        roffs.append(roff)
        valids.append(vrow)
    roff = np.stack(roffs)
    vrow = np.stack(valids)
    cols = np.arange(W)
    cs = np.clip(cols - ww // 2, 0, W - ww)
    vcol = (cols[None, :] >= cs[:, None]) & (cols[None, :] < cs[:, None] + ww)
    coff = np.clip(cols[None, :] - cols[:, None], -(ww - 1), ww - 1) + (ww - 1)
    n_ro, n_co = 2 * wh - 1, 2 * ww - 1
    onehot = (np.arange(n_co)[:, None] == coff.reshape(1, W * W)).astype(np.float32)
    H = rpb.shape[0]
    colb = jnp.dot(rpb.astype(F32).reshape(H * n_ro, n_co), jnp.asarray(onehot),
                   precision=lax.Precision.HIGHEST).reshape(H, n_ro, W, W)
    colb = jnp.where(vcol[None, None], colb, cfg.neg_inf)
    masked = jnp.full((H, W, W), cfg.neg_inf, F32)
    variants = []
    for v in range(3):
        rows = []
        for a in range(qr):
            tiles = [colb[:, int(roff[v, a, b])] if vrow[v, a, b] else masked for b in range(kr)]
            rows.append(jnp.concatenate(tiles, axis=-1))
        variants.append(jnp.concatenate(rows, axis=1))
    return jnp.stack(variants, axis=1)


def _na_attention(cfg, p, bias, offs):
    M = p.shape[0]
    W, hd = cfg.grid_w, cfg.head_dim
    RB = cfg.ctx_len + cfg.seq
    n_rows = _exact_div(cfg.seq, W)
    qr, kr = cfg.na_qrows, cfg.na_qrows + cfg.na_win_h
    n_t = _exact_div(n_rows, qr)
    qb = qr * W
    q0 = _exact_div(cfg.ctx_len, qb)
    tiles_b = _exact_div(RB, qb)
    hps = cfg.na_hps
    hw = hps * hd
    cq, ck, cv = (_exact_div(offs[i], hw) for i in range(3))

    def q_map(b, h, t):
        return (b * tiles_b + q0 + t, cq + h)

    def var_map(b, h, t):
        return (h, jnp.where(t == 0, 0, jnp.where(t == n_t - 1, 2, 1)), 0, 0)

    kern = functools.partial(_na_kernel, ctx_len=cfg.ctx_len, grid_w=W, qrows=qr, krows=kr,
                             n_rows=n_rows, scale=float(hd) ** -0.5, hd=hd, heads=hps)
    vmem = hps * (4 * RB * hd * 2 + 2 * qb * kr * W * 4 + 6 * qb * (kr * W + cfg.ctx_len) * 4)
    return pl.pallas_call(
        kern,
        out_shape=jax.ShapeDtypeStruct((M, cfg.na_heads * hd), BF16),
        grid=(cfg.batch, _exact_div(cfg.na_heads, hps), n_t),
        in_specs=[pl.BlockSpec((qb, hw), q_map),
                  pl.BlockSpec((RB, hw), lambda b, h, t: (b, ck + h)),
                  pl.BlockSpec((RB, hw), lambda b, h, t: (b, cv + h)),
                  pl.BlockSpec((hps, None, qb, kr * W), var_map),
                  pl.BlockSpec(memory_space=pl.ANY)],
        out_specs=pl.BlockSpec((qb, hw), lambda b, h, t: (b * tiles_b + q0 + t, h)),
        input_output_aliases={4: 0},
        compiler_params=_params(vmem, 3),
        name="na_attention",
    )(p, p, p, bias, jnp.zeros((M, cfg.na_heads * hd), BF16))


def _gq_kernel(q_ref, k_ref, v_ref, qg_ref, kg_ref, cos_ref, sin_ref, _, o_ref, kt_ref, vx_ref, qn_ref, *,
               l, groups, hd, tq, kc, ctx_len, seq, eps, scale):
    qi = pl.program_id(2)

    @pl.when(qi == 0)
    def _():
        cos = cos_ref[...]
        sin = sin_ref[...]
        kn = _rms_rope(k_ref[...].astype(F32), kg_ref[l:l + 1, :], cos, sin, eps)
        kt_ref[...] = kn.T.astype(BF16)
        vx_ref[:, 0:hd] = v_ref[...]
        vx_ref[:, hd:2 * hd] = jnp.ones((v_ref.shape[0], hd), BF16)
        gain = qg_ref[l:l + 1, :] * (scale * LOG2_E)
        for g in range(groups):
            qg = _rms_rope(q_ref[:, g * hd:(g + 1) * hd].astype(F32), gain, cos, sin, eps)
            qn_ref[g] = qg.astype(BF16)

    row0 = pl.multiple_of(ctx_len + qi * tq, 16)
    q = jnp.concatenate([qn_ref[g, pl.ds(row0, tq), :] for g in range(groups)], axis=0)
    bounds = [(0, ctx_len)] + [(ctx_len + c * kc, kc) for c in range(seq // kc)]
    o = _attend_chunks_base2(q, kt_ref, vx_ref, bounds, hd)
    for g in range(groups):
        o_ref[:, g * hd:(g + 1) * hd] = o[g * tq:(g + 1) * tq, :].astype(o_ref.dtype)


def _rope_tables(cfg):
    hd = cfg.head_dim
    q = hd // 4
    t = np.arange(cfg.seq)
    inv = cfg.rope_theta ** (-np.arange(q, dtype=np.float64) / q)
    ang_r = (t // cfg.grid_w)[:, None] * inv[None, :]
    ang_c = (t % cfg.grid_w)[:, None] * inv[None, :]
    cos = np.concatenate([np.cos(ang_r), np.cos(ang_r), np.cos(ang_c), np.cos(ang_c)], axis=1)
    sin = np.concatenate([-np.sin(ang_r), np.sin(ang_r), -np.sin(ang_c), np.sin(ang_c)], axis=1)
    cos = np.concatenate([np.ones((cfg.ctx_len, hd)), cos], axis=0)
    sin = np.concatenate([np.zeros((cfg.ctx_len, hd)), sin], axis=0)
    return jnp.asarray(cos, F32), jnp.asarray(sin, F32)


def _gq_attention(cfg, p, q_gain, k_gain, cos, sin, l, offs):
    M = p.shape[0]
    hd = cfg.head_dim
    RB = cfg.ctx_len + cfg.seq
    G = _exact_div(cfg.gq_heads, cfg.gq_kv)
    tq = cfg.gq_tq
    nq = _exact_div(cfg.seq, tq)
    q0 = _exact_div(cfg.ctx_len, tq)
    tiles_b = _exact_div(RB, tq)
    cq = _exact_div(offs[3], G * hd)
    ck = _exact_div(offs[4], hd)
    cv = _exact_div(offs[5], hd)
    kc = cfg.gq_kc
    _exact_div(cfg.seq, kc)
    kern = functools.partial(_gq_kernel, l=l, groups=G, hd=hd, tq=tq, kc=kc, ctx_len=cfg.ctx_len,
                             seq=cfg.seq, eps=cfg.eps, scale=float(hd) ** -0.5)
    vmem = (4 * RB * hd * 2 + 2 * RB * G * hd * 2 + (3 + G) * RB * hd * 2 + 4 * RB * hd * 4
            + 6 * RB * hd * 4 + 2 * tq * G * hd * 2 + 6 * G * tq * kc * 4)
    return pl.pallas_call(
        kern,
        out_shape=jax.ShapeDtypeStruct((M, cfg.gq_heads * hd), BF16),
        grid=(cfg.batch, cfg.gq_kv, nq),
        in_specs=[pl.BlockSpec((RB, G * hd), lambda b, k, i: (b, cq + k)),
                  pl.BlockSpec((RB, hd), lambda b, k, i: (b, ck + k)),
                  pl.BlockSpec((RB, hd), lambda b, k, i: (b, cv + k)),
                  pl.BlockSpec(q_gain.shape, lambda b, k, i: (0, 0)),
                  pl.BlockSpec(k_gain.shape, lambda b, k, i: (0, 0)),
                  pl.BlockSpec((RB, hd), lambda b, k, i: (0, 0)),
                  pl.BlockSpec((RB, hd), lambda b, k, i: (0, 0)),
                  pl.BlockSpec(memory_space=pl.ANY)],
        out_specs=pl.BlockSpec((tq, G * hd), lambda b, k, i: (b * tiles_b + q0 + i, k)),
        scratch_shapes=[pltpu.VMEM((hd, RB), BF16), pltpu.VMEM((RB, 2 * hd), BF16),
                        pltpu.VMEM((G, RB, hd), BF16)],
        input_output_aliases={7: 0},
        compiler_params=_params(vmem, 3),
        name="gq_attention",
    )(p, p, p, q_gain, k_gain, cos, sin, jnp.zeros((M, cfg.gq_heads * hd), BF16))


def _ctx_attn_kernel(*refs, l, groups, hd, norm, eps, scale):
    if norm:
        q_ref, k_ref, v_ref, qg_ref, kg_ref, _, o_ref = refs
    else:
        q_ref, k_ref, v_ref, _, o_ref = refs
    k = k_ref[...]
    if norm:
        k = _rms_rope(k.astype(F32), kg_ref[l:l + 1, :], None, None, eps).astype(BF16)
    n = q_ref.shape[0]
    qs = []
    for g in range(groups):
        qg = q_ref[:, g * hd:(g + 1) * hd].astype(F32)
        if norm:
            qg = _rms_rope(qg, qg_ref[l:l + 1, :] * scale, None, None, eps)
        else:
            qg = qg * scale
        qs.append(qg.astype(BF16))
    q = jnp.concatenate(qs, axis=0)
    o = _attend_chunks(q, [(k, v_ref[...], None)])
    for g in range(groups):
        o_ref[:, g * hd:(g + 1) * hd] = o[g * n:(g + 1) * n, :].astype(o_ref.dtype)


def _ctx_attention(cfg, p, y_prev, n_kv, groups, col_q, col_k, col_v, l, gains=None):
    hd = cfg.head_dim
    C = cfg.ctx_len
    tiles_b = _exact_div(cfg.ctx_len + cfg.seq, C)
    cq = _exact_div(col_q, groups * hd)
    ck = _exact_div(col_k, hd)
    cv = _exact_div(col_v, hd)
    norm = gains is not None
    in_specs = [pl.BlockSpec((C, groups * hd), lambda b, k: (b * tiles_b, cq + k)),
                pl.BlockSpec((C, hd), lambda b, k: (b * tiles_b, ck + k)),
                pl.BlockSpec((C, hd), lambda b, k: (b * tiles_b, cv + k))]
    args = [p, p, p]
    if norm:
        in_specs += [pl.BlockSpec(gains[0].shape, lambda b, k: (0, 0)),
                     pl.BlockSpec(gains[1].shape, lambda b, k: (0, 0))]
        args += list(gains)
    in_specs.append(pl.BlockSpec(memory_space=pl.ANY))
    args.append(y_prev)
    kern = functools.partial(_ctx_attn_kernel, l=l, groups=groups, hd=hd, norm=norm, eps=cfg.eps,
                             scale=float(hd) ** -0.5)
    return pl.pallas_call(
        kern,
        out_shape=jax.ShapeDtypeStruct(y_prev.shape, y_prev.dtype),
        grid=(cfg.batch, n_kv),
        in_specs=in_specs,
        out_specs=pl.BlockSpec((C, groups * hd), lambda b, k: (b * tiles_b, k)),
        input_output_aliases={len(args) - 1: 0},
        compiler_params=_params(8 << 20, 2),
        name="ctx_attention",
    )(*args)


def _tile_scan(a, b, reverse):
    n = a.shape[0]
    rows = lax.broadcasted_iota(jnp.int32, a.shape, 0)
    d = 1
    while d < n:
        shift = n - d if reverse else d
        keep = rows < n - d if reverse else rows >= d
        a_prev = jnp.where(keep, pltpu.roll(a, shift, axis=0), 1.0)
        b_prev = jnp.where(keep, pltpu.roll(b, shift, axis=0), 0.0)
        b = a * b_prev + b
        a = a * a_prev
        d *= 2
    return a, b


def _lru_kernel(x_ref, gate_ref, cw_ref, cb_ref, wr_ref, br_ref, wi_ref, bi_ref, lam_ref, o_ref,
                xpad_ref, af_ref, bf_ref, ab_ref, bb_ref, *, l, ctx_len, seq, chunk, lru_c, conv_w):
    cwid = x_ref.shape[1]
    nblk = cwid // V7X_LANES
    pad = V7X_SUBLANES
    ctx_base = pad
    lat_base = ctx_base + ctx_len + pad
    zeros = jnp.zeros((pad, cwid), F32)
    xpad_ref[0:pad, :] = zeros
    xpad_ref[ctx_base + ctx_len:lat_base, :] = zeros
    xpad_ref[lat_base + seq:lat_base + seq + pad, :] = zeros
    xpad_ref[ctx_base:ctx_base + ctx_len, :] = x_ref[0:ctx_len, :].astype(F32)
    xpad_ref[lat_base:lat_base + seq, :] = x_ref[ctx_len:ctx_len + seq, :].astype(F32)

    taps = cw_ref[...]
    cbias = cb_ref[l:l + 1, :]
    lam = lam_ref[...]
    nl = -lam
    softplus = jnp.maximum(nl, 0.0) + jnp.log1p(jnp.exp(-jnp.abs(nl)))
    a_refs = (af_ref, ab_ref)
    b_refs = (bf_ref, bb_ref)

    def coeffs(src_base, dst_row, rows):
        xc = cbias
        for tap in range(conv_w):
            xc = xc + taps[tap:tap + 1, :] * xpad_ref[pl.ds(src_base + tap - conv_w // 2, rows), :]
        xcb = xc.astype(BF16)
        for d in range(2):
            rs, is_ = [], []
            for j in range(nblk):
                xj = xcb[:, j * V7X_LANES:(j + 1) * V7X_LANES]
                rs.append(jnp.dot(xj, wr_ref[d, j].astype(BF16), preferred_element_type=F32))
                is_.append(jnp.dot(xj, wi_ref[d, j].astype(BF16), preferred_element_type=F32))
            gr = jax.nn.sigmoid(jnp.concatenate(rs, axis=1) + br_ref[d:d + 1, :])
            gi = jax.nn.sigmoid(jnp.concatenate(is_, axis=1) + bi_ref[d:d + 1, :])
            log_a = -lru_c * gr * softplus[d:d + 1, :]
            a = jnp.exp(log_a)
            a_refs[d][pl.ds(dst_row, rows), :] = a
            one_minus_a2 = -jnp.tanh(log_a) * (1.0 + a * a)
            b_refs[d][pl.ds(dst_row, rows), :] = jnp.sqrt(one_minus_a2) * (gi * xc)

    coeffs(ctx_base, 0, ctx_len)
    for c in range(seq // chunk):
        coeffs(lat_base + c * chunk, ctx_len + c * chunk, chunk)

    nsub = V7X_SUBLANES

    def scan_seg(first, n, h):
        def body(i, hs):
            hf, hb = hs
            rf = pl.multiple_of(first + nsub * i, nsub)
            rb = pl.multiple_of(first + n - nsub - nsub * i, nsub)
            ca, cb = _tile_scan(af_ref[pl.ds(rf, nsub), :], bf_ref[pl.ds(rf, nsub), :], False)
            tile_f = ca * hf + cb
            bf_ref[pl.ds(rf, nsub), :] = tile_f
            ca, cb = _tile_scan(ab_ref[pl.ds(rb, nsub), :], bb_ref[pl.ds(rb, nsub), :], True)
            tile_b = ca * hb + cb
            bb_ref[pl.ds(rb, nsub), :] = tile_b
            return tile_f[nsub - 1:nsub, :], tile_b[0:1, :]
        return lax.fori_loop(0, n // nsub, body, h, unroll=2)

    h0 = jnp.zeros((1, cwid), F32)
    hs = scan_seg(0, ctx_len, (h0, h0))
    scan_seg(ctx_len, seq, hs)

    def out_body(c, carry):
        r0 = pl.multiple_of(c * ctx_len, ctx_len)
        h = bf_ref[pl.ds(r0, ctx_len), :] + bb_ref[pl.ds(r0, ctx_len), :]
        g = gate_ref[pl.ds(r0, ctx_len), :].astype(F32)
        o_ref[pl.ds(r0, ctx_len), :] = (h * jax.nn.gelu(g)).astype(o_ref.dtype)
        return carry

    lax.fori_loop(0, (ctx_len + seq) // ctx_len, out_body, 0)


def _lru_mixer(cfg, p, conv_w, conv_b, w_r, b_r, w_i, b_i, lam, l, offs):
    M = p.shape[0]
    RB = cfg.ctx_len + cfg.seq
    cw = cfg.lru_cw
    ng = _exact_div(cfg.lru_width, cw)
    blk = _exact_div(cfg.lru_width, cfg.lru_blocks)
    assert blk == V7X_LANES
    nb = _exact_div(cw, blk)
    cx = _exact_div(offs[6], cw)
    cg = _exact_div(offs[7], cw)
    _exact_div(cfg.seq, cfg.lru_chunk)
    _exact_div(RB, cfg.ctx_len)
    kern = functools.partial(_lru_kernel, l=l, ctx_len=cfg.ctx_len, seq=cfg.seq, chunk=cfg.lru_chunk,
                             lru_c=cfg.lru_c, conv_w=cfg.conv_w)
    vmem = 5 * (RB + 32) * cw * 4 + 3 * 2 * RB * cw * 2 + 12 * cfg.lru_chunk * cw * 4 + (4 << 20)
    return pl.pallas_call(
        kern,
        out_shape=jax.ShapeDtypeStruct((M, cfg.lru_width), BF16),
        grid=(cfg.batch, ng),
        in_specs=[pl.BlockSpec((RB, cw), lambda b, g: (b, cx + g)),
                  pl.BlockSpec((RB, cw), lambda b, g: (b, cg + g)),
                  pl.BlockSpec((None, cfg.conv_w, cw), lambda b, g: (l, 0, g)),
                  pl.BlockSpec((conv_b.shape[0], cw), lambda b, g: (0, g)),
                  pl.BlockSpec((None, 2, nb, blk, blk), lambda b, g: (l, 0, g, 0, 0)),
                  pl.BlockSpec((None, 2, cw), lambda b, g: (l, 0, g)),
                  pl.BlockSpec((None, 2, nb, blk, blk), lambda b, g: (l, 0, g, 0, 0)),
                  pl.BlockSpec((None, 2, cw), lambda b, g: (l, 0, g)),
                  pl.BlockSpec((None, 2, cw), lambda b, g: (l, 0, g))],
        out_specs=pl.BlockSpec((RB, cw), lambda b, g: (b, g)),
        scratch_shapes=[pltpu.VMEM((RB + 3 * V7X_SUBLANES, cw), F32)] + [pltpu.VMEM((RB, cw), F32)] * 4,
        compiler_params=_params(vmem, 2),
        name="rglru_mixer",
    )(p, p, conv_w, conv_b, w_r, b_r, w_i, b_i, lam)


def _merge_kernel(ya_ref, yb_ref, yc_ref, wa_ref, wb_ref, wc_ref, g0_ref, g1_ref, g2_ref, o_ref,
                  wab_ref, wbb_ref, wcb_ref):
    @pl.when(pl.program_id(1) == 0)
    def _():
        wab_ref[...] = wa_ref[...].astype(BF16)
        wbb_ref[...] = wb_ref[...].astype(BF16)
        wcb_ref[...] = wc_ref[...].astype(BF16)

    acc = jax.nn.sigmoid(g0_ref[...].astype(F32)) * jnp.dot(ya_ref[...], wab_ref[...], preferred_element_type=F32)
    acc += jax.nn.sigmoid(g1_ref[...].astype(F32)) * jnp.dot(yb_ref[...], wbb_ref[...], preferred_element_type=F32)
    acc += jax.nn.sigmoid(g2_ref[...].astype(F32)) * jnp.dot(yc_ref[...], wcb_ref[...], preferred_element_type=F32)
    o_ref[...] = acc.astype(o_ref.dtype)


def _merge(cfg, ya, yb, yc, p, w_na, w_gq, w_lru, l, offs):
    M = ya.shape[0]
    D = cfg.d_model
    bm, bn = cfg.merge_bm, cfg.merge_bn
    nj = _exact_div(D, bn)
    gcol = _exact_div(offs[8], bn)
    ks = (ya.shape[1], yb.shape[1], yc.shape[1])
    ksum = sum(ks)
    vmem = 2 * bm * ksum * 2 + 2 * ksum * bn * 4 + ksum * bn * 2 + 3 * 2 * bm * bn * 2 + 2 * bm * bn * 2 + 4 * bm * bn * 4
    x_specs = [pl.BlockSpec((bm, k), lambda j, i: (i, 0)) for k in ks]
    w_specs = [pl.BlockSpec((None, k, bn), lambda j, i: (l, 0, j)) for k in ks]
    g_specs = [pl.BlockSpec((bm, bn), functools.partial(lambda j, i, s: (i, gcol + s * nj + j), s=s))
               for s in range(3)]
    return pl.pallas_call(
        _merge_kernel,
        out_shape=jax.ShapeDtypeStruct((M, D), BF16),
        grid=(nj, _exact_div(M, bm)),
        in_specs=x_specs + w_specs + g_specs,
        out_specs=pl.BlockSpec((bm, bn), lambda j, i: (i, j)),
        scratch_shapes=[pltpu.VMEM((k, bn), BF16) for k in ks],
        compiler_params=_params(vmem, 2),
        name="branch_merge",
    )(ya, yb, yc, w_na, w_gq, w_lru, p, p, p)


def _forward(cfg, x, c, ctx, c_ctx, w_ada, b_ada, w_in, na_rpb, gq_q_gain, gq_k_gain, lru_conv_w,
             lru_conv_b, lru_w_r, lru_b_r, lru_w_i, lru_b_i, lru_lambda, w_br_na, w_br_gq, w_br_lru,
             w_o, ln1_g, ln1_b, ln2_g, ln2_b, ffn_w1, ffn_w3, ffn_w2, moe_router, moe_w1, moe_w3,
             moe_w2):
    B, D = cfg.batch, cfg.d_model
    offs, widths = _splits(cfg)
    in_cols = offs[-1] + widths[-1]
    bm, bn = cfg.mm_bm, cfg.mm_bn

    x_all = jnp.concatenate([ctx, x], axis=1).reshape(B * (cfg.ctx_len + cfg.seq), D)
    cond = jnp.concatenate([c, c_ctx[None, :], jnp.zeros((V7X_SUBLANES - B - 1, D), F32)], axis=0)
    b_ada3 = b_ada.reshape(cfg.depth, 1, 6 * D)
    mods = [_ada_table(cond, w_ada, b_ada3, l).reshape(V7X_SUBLANES, 6, D) for l in range(cfg.depth)]
    cos, sin = _rope_tables(cfg)

    u = _modulate(cfg, x_all, mods[0], 0, 1)
    for l in range(cfg.depth):
        need_ctx = l < cfg.depth - 1
        mod = mods[l]
        p = _matmul(u, [w_in], (l,), bm=bm, bn=bn, out_dtype=BF16, name="in_proj")
        assert p.shape[1] == in_cols
        bias = _na_bias(cfg, na_rpb[l])
        ya = _na_attention(cfg, p, bias, offs)
        yb = _gq_attention(cfg, p, gq_q_gain, gq_k_gain, cos, sin, l, offs)
        if need_ctx:
            ya = _ctx_attention(cfg, p, ya, cfg.na_heads, 1, offs[0], offs[1], offs[2], l)
            yb = _ctx_attention(cfg, p, yb, cfg.gq_kv, cfg.gq_heads // cfg.gq_kv, offs[3], offs[4],
                                offs[5], l, gains=(gq_q_gain, gq_k_gain))
        yc = _lru_mixer(cfg, p, lru_conv_w, lru_conv_b, lru_w_r, lru_b_r, lru_w_i, lru_b_i,
                        lru_lambda, l, offs)
        merged = _merge(cfg, ya, yb, yc, p, w_br_na, w_br_gq, w_br_lru, l, offs)
        y = _matmul(merged, [w_o], (l,), bm=bm, bn=bn, out_dtype=BF16, name="out_proj")

        dense = l % 2 == 0
        router = None
        if not dense:
            router = jnp.pad(moe_router[l // 2], ((0, 0), (0, V7X_LANES - cfg.n_experts)))
        res = _layer_norm(cfg, x_all, y, mod, 2, ln1_g, ln1_b, l, mod_next=mod, next_rows=(3, 4),
                          router=router, u_dtype=BF16 if dense else F32)
        x_all, u2 = res[0], res[1]
        if dense:
            h = _matmul(u2, [ffn_w1, ffn_w3], (l // 2,), bm=bm, bn=cfg.glu_bn, out_dtype=BF16,
                        glu=True, name="ffn_up")
            w2 = ffn_w2[l // 2].astype(BF16)[None]
            f = _matmul(h, [w2], (0,), bm=cfg.down_bm, bn=bn, out_dtype=BF16, name="ffn_down")
        else:
            top = res[2]
            if not need_ctx:
                top = top.reshape(B, cfg.ctx_len + cfg.seq, V7X_LANES)[:, cfg.ctx_len:, :]
                top = top.reshape(B * cfg.seq, V7X_LANES)
            rt = _route(cfg, top, latent_only=not need_ctx)
            xs = _gather_rows(u2, rt["src3"], BF16)
            h = _grouped_matmul(xs, [moe_w1, moe_w3], l // 2, rt["te"], rt["n_used"], bm=cfg.moe_bm,
                                bn=cfg.moe_up_bn, out_dtype=BF16, glu=True, name="moe_up")
            ys = _grouped_matmul(h, [moe_w2], l // 2, rt["te"], rt["n_used"], bm=cfg.moe_bm,
                                 bn=cfg.moe_down_bn, out_dtype=F32, name="moe_down")
            f = _combine(ys, rt["pos1"], rt["pos2"], top, BF16)
        f_dense = (not dense) and (not need_ctx)
        if l + 1 < cfg.depth:
            assert not f_dense
            x_all, u = _layer_norm(cfg, x_all, f, mod, 5, ln2_g, ln2_b, l, mod_next=mods[l + 1],
                                   next_rows=(0, 1))
        else:
            (x_lat,) = _layer_norm(cfg, x_all, f, mod, 5, ln2_g, ln2_b, l, latent_only=True,
                                   y_dense=f_dense)
    return x_lat.reshape(B, cfg.seq, D)


def kernel(x, c, ctx, c_ctx, w_ada, b_ada, w_in, na_rpb, gq_q_gain, gq_k_gain, lru_conv_w, lru_conv_b, lru_w_r, lru_b_r, lru_w_i, lru_b_i, lru_lambda, w_br_na, w_br_gq, w_br_lru, w_o, ln1_g, ln1_b, ln2_g, ln2_b, ffn_w1, ffn_w3, ffn_w2, moe_router, moe_w1, moe_w3, moe_w2):
    return _forward(FULL_CFG, x, c, ctx, c_ctx, w_ada, b_ada, w_in, na_rpb, gq_q_gain, gq_k_gain,
                    lru_conv_w, lru_conv_b, lru_w_r, lru_b_r, lru_w_i, lru_b_i, lru_lambda, w_br_na,
                    w_br_gq, w_br_lru, w_o, ln1_g, ln1_b, ln2_g, ln2_b, ffn_w1, ffn_w3, ffn_w2,
                    moe_router, moe_w1, moe_w3, moe_w2)
```

```python
import collections
import functools

import numpy as np
import jax
import jax.numpy as jnp
from jax import lax
from jax.experimental import pallas as pl
from jax.experimental.pallas import tpu as pltpu

F32 = jnp.float32
BF16 = jnp.bfloat16
LOG2_E = 1.4426950408889634

V7X_LANES = 128
V7X_SUBLANES = 8
V7X_VMEM_BYTES = 64 * 1024 * 1024
VMEM_CAP_BYTES = V7X_VMEM_BYTES - 6 * 1024 * 1024

Cfg = collections.namedtuple(
    "Cfg",
    "d_model batch seq depth ctx_len grid_w head_dim na_heads na_win_h na_win_w gq_heads gq_kv "
    "rope_theta lru_width lru_blocks conv_w lru_c d_ff n_experts d_ff_expert eps neg_inf "
    "row_tile mm_bm mm_bn glu_bn merge_bm merge_bn down_bm moe_bm gather_rows moe_up_bn moe_down_bn na_qrows na_hps gq_tq gq_kc lru_cw "
    "lru_chunk")

FULL_CFG = Cfg(
    d_model=4096, batch=2, seq=4096, depth=2, ctx_len=256, grid_w=64, head_dim=128,
    na_heads=12, na_win_h=8, na_win_w=16, gq_heads=16, gq_kv=4, rope_theta=10000.0,
    lru_width=1536, lru_blocks=12, conv_w=4, lru_c=8.0, d_ff=11008, n_experts=8,
    d_ff_expert=3072, eps=1e-6, neg_inf=-1e30,
    row_tile=256, mm_bm=1088, mm_bn=512, glu_bn=256, merge_bm=512, merge_bn=512, down_bm=512,
    moe_bm=512, gather_rows=256, moe_up_bn=512, moe_down_bn=1024, na_qrows=4, na_hps=4, gq_tq=256, gq_kc=1024, lru_cw=256, lru_chunk=512)


def _splits(cfg):
    na_w = cfg.na_heads * cfg.head_dim
    gq_q = cfg.gq_heads * cfg.head_dim
    gq_kv = cfg.gq_kv * cfg.head_dim
    widths = (na_w, na_w, na_w, gq_q, gq_kv, gq_kv, cfg.lru_width, cfg.lru_width, 3 * cfg.d_model)
    offs = tuple(int(v) for v in np.cumsum((0,) + widths[:-1]))
    return offs, widths


def _exact_div(a, b):
    assert a % b == 0, (a, b)
    return a // b


def _params(vmem_bytes, n_axes):
    limit = int(min(VMEM_CAP_BYTES, max(vmem_bytes * 5 // 4 + (4 << 20), 16 << 20)))
    return pltpu.CompilerParams(dimension_semantics=("arbitrary",) * n_axes, vmem_limit_bytes=limit)


def _dot_tile(x, wb_refs, glu):
    acc = jnp.dot(x, wb_refs[0][...], preferred_element_type=F32)
    if glu:
        acc = jax.nn.silu(acc) * jnp.dot(x, wb_refs[1][...], preferred_element_type=F32)
    return acc


def _mm_kernel(*refs, n_w, cast_w, glu):
    x_ref = refs[0]
    w_refs = refs[1:1 + n_w]
    o_ref = refs[1 + n_w]
    wb_refs = refs[2 + n_w:] if cast_w else w_refs

    if cast_w:
        @pl.when(pl.program_id(1) == 0)
        def _():
            for w_ref, wb_ref in zip(w_refs, wb_refs):
                wb_ref[...] = w_ref[...].astype(BF16)

    o_ref[...] = _dot_tile(x_ref[...], wb_refs, glu).astype(o_ref.dtype)


def _gmm_kernel(te_ref, nu_ref, *refs, n_w, glu):
    x_ref = refs[0]
    w_refs = refs[1:1 + n_w]
    o_ref = refs[1 + n_w]
    wb_refs = refs[2 + n_w:]
    i = pl.program_id(1)
    prev = te_ref[jnp.maximum(i - 1, 0)]

    @pl.when(jnp.logical_or(i == 0, te_ref[i] != prev))
    def _():
        for w_ref, wb_ref in zip(w_refs, wb_refs):
            wb_ref[...] = w_ref[...].astype(BF16)

    @pl.when(i < nu_ref[0])
    def _():
        o_ref[...] = _dot_tile(x_ref[...], wb_refs, glu).astype(o_ref.dtype)

    @pl.when(i >= nu_ref[0])
    def _():
        o_ref[...] = jnp.zeros(o_ref.shape, o_ref.dtype)


def _mm_vmem_bytes(n_w, bm, bn, K, wbytes, cast_w, out_dtype):
    return (2 * bm * K * 2 + n_w * (2 * K * bn * wbytes + (K * bn * 2 if cast_w else 0))
            + 2 * bm * bn * jnp.dtype(out_dtype).itemsize + 3 * bm * bn * 4)


def _matmul(x, ws, w_lead, *, bm, bn, out_dtype, glu=False, name="mm"):
    M, K = x.shape
    N = ws[0].shape[-1]
    assert ws[0].shape[-2] == K
    cast_w = ws[0].dtype != BF16
    lead = tuple(w_lead)
    in_specs = [pl.BlockSpec((bm, K), lambda j, i: (i, 0))]
    in_specs += [pl.BlockSpec((None,) * len(lead) + (K, bn), lambda j, i: lead + (0, j)) for _ in ws]
    scratch = [pltpu.VMEM((K, bn), BF16) for _ in ws] if cast_w else []
    vmem = _mm_vmem_bytes(len(ws), bm, bn, K, ws[0].dtype.itemsize, cast_w, out_dtype)
    return pl.pallas_call(
        functools.partial(_mm_kernel, n_w=len(ws), cast_w=cast_w, glu=glu),
        out_shape=jax.ShapeDtypeStruct((M, N), out_dtype),
        grid=(_exact_div(N, bn), _exact_div(M, bm)),
        in_specs=in_specs,
        out_specs=pl.BlockSpec((bm, bn), lambda j, i: (i, j)),
        scratch_shapes=scratch,
        compiler_params=_params(vmem, 2),
        name=name,
    )(x, *ws)


def _grouped_matmul(x, ws, l, tile_expert, n_used, *, bm, bn, out_dtype, glu=False, name="gmm"):
    R, K = x.shape
    N = ws[0].shape[-1]
    assert ws[0].shape[-2] == K
    in_specs = [pl.BlockSpec((bm, K), lambda j, i, te, nu: (i, 0))]
    in_specs += [pl.BlockSpec((None, None, K, bn), lambda j, i, te, nu: (l, te[i], 0, j)) for _ in ws]
    vmem = _mm_vmem_bytes(len(ws), bm, bn, K, 4, True, out_dtype)
    grid_spec = pltpu.PrefetchScalarGridSpec(
        num_scalar_prefetch=2,
        grid=(_exact_div(N, bn), _exact_div(R, bm)),
        in_specs=in_specs,
        out_specs=pl.BlockSpec((bm, bn), lambda j, i, te, nu: (i, j)),
        scratch_shapes=[pltpu.VMEM((K, bn), BF16) for _ in ws])
    return pl.pallas_call(
        functools.partial(_gmm_kernel, n_w=len(ws), glu=glu),
        out_shape=jax.ShapeDtypeStruct((R, N), out_dtype),
        grid_spec=grid_spec,
        compiler_params=_params(vmem, 2),
        name=name,
    )(tile_expert, n_used, x, *ws)


def _row_copy(src_hbm, row, dst_vmem, r, sem):
    return pltpu.make_async_copy(src_hbm.at[pl.ds(row, 1), :], dst_vmem.at[pl.ds(r, 1), :], sem)


def _gather_kernel(src_ref, nxt_ref, x_hbm, o_ref, buf_ref, sems):
    t = pl.program_id(0)
    rows = o_ref.shape[0]
    slot = t % 2

    def issue_tile(idx_ref, s):
        def body(r, c):
            _row_copy(x_hbm, idx_ref[0, r], buf_ref.at[s], r, sems.at[s]).start()
            return c
        lax.fori_loop(0, rows, body, 0, unroll=8)

    @pl.when(t == 0)
    def _():
        issue_tile(src_ref, 0)

    @pl.when(t + 1 < pl.num_programs(0))
    def _():
        issue_tile(nxt_ref, 1 - slot)

    def wait(r, c):
        _row_copy(x_hbm, 0, buf_ref.at[slot], r, sems.at[slot]).wait()
        return c

    lax.fori_loop(0, rows, wait, 0, unroll=8)
    o_ref[...] = buf_ref[slot].astype(o_ref.dtype)


def _gather_rows(x, src, rows, out_dtype):
    n_tiles = _exact_div(src.shape[0], rows)
    src3 = src.reshape(n_tiles, 1, rows)
    D = x.shape[1]
    vmem = 2 * rows * D * 4 + 2 * rows * D * jnp.dtype(out_dtype).itemsize + 2 * rows * D * 4
    return pl.pallas_call(
        _gather_kernel,
        out_shape=jax.ShapeDtypeStruct((n_tiles * rows, D), out_dtype),
        grid=(n_tiles,),
        in_specs=[pl.BlockSpec((None, 1, rows), lambda t: (t, 0, 0), memory_space=pltpu.SMEM),
                  pl.BlockSpec((None, 1, rows), lambda t: (jnp.minimum(t + 1, n_tiles - 1), 0, 0),
                               memory_space=pltpu.SMEM),
                  pl.BlockSpec(memory_space=pl.ANY)],
        out_specs=pl.BlockSpec((rows, D), lambda t: (t, 0)),
        scratch_shapes=[pltpu.VMEM((2, rows, D), x.dtype), pltpu.SemaphoreType.DMA((2,))],
        compiler_params=_params(vmem, 1),
        name="moe_gather",
    )(src3, src3, x)


def _combine_kernel(p1_ref, p2_ref, top_ref, y_hbm, o_ref, a_ref, b_ref, sems):
    bm = o_ref.shape[0]

    def issue(r, c):
        _row_copy(y_hbm, p1_ref[0, r], a_ref, r, sems.at[0]).start(priority=0)
        _row_copy(y_hbm, p2_ref[0, r], b_ref, r, sems.at[1]).start(priority=1)
        return c

    def wait(r, c):
        _row_copy(y_hbm, 0, a_ref, r, sems.at[0]).wait()
        _row_copy(y_hbm, 0, b_ref, r, sems.at[1]).wait()
        return c

    lax.fori_loop(0, bm, issue, 0, unroll=8)
    lax.fori_loop(0, bm, wait, 0, unroll=8)
    top = top_ref[...]
    o_ref[...] = (top[:, 2:3] * a_ref[...] + top[:, 3:4] * b_ref[...]).astype(o_ref.dtype)


def _combine(ys, pos1, pos2, top, out_dtype):
    n_tiles, _, bm = pos1.shape
    D = ys.shape[1]
    M = n_tiles * bm
    vmem = 2 * bm * D * 4 + 2 * bm * D * 4 + 2 * bm * D * 4
    idx_spec = pl.BlockSpec((None, 1, bm), lambda t: (t, 0, 0), memory_space=pltpu.SMEM)
    return pl.pallas_call(
        _combine_kernel,
        out_shape=jax.ShapeDtypeStruct((M, D), out_dtype),
        grid=(n_tiles,),
        in_specs=[idx_spec, idx_spec,
                  pl.BlockSpec((bm, V7X_LANES), lambda t: (t, 0)),
                  pl.BlockSpec(memory_space=pl.ANY)],
        out_specs=pl.BlockSpec((bm, D), lambda t: (t, 0)),
        scratch_shapes=[pltpu.VMEM((bm, D), ys.dtype), pltpu.VMEM((bm, D), ys.dtype),
                        pltpu.SemaphoreType.DMA((2,))],
        compiler_params=_params(vmem, 1),
        name="moe_combine",
    )(pos1, pos2, top, ys)


def _route(cfg, top, latent_only):
    T = top.shape[0]
    E, bm = cfg.n_experts, cfg.moe_bm
    P = 2 * T
    n_tiles = _exact_div(P, bm) + E
    experts = jnp.arange(E, dtype=jnp.int32)
    ef = jnp.clip(top[:, :2].astype(jnp.int32), 0, E - 1).reshape(P)
    onehot = (ef[:, None] == experts[None, :]).astype(jnp.int32)
    csum = jnp.cumsum(onehot, axis=0)
    n_e = csum[-1]
    padded = ((n_e + bm - 1) // bm) * bm
    gstart = jnp.cumsum(padded) - padded
    ustart = jnp.cumsum(n_e) - n_e
    dest = jnp.sum((csum - 1 + gstart[None, :]) * onehot, axis=1).reshape(T, 2)
    n_used = (jnp.sum(padded) // bm).astype(jnp.int32).reshape(1)
    tstart = jnp.arange(n_tiles, dtype=jnp.int32) * bm
    gend = gstart + padded
    te = jnp.minimum(jnp.sum((tstart[:, None] >= gend[None, :]).astype(jnp.int32), axis=1), E - 1)
    tile_oh = (te[:, None] == experts[None, :]).astype(jnp.int32)
    tile_n = jnp.sum(tile_oh * n_e[None, :], axis=1)
    k = ((tstart - jnp.sum(tile_oh * gstart[None, :], axis=1))[:, None]
         + jnp.arange(bm, dtype=jnp.int32)[None, :])
    s = jnp.clip(jnp.sum(tile_oh * ustart[None, :], axis=1)[:, None] + k, 0, P - 1)
    order = jnp.argsort(ef, stable=True).astype(jnp.int32)
    tok = jnp.where(k < tile_n[:, None], order[s] // 2, 0)
    if latent_only:
        tok = (tok // cfg.seq) * (cfg.ctx_len + cfg.seq) + cfg.ctx_len + tok % cfg.seq
    tiles_t = _exact_div(T, bm)
    return dict(te=te, n_used=n_used, src=tok.reshape(n_tiles * bm),
                pos1=dest[:, 0].reshape(tiles_t, 1, bm), pos2=dest[:, 1].reshape(tiles_t, 1, bm))


def _ada_kernel(c_ref, w_ref, b_ref, o_ref):
    c = c_ref[...]
    s = (c * jax.nn.sigmoid(c)).astype(BF16)
    acc = jnp.dot(s, w_ref[...].astype(BF16), preferred_element_type=F32)
    o_ref[...] = acc + b_ref[...]


def _ada_table(cond, w_ada, b_ada3, l, bn=1024):
    R, D = cond.shape
    N = w_ada.shape[-1]
    vmem = 2 * D * bn * 4 + D * bn * 2 + 4 * R * bn * 4 + 2 * R * D * 4
    return pl.pallas_call(
        _ada_kernel,
        out_shape=jax.ShapeDtypeStruct((R, N), F32),
        grid=(_exact_div(N, bn),),
        in_specs=[pl.BlockSpec((R, D), lambda j: (0, 0)),
                  pl.BlockSpec((None, D, bn), lambda j: (l, 0, j)),
                  pl.BlockSpec((None, 1, bn), lambda j: (l, 0, j))],
        out_specs=pl.BlockSpec((R, bn), lambda j: (0, j)),
        compiler_params=_params(vmem, 1),
        name="ada_table",
    )(cond, w_ada, b_ada3)


def _mod_sel(cfg):
    tiles_per_batch = _exact_div(cfg.ctx_len + cfg.seq, cfg.row_tile)
    ctx_tiles = _exact_div(cfg.ctx_len, cfg.row_tile)

    def sel(i):
        return jnp.where(i % tiles_per_batch < ctx_tiles, cfg.batch, i // tiles_per_batch)
    return sel


def _modulate_kernel(x_ref, m_ref, u_ref, *, shift_row, scale_row):
    m = m_ref[...]
    u = x_ref[...] * (1.0 + m[scale_row:scale_row + 1, :]) + m[shift_row:shift_row + 1, :]
    u_ref[...] = u.astype(u_ref.dtype)


def _modulate(cfg, x_all, modtab, shift_row, scale_row):
    M, D = x_all.shape
    rt = cfg.row_tile
    sel = _mod_sel(cfg)
    vmem = 2 * rt * D * 4 + 2 * rt * D * 2 + 2 * 8 * D * 4 + 2 * rt * D * 4
    return pl.pallas_call(
        functools.partial(_modulate_kernel, shift_row=shift_row, scale_row=scale_row),
        out_shape=jax.ShapeDtypeStruct((M, D), BF16),
        grid=(_exact_div(M, rt),),
        in_specs=[pl.BlockSpec((rt, D), lambda i: (i, 0)),
                  pl.BlockSpec((None, 6, D), lambda i: (sel(i), 0, 0))],
        out_specs=pl.BlockSpec((rt, D), lambda i: (i, 0)),
        compiler_params=_params(vmem, 1),
        name="modulate",
    )(x_all, modtab)


def _ln_kernel(*refs, alpha, eps, gate_row, l, next_rows, n_experts):
    it = iter(refs)
    x_ref, y_ref, mc_ref = next(it), next(it), next(it)
    mn_ref = next(it) if next_rows is not None else None
    g_ref, b_ref = next(it), next(it)
    r_ref = next(it) if n_experts else None
    xo_ref = next(it)
    u_ref = next(it) if next_rows is not None else None
    go_ref = next(it) if n_experts else None

    gate = mc_ref[gate_row:gate_row + 1, :]
    z = alpha * x_ref[...] + gate * y_ref[...].astype(F32)
    mu = jnp.mean(z, axis=-1, keepdims=True)
    zc = z - mu
    var = jnp.mean(zc * zc, axis=-1, keepdims=True)
    xn = zc * lax.rsqrt(var + eps) * g_ref[l:l + 1, :] + b_ref[l:l + 1, :]
    xo_ref[...] = xn
    if next_rows is not None:
        shift_row, scale_row = next_rows
        mn = mn_ref[...]
        u = xn * (1.0 + mn[scale_row:scale_row + 1, :]) + mn[shift_row:shift_row + 1, :]
        u_ref[...] = u.astype(u_ref.dtype)
        if n_experts:
            logits = jnp.dot(u, r_ref[...], preferred_element_type=F32,
                             precision=lax.Precision.HIGHEST)
            lane = lax.broadcasted_iota(jnp.int32, logits.shape, 1)
            neg = jnp.float32(-jnp.inf)
            lg = jnp.where(lane < n_experts, logits, neg)
            m1 = jnp.max(lg, axis=-1, keepdims=True)
            i1 = jnp.min(jnp.where(lg == m1, lane, V7X_LANES), axis=-1, keepdims=True)
            lg2 = jnp.where(lane == i1, neg, lg)
            m2 = jnp.max(lg2, axis=-1, keepdims=True)
            i2 = jnp.min(jnp.where(lg2 == m2, lane, V7X_LANES), axis=-1, keepdims=True)
            e2 = jnp.exp(m2 - m1)
            w1 = 1.0 / (1.0 + e2)
            w2 = e2 / (1.0 + e2)
            packed = jnp.where(lane == 0, i1.astype(F32), jnp.where(lane == 1, i2.astype(F32), 0.0))
            go_ref[...] = packed + jnp.where(lane == 2, w1, 0.0) + jnp.where(lane == 3, w2, 0.0)


def _layer_norm(cfg, x_all, y, mod_cur, gate_row, ln_g, ln_b, l, mod_next=None, next_rows=None,
                router=None, u_dtype=BF16, latent_only=False, y_dense=False):
    M, D = x_all.shape
    rt = cfg.row_tile
    alpha = float((2 * cfg.depth) ** 0.25)
    n_experts = cfg.n_experts if router is not None else 0
    if latent_only:
        assert next_rows is None and router is None
        lat_tiles = _exact_div(cfg.seq, rt)
        tiles_b = _exact_div(cfg.ctx_len + cfg.seq, rt)
        ctx_tiles = _exact_div(cfg.ctx_len, rt)
        n_tiles = cfg.batch * lat_tiles
        in_row = pl.BlockSpec((rt, D), lambda i: ((i // lat_tiles) * tiles_b + ctx_tiles + i % lat_tiles, 0))
        mod_spec = pl.BlockSpec((None, 6, D), lambda i: (i // lat_tiles, 0, 0))
        out_rows = cfg.batch * cfg.seq
    else:
        sel = _mod_sel(cfg)
        n_tiles = _exact_div(M, rt)
        in_row = pl.BlockSpec((rt, D), lambda i: (i, 0))
        mod_spec = pl.BlockSpec((None, 6, D), lambda i: (sel(i), 0, 0))
        out_rows = M
    row_spec = pl.BlockSpec((rt, D), lambda i: (i, 0))
    par_spec = pl.BlockSpec(ln_g.shape, lambda i: (0, 0))
    assert latent_only or not y_dense
    in_specs = [in_row, row_spec if y_dense else in_row, mod_spec]
    args = [x_all, y, mod_cur]
    if next_rows is not None:
        in_specs.append(mod_spec)
        args.append(mod_next)
    in_specs += [par_spec, par_spec]
    args += [ln_g, ln_b]
    out_shape = [jax.ShapeDtypeStruct((out_rows, D), F32)]
    out_specs = [row_spec]
    if next_rows is not None:
        out_shape.append(jax.ShapeDtypeStruct((M, D), u_dtype))
        out_specs.append(row_spec)
    if n_experts:
        in_specs.append(pl.BlockSpec(router.shape, lambda i: (0, 0)))
        args.append(router)
        out_shape.append(jax.ShapeDtypeStruct((M, V7X_LANES), F32))
        out_specs.append(pl.BlockSpec((rt, V7X_LANES), lambda i: (i, 0)))
    vmem = 2 * rt * D * (4 + 4 + 4 + 2) + 4 * 8 * D * 4 + 4 * rt * D * 4 + (2 * D * 128 * 4 * 4 if n_experts else 0)
    kern = functools.partial(_ln_kernel, alpha=alpha, eps=cfg.eps, gate_row=gate_row, l=l,
                             next_rows=next_rows, n_experts=n_experts)
    return pl.pallas_call(
        kern,
        out_shape=out_shape,
        grid=(n_tiles,),
        in_specs=in_specs,
        out_specs=out_specs,
        compiler_params=_params(vmem, 1),
        name="deepnorm_ln",
    )(*args)


def _rms_rope(x, gain, cos, sin, eps):
    xn = x * lax.rsqrt(jnp.mean(x * x, axis=-1, keepdims=True) + eps) * gain
    if cos is None:
        return xn
    d = x.shape[-1]
    q = d // 4
    lane = lax.broadcasted_iota(jnp.int32, xn.shape, 1)
    fwd = pltpu.roll(xn, d - q, axis=1)
    bwd = pltpu.roll(xn, q, axis=1)
    swapped = jnp.where(lane % (2 * q) < q, fwd, bwd)
    return xn * cos + swapped * sin


def _dot_nt(a, b):
    return lax.dot_general(a, b, (((1,), (1,)), ((), ())), preferred_element_type=F32)


def _attend_chunks(q, chunks):
    m = l = acc = None
    for k, v, bias in chunks:
        s = _dot_nt(q, k)
        if bias is not None:
            s = s + bias
        cm = jnp.max(s, axis=-1, keepdims=True)
        if m is None:
            m_new = cm
            p = jnp.exp(s - m_new)
            l = jnp.sum(p, axis=-1, keepdims=True)
            acc = jnp.dot(p.astype(BF16), v, preferred_element_type=F32)
        else:
            m_new = jnp.maximum(m, cm)
            a = jnp.exp(m - m_new)
            p = jnp.exp(s - m_new)
            l = a * l + jnp.sum(p, axis=-1, keepdims=True)
            acc = a * acc + jnp.dot(p.astype(BF16), v, preferred_element_type=F32)
        m = m_new
    return acc / l


def _attend_chunks_base2(q, kt_ref, vx_ref, bounds, hd):
    m = acc = None
    for s0, n in bounds:
        s = jnp.dot(q, kt_ref[:, s0:s0 + n], preferred_element_type=F32)
        cm = jnp.max(s, axis=-1, keepdims=True)
        if m is None:
            m_new = cm
            acc = jnp.dot(jnp.exp2(s - m_new).astype(BF16), vx_ref[s0:s0 + n, :], preferred_element_type=F32)
        else:
            m_new = jnp.maximum(m, cm)
            pv = jnp.dot(jnp.exp2(s - m_new).astype(BF16), vx_ref[s0:s0 + n, :], preferred_element_type=F32)
            acc = jnp.exp2(m - m_new) * acc + pv
        m = m_new
    return acc[:, :hd] / acc[:, hd:2 * hd]


def _na_kernel(q_ref, k_ref, v_ref, bias_ref, _, o_ref, *, ctx_len, grid_w, qrows, krows, n_rows, scale,
               hd, heads):
    t = pl.program_id(2)
    ks = jnp.clip(qrows * t - qrows, 0, n_rows - krows)
    start = pl.multiple_of(ctx_len + ks * grid_w, 16)
    nk = krows * grid_w
    for h in range(heads):
        cols = slice(h * hd, (h + 1) * hd)
        q = (q_ref[:, cols].astype(F32) * scale).astype(BF16)
        k_win = k_ref[pl.ds(start, nk), cols]
        v_win = v_ref[pl.ds(start, nk), cols]
        k_ctx = k_ref[0:ctx_len, cols]
        v_ctx = v_ref[0:ctx_len, cols]
        o = _attend_chunks(q, [(k_win, v_win, bias_ref[h]), (k_ctx, v_ctx, None)])
        o_ref[:, cols] = o.astype(o_ref.dtype)


def _na_bias(cfg, rpb):
    W = cfg.grid_w
    n_rows = _exact_div(cfg.seq, W)
    qr, kr = cfg.na_qrows, cfg.na_qrows + cfg.na_win_h
    n_t = _exact_div(n_rows, qr)
    wh, ww = cfg.na_win_h, cfg.na_win_w
    roffs, valids = [], []
    for t in (0, 1, n_t - 1):
        ks = int(np.clip(qr * t - qr, 0, n_rows - kr))
        r = qr * t + np.arange(qr)
        rs = np.clip(r - wh // 2, 0, n_rows - wh)
        kabs = ks + np.arange(kr)
        vrow = (kabs[None, :] >= rs[:, None]) & (kabs[None, :] < rs[:, None] + wh)
        roff = np.clip(kabs[None, :] - r[:, None] + (wh - 1), 0, 2 * wh - 2)
        roffs.append(roff)
        valids.append(vrow)
    roff = np.stack(roffs)
    vrow = np.stack(valids)
    cols = np.arange(W)
    cs = np.clip(cols - ww // 2, 0, W - ww)
    vcol = (cols[None, :] >= cs[:, None]) & (cols[None, :] < cs[:, None] + ww)
    coff = np.clip(cols[None, :] - cols[:, None], -(ww - 1), ww - 1) + (ww - 1)
    n_ro, n_co = 2 * wh - 1, 2 * ww - 1
    onehot = (np.arange(n_co)[:, None] == coff.reshape(1, W * W)).astype(np.float32)
    H = rpb.shape[0]
    colb = jnp.dot(rpb.astype(F32).reshape(H * n_ro, n_co), jnp.asarray(onehot),
                   precision=lax.Precision.HIGHEST).reshape(H, n_ro, W, W)
    colb = jnp.where(vcol[None, None], colb, cfg.neg_inf)
    masked = jnp.full((H, W, W), cfg.neg_inf, F32)
    variants = []
    for v in range(3):
        rows = []
        for a in range(qr):
            tiles = [colb[:, int(roff[v, a, b])] if vrow[v, a, b] else masked for b in range(kr)]
            rows.append(jnp.concatenate(tiles, axis=-1))
        variants.append(jnp.concatenate(rows, axis=1))
    return jnp.stack(variants, axis=1)


def _na_attention(cfg, p, bias, offs):
    M = p.shape[0]
    W, hd = cfg.grid_w, cfg.head_dim
    RB = cfg.ctx_len + cfg.seq
    n_rows = _exact_div(cfg.seq, W)
    qr, kr = cfg.na_qrows, cfg.na_qrows + cfg.na_win_h
    n_t = _exact_div(n_rows, qr)
    qb = qr * W
    q0 = _exact_div(cfg.ctx_len, qb)
    tiles_b = _exact_div(RB, qb)
    hps = cfg.na_hps
    hw = hps * hd
    cq, ck, cv = (_exact_div(offs[i], hw) for i in range(3))

    def q_map(b, h, t):
        return (b * tiles_b + q0 + t, cq + h)

    def var_map(b, h, t):
        return (h, jnp.where(t == 0, 0, jnp.where(t == n_t - 1, 2, 1)), 0, 0)

    kern = functools.partial(_na_kernel, ctx_len=cfg.ctx_len, grid_w=W, qrows=qr, krows=kr,
                             n_rows=n_rows, scale=float(hd) ** -0.5, hd=hd, heads=hps)
    vmem = hps * (4 * RB * hd * 2 + 2 * qb * kr * W * 4 + 6 * qb * (kr * W + cfg.ctx_len) * 4)
    return pl.pallas_call(
        kern,
        out_shape=jax.ShapeDtypeStruct((M, cfg.na_heads * hd), BF16),
        grid=(cfg.batch, _exact_div(cfg.na_heads, hps), n_t),
        in_specs=[pl.BlockSpec((qb, hw), q_map),
                  pl.BlockSpec((RB, hw), lambda b, h, t: (b, ck + h)),
                  pl.BlockSpec((RB, hw), lambda b, h, t: (b, cv + h)),
                  pl.BlockSpec((hps, None, qb, kr * W), var_map),
                  pl.BlockSpec(memory_space=pl.ANY)],
        out_specs=pl.BlockSpec((qb, hw), lambda b, h, t: (b * tiles_b + q0 + t, h)),
        input_output_aliases={4: 0},
        compiler_params=_params(vmem, 3),
        name="na_attention",
    )(p, p, p, bias, jnp.zeros((M, cfg.na_heads * hd), BF16))


def _gq_kernel(q_ref, k_ref, v_ref, qg_ref, kg_ref, cos_ref, sin_ref, _, o_ref, kt_ref, vx_ref, qn_ref, *,
               l, groups, hd, tq, kc, ctx_len, seq, eps, scale):
    qi = pl.program_id(2)

    @pl.when(qi == 0)
    def _():
        cos = cos_ref[...]
        sin = sin_ref[...]
        kn = _rms_rope(k_ref[...].astype(F32), kg_ref[l:l + 1, :], cos, sin, eps)
        kt_ref[...] = kn.T.astype(BF16)
        vx_ref[:, 0:hd] = v_ref[...]
        vx_ref[:, hd:2 * hd] = jnp.ones((v_ref.shape[0], hd), BF16)
        gain = qg_ref[l:l + 1, :] * (scale * LOG2_E)
        for g in range(groups):
            qg = _rms_rope(q_ref[:, g * hd:(g + 1) * hd].astype(F32), gain, cos, sin, eps)
            qn_ref[g] = qg.astype(BF16)

    row0 = pl.multiple_of(ctx_len + qi * tq, 16)
    q = jnp.concatenate([qn_ref[g, pl.ds(row0, tq), :] for g in range(groups)], axis=0)
    bounds = [(0, ctx_len)] + [(ctx_len + c * kc, kc) for c in range(seq // kc)]
    o = _attend_chunks_base2(q, kt_ref, vx_ref, bounds, hd)
    for g in range(groups):
        o_ref[:, g * hd:(g + 1) * hd] = o[g * tq:(g + 1) * tq, :].astype(o_ref.dtype)


def _rope_tables(cfg):
    hd = cfg.head_dim
    q = hd // 4
    t = np.arange(cfg.seq)
    inv = cfg.rope_theta ** (-np.arange(q, dtype=np.float64) / q)
    ang_r = (t // cfg.grid_w)[:, None] * inv[None, :]
    ang_c = (t % cfg.grid_w)[:, None] * inv[None, :]
    cos = np.concatenate([np.cos(ang_r), np.cos(ang_r), np.cos(ang_c), np.cos(ang_c)], axis=1)
    sin = np.concatenate([-np.sin(ang_r), np.sin(ang_r), -np.sin(ang_c), np.sin(ang_c)], axis=1)
    cos = np.concatenate([np.ones((cfg.ctx_len, hd)), cos], axis=0)
    sin = np.concatenate([np.zeros((cfg.ctx_len, hd)), sin], axis=0)
    return jnp.asarray(cos, F32), jnp.asarray(sin, F32)


def _gq_attention(cfg, p, q_gain, k_gain, cos, sin, l, offs):
    M = p.shape[0]
    hd = cfg.head_dim
    RB = cfg.ctx_len + cfg.seq
    G = _exact_div(cfg.gq_heads, cfg.gq_kv)
    tq = cfg.gq_tq
    nq = _exact_div(cfg.seq, tq)
    q0 = _exact_div(cfg.ctx_len, tq)
    tiles_b = _exact_div(RB, tq)
    cq = _exact_div(offs[3], G * hd)
    ck = _exact_div(offs[4], hd)
    cv = _exact_div(offs[5], hd)
    kc = cfg.gq_kc
    _exact_div(cfg.seq, kc)
    kern = functools.partial(_gq_kernel, l=l, groups=G, hd=hd, tq=tq, kc=kc, ctx_len=cfg.ctx_len,
                             seq=cfg.seq, eps=cfg.eps, scale=float(hd) ** -0.5)
    vmem = (4 * RB * hd * 2 + 2 * RB * G * hd * 2 + (3 + G) * RB * hd * 2 + 4 * RB * hd * 4
            + 6 * RB * hd * 4 + 2 * tq * G * hd * 2 + 6 * G * tq * kc * 4)
    return pl.pallas_call(
        kern,
        out_shape=jax.ShapeDtypeStruct((M, cfg.gq_heads * hd), BF16),
        grid=(cfg.batch, cfg.gq_kv, nq),
        in_specs=[pl.BlockSpec((RB, G * hd), lambda b, k, i: (b, cq + k)),
                  pl.BlockSpec((RB, hd), lambda b, k, i: (b, ck + k)),
                  pl.BlockSpec((RB, hd), lambda b, k, i: (b, cv + k)),
                  pl.BlockSpec(q_gain.shape, lambda b, k, i: (0, 0)),
                  pl.BlockSpec(k_gain.shape, lambda b, k, i: (0, 0)),
                  pl.BlockSpec((RB, hd), lambda b, k, i: (0, 0)),
                  pl.BlockSpec((RB, hd), lambda b, k, i: (0, 0)),
                  pl.BlockSpec(memory_space=pl.ANY)],
        out_specs=pl.BlockSpec((tq, G * hd), lambda b, k, i: (b * tiles_b + q0 + i, k)),
        scratch_shapes=[pltpu.VMEM((hd, RB), BF16), pltpu.VMEM((RB, 2 * hd), BF16),
                        pltpu.VMEM((G, RB, hd), BF16)],
        input_output_aliases={7: 0},
        compiler_params=_params(vmem, 3),
        name="gq_attention",
    )(p, p, p, q_gain, k_gain, cos, sin, jnp.zeros((M, cfg.gq_heads * hd), BF16))


def _ctx_attn_kernel(*refs, l, groups, hd, norm, eps, scale):
    if norm:
        q_ref, k_ref, v_ref, qg_ref, kg_ref, _, o_ref = refs
    else:
        q_ref, k_ref, v_ref, _, o_ref = refs
    k = k_ref[...]
    if norm:
        k = _rms_rope(k.astype(F32), kg_ref[l:l + 1, :], None, None, eps).astype(BF16)
    n = q_ref.shape[0]
    qs = []
    for g in range(groups):
        qg = q_ref[:, g * hd:(g + 1) * hd].astype(F32)
        if norm:
            qg = _rms_rope(qg, qg_ref[l:l + 1, :] * scale, None, None, eps)
        else:
            qg = qg * scale
        qs.append(qg.astype(BF16))
    q = jnp.concatenate(qs, axis=0)
    o = _attend_chunks(q, [(k, v_ref[...], None)])
    for g in range(groups):
        o_ref[:, g * hd:(g + 1) * hd] = o[g * n:(g + 1) * n, :].astype(o_ref.dtype)


def _ctx_attention(cfg, p, y_prev, n_kv, groups, col_q, col_k, col_v, l, gains=None):
    hd = cfg.head_dim
    C = cfg.ctx_len
    tiles_b = _exact_div(cfg.ctx_len + cfg.seq, C)
    cq = _exact_div(col_q, groups * hd)
    ck = _exact_div(col_k, hd)
    cv = _exact_div(col_v, hd)
    norm = gains is not None
    in_specs = [pl.BlockSpec((C, groups * hd), lambda b, k: (b * tiles_b, cq + k)),
                pl.BlockSpec((C, hd), lambda b, k: (b * tiles_b, ck + k)),
                pl.BlockSpec((C, hd), lambda b, k: (b * tiles_b, cv + k))]
    args = [p, p, p]
    if norm:
        in_specs += [pl.BlockSpec(gains[0].shape, lambda b, k: (0, 0)),
                     pl.BlockSpec(gains[1].shape, lambda b, k: (0, 0))]
        args += list(gains)
    in_specs.append(pl.BlockSpec(memory_space=pl.ANY))
    args.append(y_prev)
    kern = functools.partial(_ctx_attn_kernel, l=l, groups=groups, hd=hd, norm=norm, eps=cfg.eps,
                             scale=float(hd) ** -0.5)
    return pl.pallas_call(
        kern,
        out_shape=jax.ShapeDtypeStruct(y_prev.shape, y_prev.dtype),
        grid=(cfg.batch, n_kv),
        in_specs=in_specs,
        out_specs=pl.BlockSpec((C, groups * hd), lambda b, k: (b * tiles_b, k)),
        input_output_aliases={len(args) - 1: 0},
        compiler_params=_params(8 << 20, 2),
        name="ctx_attention",
    )(*args)


def _tile_scan(a, b, reverse):
    n = a.shape[0]
    rows = lax.broadcasted_iota(jnp.int32, a.shape, 0)
    d = 1
    while d < n:
        shift = n - d if reverse else d
        keep = rows < n - d if reverse else rows >= d
        a_prev = jnp.where(keep, pltpu.roll(a, shift, axis=0), 1.0)
        b_prev = jnp.where(keep, pltpu.roll(b, shift, axis=0), 0.0)
        b = a * b_prev + b
        a = a * a_prev
        d *= 2
    return a, b


def _lru_kernel(x_ref, gate_ref, cw_ref, cb_ref, wr_ref, br_ref, wi_ref, bi_ref, lam_ref, o_ref,
                xpad_ref, af_ref, bf_ref, ab_ref, bb_ref, *, l, ctx_len, seq, chunk, lru_c, conv_w):
    cwid = x_ref.shape[1]
    nblk = cwid // V7X_LANES
    pad = V7X_SUBLANES
    ctx_base = pad
    lat_base = ctx_base + ctx_len + pad
    zeros = jnp.zeros((pad, cwid), F32)
    xpad_ref[0:pad, :] = zeros
    xpad_ref[ctx_base + ctx_len:lat_base, :] = zeros
    xpad_ref[lat_base + seq:lat_base + seq + pad, :] = zeros
    xpad_ref[ctx_base:ctx_base + ctx_len, :] = x_ref[0:ctx_len, :].astype(F32)
    xpad_ref[lat_base:lat_base + seq, :] = x_ref[ctx_len:ctx_len + seq, :].astype(F32)

    taps = cw_ref[...]
    cbias = cb_ref[l:l + 1, :]
    lam = lam_ref[...]
    nl = -lam
    softplus = jnp.maximum(nl, 0.0) + jnp.log1p(jnp.exp(-jnp.abs(nl)))
    a_refs = (af_ref, ab_ref)
    b_refs = (bf_ref, bb_ref)

    def coeffs(src_base, dst_row, rows):
        xc = cbias
        for tap in range(conv_w):
            xc = xc + taps[tap:tap + 1, :] * xpad_ref[pl.ds(src_base + tap - conv_w // 2, rows), :]
        xcb = xc.astype(BF16)
        for d in range(2):
            rs, is_ = [], []
            for j in range(nblk):
                xj = xcb[:, j * V7X_LANES:(j + 1) * V7X_LANES]
                rs.append(jnp.dot(xj, wr_ref[d, j].astype(BF16), preferred_element_type=F32))
                is_.append(jnp.dot(xj, wi_ref[d, j].astype(BF16), preferred_element_type=F32))
            gr = jax.nn.sigmoid(jnp.concatenate(rs, axis=1) + br_ref[d:d + 1, :])
            gi = jax.nn.sigmoid(jnp.concatenate(is_, axis=1) + bi_ref[d:d + 1, :])
            log_a = -lru_c * gr * softplus[d:d + 1, :]
            a = jnp.exp(log_a)
            a_refs[d][pl.ds(dst_row, rows), :] = a
            one_minus_a2 = -jnp.tanh(log_a) * (1.0 + a * a)
            b_refs[d][pl.ds(dst_row, rows), :] = jnp.sqrt(one_minus_a2) * (gi * xc)

    coeffs(ctx_base, 0, ctx_len)
    for c in range(seq // chunk):
        coeffs(lat_base + c * chunk, ctx_len + c * chunk, chunk)

    nsub = V7X_SUBLANES

    def scan_seg(first, n, h):
        def body(i, hs):
            hf, hb = hs
            rf = pl.multiple_of(first + nsub * i, nsub)
            rb = pl.multiple_of(first + n - nsub - nsub * i, nsub)
            ca, cb = _tile_scan(af_ref[pl.ds(rf, nsub), :], bf_ref[pl.ds(rf, nsub), :], False)
            tile_f = ca * hf + cb
            bf_ref[pl.ds(rf, nsub), :] = tile_f
            ca, cb = _tile_scan(ab_ref[pl.ds(rb, nsub), :], bb_ref[pl.ds(rb, nsub), :], True)
            tile_b = ca * hb + cb
            bb_ref[pl.ds(rb, nsub), :] = tile_b
            return tile_f[nsub - 1:nsub, :], tile_b[0:1, :]
        return lax.fori_loop(0, n // nsub, body, h, unroll=2)

    h0 = jnp.zeros((1, cwid), F32)
    hs = scan_seg(0, ctx_len, (h0, h0))
    scan_seg(ctx_len, seq, hs)

    def out_body(c, carry):
        r0 = pl.multiple_of(c * ctx_len, ctx_len)
        h = bf_ref[pl.ds(r0, ctx_len), :] + bb_ref[pl.ds(r0, ctx_len), :]
        g = gate_ref[pl.ds(r0, ctx_len), :].astype(F32)
        o_ref[pl.ds(r0, ctx_len), :] = (h * jax.nn.gelu(g)).astype(o_ref.dtype)
        return carry

    lax.fori_loop(0, (ctx_len + seq) // ctx_len, out_body, 0)


def _lru_mixer(cfg, p, conv_w, conv_b, w_r, b_r, w_i, b_i, lam, l, offs):
    M = p.shape[0]
    RB = cfg.ctx_len + cfg.seq
    cw = cfg.lru_cw
    ng = _exact_div(cfg.lru_width, cw)
    blk = _exact_div(cfg.lru_width, cfg.lru_blocks)
    assert blk == V7X_LANES
    nb = _exact_div(cw, blk)
    cx = _exact_div(offs[6], cw)
    cg = _exact_div(offs[7], cw)
    _exact_div(cfg.seq, cfg.lru_chunk)
    _exact_div(RB, cfg.ctx_len)
    kern = functools.partial(_lru_kernel, l=l, ctx_len=cfg.ctx_len, seq=cfg.seq, chunk=cfg.lru_chunk,
                             lru_c=cfg.lru_c, conv_w=cfg.conv_w)
    vmem = 5 * (RB + 32) * cw * 4 + 3 * 2 * RB * cw * 2 + 12 * cfg.lru_chunk * cw * 4 + (4 << 20)
    return pl.pallas_call(
        kern,
        out_shape=jax.ShapeDtypeStruct((M, cfg.lru_width), BF16),
        grid=(cfg.batch, ng),
        in_specs=[pl.BlockSpec((RB, cw), lambda b, g: (b, cx + g)),
                  pl.BlockSpec((RB, cw), lambda b, g: (b, cg + g)),
                  pl.BlockSpec((None, cfg.conv_w, cw), lambda b, g: (l, 0, g)),
                  pl.BlockSpec((conv_b.shape[0], cw), lambda b, g: (0, g)),
                  pl.BlockSpec((None, 2, nb, blk, blk), lambda b, g: (l, 0, g, 0, 0)),
                  pl.BlockSpec((None, 2, cw), lambda b, g: (l, 0, g)),
                  pl.BlockSpec((None, 2, nb, blk, blk), lambda b, g: (l, 0, g, 0, 0)),
                  pl.BlockSpec((None, 2, cw), lambda b, g: (l, 0, g)),
                  pl.BlockSpec((None, 2, cw), lambda b, g: (l, 0, g))],
        out_specs=pl.BlockSpec((RB, cw), lambda b, g: (b, g)),
        scratch_shapes=[pltpu.VMEM((RB + 3 * V7X_SUBLANES, cw), F32)] + [pltpu.VMEM((RB, cw), F32)] * 4,
        compiler_params=_params(vmem, 2),
        name="rglru_mixer",
    )(p, p, conv_w, conv_b, w_r, b_r, w_i, b_i, lam)


def _merge_kernel(ya_ref, yb_ref, yc_ref, wa_ref, wb_ref, wc_ref, g0_ref, g1_ref, g2_ref, o_ref,
                  wab_ref, wbb_ref, wcb_ref):
    @pl.when(pl.program_id(1) == 0)
    def _():
        wab_ref[...] = wa_ref[...].astype(BF16)
        wbb_ref[...] = wb_ref[...].astype(BF16)
        wcb_ref[...] = wc_ref[...].astype(BF16)

    acc = jax.nn.sigmoid(g0_ref[...].astype(F32)) * jnp.dot(ya_ref[...], wab_ref[...], preferred_element_type=F32)
    acc += jax.nn.sigmoid(g1_ref[...].astype(F32)) * jnp.dot(yb_ref[...], wbb_ref[...], preferred_element_type=F32)
    acc += jax.nn.sigmoid(g2_ref[...].astype(F32)) * jnp.dot(yc_ref[...], wcb_ref[...], preferred_element_type=F32)
    o_ref[...] = acc.astype(o_ref.dtype)


def _merge(cfg, ya, yb, yc, p, w_na, w_gq, w_lru, l, offs):
    M = ya.shape[0]
    D = cfg.d_model
    bm, bn = cfg.merge_bm, cfg.merge_bn
    nj = _exact_div(D, bn)
    gcol = _exact_div(offs[8], bn)
    ks = (ya.shape[1], yb.shape[1], yc.shape[1])
    ksum = sum(ks)
    vmem = 2 * bm * ksum * 2 + 2 * ksum * bn * 4 + ksum * bn * 2 + 3 * 2 * bm * bn * 2 + 2 * bm * bn * 2 + 4 * bm * bn * 4
    x_specs = [pl.BlockSpec((bm, k), lambda j, i: (i, 0)) for k in ks]
    w_specs = [pl.BlockSpec((None, k, bn), lambda j, i: (l, 0, j)) for k in ks]
    g_specs = [pl.BlockSpec((bm, bn), functools.partial(lambda j, i, s: (i, gcol + s * nj + j), s=s))
               for s in range(3)]
    return pl.pallas_call(
        _merge_kernel,
        out_shape=jax.ShapeDtypeStruct((M, D), BF16),
        grid=(nj, _exact_div(M, bm)),
        in_specs=x_specs + w_specs + g_specs,
        out_specs=pl.BlockSpec((bm, bn), lambda j, i: (i, j)),
        scratch_shapes=[pltpu.VMEM((k, bn), BF16) for k in ks],
        compiler_params=_params(vmem, 2),
        name="branch_merge",
    )(ya, yb, yc, w_na, w_gq, w_lru, p, p, p)


def _forward(cfg, x, c, ctx, c_ctx, w_ada, b_ada, w_in, na_rpb, gq_q_gain, gq_k_gain, lru_conv_w,
             lru_conv_b, lru_w_r, lru_b_r, lru_w_i, lru_b_i, lru_lambda, w_br_na, w_br_gq, w_br_lru,
             w_o, ln1_g, ln1_b, ln2_g, ln2_b, ffn_w1, ffn_w3, ffn_w2, moe_router, moe_w1, moe_w3,
             moe_w2):
    B, D = cfg.batch, cfg.d_model
    offs, widths = _splits(cfg)
    in_cols = offs[-1] + widths[-1]
    bm, bn = cfg.mm_bm, cfg.mm_bn

    x_all = jnp.concatenate([ctx, x], axis=1).reshape(B * (cfg.ctx_len + cfg.seq), D)
    cond = jnp.concatenate([c, c_ctx[None, :], jnp.zeros((V7X_SUBLANES - B - 1, D), F32)], axis=0)
    b_ada3 = b_ada.reshape(cfg.depth, 1, 6 * D)
    mods = [_ada_table(cond, w_ada, b_ada3, l).reshape(V7X_SUBLANES, 6, D) for l in range(cfg.depth)]
    cos, sin = _rope_tables(cfg)

    u = _modulate(cfg, x_all, mods[0], 0, 1)
    for l in range(cfg.depth):
        need_ctx = l < cfg.depth - 1
        mod = mods[l]
        p = _matmul(u, [w_in], (l,), bm=bm, bn=bn, out_dtype=BF16, name="in_proj")
        assert p.shape[1] == in_cols
        bias = _na_bias(cfg, na_rpb[l])
        ya = _na_attention(cfg, p, bias, offs)
        yb = _gq_attention(cfg, p, gq_q_gain, gq_k_gain, cos, sin, l, offs)
        if need_ctx:
            ya = _ctx_attention(cfg, p, ya, cfg.na_heads, 1, offs[0], offs[1], offs[2], l)
            yb = _ctx_attention(cfg, p, yb, cfg.gq_kv, cfg.gq_heads // cfg.gq_kv, offs[3], offs[4],
                                offs[5], l, gains=(gq_q_gain, gq_k_gain))
        yc = _lru_mixer(cfg, p, lru_conv_w, lru_conv_b, lru_w_r, lru_b_r, lru_w_i, lru_b_i,
                        lru_lambda, l, offs)
        merged = _merge(cfg, ya, yb, yc, p, w_br_na, w_br_gq, w_br_lru, l, offs)
        y = _matmul(merged, [w_o], (l,), bm=bm, bn=bn, out_dtype=BF16, name="out_proj")

        dense = l % 2 == 0
        router = None
        if not dense:
            router = jnp.pad(moe_router[l // 2], ((0, 0), (0, V7X_LANES - cfg.n_experts)))
        res = _layer_norm(cfg, x_all, y, mod, 2, ln1_g, ln1_b, l, mod_next=mod, next_rows=(3, 4),
                          router=router, u_dtype=BF16 if dense else F32)
        x_all, u2 = res[0], res[1]
        if dense:
            h = _matmul(u2, [ffn_w1, ffn_w3], (l // 2,), bm=bm, bn=cfg.glu_bn, out_dtype=BF16,
                        glu=True, name="ffn_up")
            w2 = ffn_w2[l // 2].astype(BF16)[None]
            f = _matmul(h, [w2], (0,), bm=cfg.down_bm, bn=bn, out_dtype=BF16, name="ffn_down")
        else:
            top = res[2]
            if not need_ctx:
                top = top.reshape(B, cfg.ctx_len + cfg.seq, V7X_LANES)[:, cfg.ctx_len:, :]
                top = top.reshape(B * cfg.seq, V7X_LANES)
            rt = _route(cfg, top, latent_only=not need_ctx)
            xs = _gather_rows(u2, rt["src"], cfg.gather_rows, BF16)
            h = _grouped_matmul(xs, [moe_w1, moe_w3], l // 2, rt["te"], rt["n_used"], bm=cfg.moe_bm,
                                bn=cfg.moe_up_bn, out_dtype=BF16, glu=True, name="moe_up")
            ys = _grouped_matmul(h, [moe_w2], l // 2, rt["te"], rt["n_used"], bm=cfg.moe_bm,
                                 bn=cfg.moe_down_bn, out_dtype=F32, name="moe_down")
            f = _combine(ys, rt["pos1"], rt["pos2"], top, BF16)
        f_dense = (not dense) and (not need_ctx)
        if l + 1 < cfg.depth:
            assert not f_dense
            x_all, u = _layer_norm(cfg, x_all, f, mod, 5, ln2_g, ln2_b, l, mod_next=mods[l + 1],
                                   next_rows=(0, 1))
        else:
            (x_lat,) = _layer_norm(cfg, x_all, f, mod, 5, ln2_g, ln2_b, l, latent_only=True,
                                   y_dense=f_dense)
    return x_lat.reshape(B, cfg.seq, D)


def kernel(x, c, ctx, c_ctx, w_ada, b_ada, w_in, na_rpb, gq_q_gain, gq_k_gain, lru_conv_w, lru_conv_b, lru_w_r, lru_b_r, lru_w_i, lru_b_i, lru_lambda, w_br_na, w_br_gq, w_br_lru, w_o, ln1_g, ln1_b, ln2_g, ln2_b, ffn_w1, ffn_w3, ffn_w2, moe_router, moe_w1, moe_w3, moe_w2):
    return _forward(FULL_CFG, x, c, ctx, c_ctx, w_ada, b_ada, w_in, na_rpb, gq_q_gain, gq_k_gain,
                    lru_conv_w, lru_conv_b, lru_w_r, lru_b_r, lru_w_i, lru_b_i, lru_lambda, w_br_na,
                    w_br_gq, w_br_lru, w_o, ln1_g, ln1_b, ln2_g, ln2_b, ffn_w1, ffn_w3, ffn_w2,
                    moe_router, moe_w1, moe_w3, moe_w2)
```

```python
import collections
import functools

import numpy as np
import jax
import jax.numpy as jnp
from jax import lax
from jax.experimental import pallas as pl
from jax.experimental.pallas import tpu as pltpu

F32 = jnp.float32
BF16 = jnp.bfloat16
LOG2_E = 1.4426950408889634

V7X_LANES = 128
V7X_SUBLANES = 8
V7X_VMEM_BYTES = 64 * 1024 * 1024
VMEM_CAP_BYTES = V7X_VMEM_BYTES - 6 * 1024 * 1024

Cfg = collections.namedtuple(
    "Cfg",
    "d_model batch seq depth ctx_len grid_w head_dim na_heads na_win_h na_win_w gq_heads gq_kv "
    "rope_theta lru_width lru_blocks conv_w lru_c d_ff n_experts d_ff_expert eps neg_inf "
    "row_tile mm_bm mm_bn glu_bn merge_bm merge_bn down_bm moe_bm gather_rows moe_up_bn moe_down_bn na_qrows na_hps gq_tq gq_kc lru_cw "
    "lru_chunk")

FULL_CFG = Cfg(
    d_model=4096, batch=2, seq=4096, depth=2, ctx_len=256, grid_w=64, head_dim=128,
    na_heads=12, na_win_h=8, na_win_w=16, gq_heads=16, gq_kv=4, rope_theta=10000.0,
    lru_width=1536, lru_blocks=12, conv_w=4, lru_c=8.0, d_ff=11008, n_experts=8,
    d_ff_expert=3072, eps=1e-6, neg_inf=-1e30,
    row_tile=256, mm_bm=1088, mm_bn=512, glu_bn=256, merge_bm=512, merge_bn=512, down_bm=512,
    moe_bm=512, gather_rows=256, moe_up_bn=512, moe_down_bn=1024, na_qrows=4, na_hps=6, gq_tq=256, gq_kc=1024, lru_cw=256, lru_chunk=512)


def _splits(cfg):
    na_w = cfg.na_heads * cfg.head_dim
    gq_q = cfg.gq_heads * cfg.head_dim
    gq_kv = cfg.gq_kv * cfg.head_dim
    widths = (na_w, na_w, na_w, gq_q, gq_kv, gq_kv, cfg.lru_width, cfg.lru_width, 3 * cfg.d_model)
    offs = tuple(int(v) for v in np.cumsum((0,) + widths[:-1]))
    return offs, widths


def _exact_div(a, b):
    assert a % b == 0, (a, b)
    return a // b


def _params(vmem_bytes, n_axes):
    limit = int(min(VMEM_CAP_BYTES, max(vmem_bytes * 5 // 4 + (4 << 20), 16 << 20)))
    return pltpu.CompilerParams(dimension_semantics=("arbitrary",) * n_axes, vmem_limit_bytes=limit)


def _dot_tile(x, wb_refs, glu):
    acc = jnp.dot(x, wb_refs[0][...], preferred_element_type=F32)
    if glu:
        acc = jax.nn.silu(acc) * jnp.dot(x, wb_refs[1][...], preferred_element_type=F32)
    return acc


def _mm_kernel(*refs, n_w, cast_w, glu):
    x_ref = refs[0]
    w_refs = refs[1:1 + n_w]
    o_ref = refs[1 + n_w]
    wb_refs = refs[2 + n_w:] if cast_w else w_refs

    if cast_w:
        @pl.when(pl.program_id(1) == 0)
        def _():
            for w_ref, wb_ref in zip(w_refs, wb_refs):
                wb_ref[...] = w_ref[...].astype(BF16)

    o_ref[...] = _dot_tile(x_ref[...], wb_refs, glu).astype(o_ref.dtype)


def _gmm_kernel(te_ref, nu_ref, *refs, n_w, glu):
    x_ref = refs[0]
    w_refs = refs[1:1 + n_w]
    o_ref = refs[1 + n_w]
    wb_refs = refs[2 + n_w:]
    i = pl.program_id(1)
    prev = te_ref[jnp.maximum(i - 1, 0)]

    @pl.when(jnp.logical_or(i == 0, te_ref[i] != prev))
    def _():
        for w_ref, wb_ref in zip(w_refs, wb_refs):
            wb_ref[...] = w_ref[...].astype(BF16)

    @pl.when(i < nu_ref[0])
    def _():
        o_ref[...] = _dot_tile(x_ref[...], wb_refs, glu).astype(o_ref.dtype)

    @pl.when(i >= nu_ref[0])
    def _():
        o_ref[...] = jnp.zeros(o_ref.shape, o_ref.dtype)


def _mm_vmem_bytes(n_w, bm, bn, K, wbytes, cast_w, out_dtype):
    return (2 * bm * K * 2 + n_w * (2 * K * bn * wbytes + (K * bn * 2 if cast_w else 0))
            + 2 * bm * bn * jnp.dtype(out_dtype).itemsize + 3 * bm * bn * 4)


def _matmul(x, ws, w_lead, *, bm, bn, out_dtype, glu=False, name="mm"):
    M, K = x.shape
    N = ws[0].shape[-1]
    assert ws[0].shape[-2] == K
    cast_w = ws[0].dtype != BF16
    lead = tuple(w_lead)
    in_specs = [pl.BlockSpec((bm, K), lambda j, i: (i, 0))]
    in_specs += [pl.BlockSpec((None,) * len(lead) + (K, bn), lambda j, i: lead + (0, j)) for _ in ws]
    scratch = [pltpu.VMEM((K, bn), BF16) for _ in ws] if cast_w else []
    vmem = _mm_vmem_bytes(len(ws), bm, bn, K, ws[0].dtype.itemsize, cast_w, out_dtype)
    return pl.pallas_call(
        functools.partial(_mm_kernel, n_w=len(ws), cast_w=cast_w, glu=glu),
        out_shape=jax.ShapeDtypeStruct((M, N), out_dtype),
        grid=(_exact_div(N, bn), _exact_div(M, bm)),
        in_specs=in_specs,
        out_specs=pl.BlockSpec((bm, bn), lambda j, i: (i, j)),
        scratch_shapes=scratch,
        compiler_params=_params(vmem, 2),
        name=name,
    )(x, *ws)


def _grouped_matmul(x, ws, l, tile_expert, n_used, *, bm, bn, out_dtype, glu=False, name="gmm"):
    R, K = x.shape
    N = ws[0].shape[-1]
    assert ws[0].shape[-2] == K
    in_specs = [pl.BlockSpec((bm, K), lambda j, i, te, nu: (i, 0))]
    in_specs += [pl.BlockSpec((None, None, K, bn), lambda j, i, te, nu: (l, te[i], 0, j)) for _ in ws]
    vmem = _mm_vmem_bytes(len(ws), bm, bn, K, 4, True, out_dtype)
    grid_spec = pltpu.PrefetchScalarGridSpec(
        num_scalar_prefetch=2,
        grid=(_exact_div(N, bn), _exact_div(R, bm)),
        in_specs=in_specs,
        out_specs=pl.BlockSpec((bm, bn), lambda j, i, te, nu: (i, j)),
        scratch_shapes=[pltpu.VMEM((K, bn), BF16) for _ in ws])
    return pl.pallas_call(
        functools.partial(_gmm_kernel, n_w=len(ws), glu=glu),
        out_shape=jax.ShapeDtypeStruct((R, N), out_dtype),
        grid_spec=grid_spec,
        compiler_params=_params(vmem, 2),
        name=name,
    )(tile_expert, n_used, x, *ws)


def _row_copy(src_hbm, row, dst_vmem, r, sem):
    return pltpu.make_async_copy(src_hbm.at[pl.ds(row, 1), :], dst_vmem.at[pl.ds(r, 1), :], sem)


def _gather_kernel(src_ref, nxt_ref, x_hbm, o_ref, buf_ref, sems):
    t = pl.program_id(0)
    rows = o_ref.shape[0]
    slot = t % 2

    def issue_tile(idx_ref, s):
        def body(r, c):
            _row_copy(x_hbm, idx_ref[0, r], buf_ref.at[s], r, sems.at[s]).start()
            return c
        lax.fori_loop(0, rows, body, 0, unroll=8)

    @pl.when(t == 0)
    def _():
        issue_tile(src_ref, 0)

    @pl.when(t + 1 < pl.num_programs(0))
    def _():
        issue_tile(nxt_ref, 1 - slot)

    def wait(r, c):
        _row_copy(x_hbm, 0, buf_ref.at[slot], r, sems.at[slot]).wait()
        return c

    lax.fori_loop(0, rows, wait, 0, unroll=8)
    o_ref[...] = buf_ref[slot].astype(o_ref.dtype)


def _gather_rows(x, src, rows, out_dtype):
    n_tiles = _exact_div(src.shape[0], rows)
    src3 = src.reshape(n_tiles, 1, rows)
    D = x.shape[1]
    vmem = 2 * rows * D * 4 + 2 * rows * D * jnp.dtype(out_dtype).itemsize + 2 * rows * D * 4
    return pl.pallas_call(
        _gather_kernel,
        out_shape=jax.ShapeDtypeStruct((n_tiles * rows, D), out_dtype),
        grid=(n_tiles,),
        in_specs=[pl.BlockSpec((None, 1, rows), lambda t: (t, 0, 0), memory_space=pltpu.SMEM),
                  pl.BlockSpec((None, 1, rows), lambda t: (jnp.minimum(t + 1, n_tiles - 1), 0, 0),
                               memory_space=pltpu.SMEM),
                  pl.BlockSpec(memory_space=pl.ANY)],
        out_specs=pl.BlockSpec((rows, D), lambda t: (t, 0)),
        scratch_shapes=[pltpu.VMEM((2, rows, D), x.dtype), pltpu.SemaphoreType.DMA((2,))],
        compiler_params=_params(vmem, 1),
        name="moe_gather",
    )(src3, src3, x)


def _combine_kernel(p1_ref, p2_ref, top_ref, y_hbm, o_ref, a_ref, b_ref, sems):
    bm = o_ref.shape[0]

    def issue(r, c):
        _row_copy(y_hbm, p1_ref[0, r], a_ref, r, sems.at[0]).start(priority=0)
        _row_copy(y_hbm, p2_ref[0, r], b_ref, r, sems.at[1]).start(priority=1)
        return c

    def wait(r, c):
        _row_copy(y_hbm, 0, a_ref, r, sems.at[0]).wait()
        _row_copy(y_hbm, 0, b_ref, r, sems.at[1]).wait()
        return c

    lax.fori_loop(0, bm, issue, 0, unroll=8)
    lax.fori_loop(0, bm, wait, 0, unroll=8)
    top = top_ref[...]
    o_ref[...] = (top[:, 2:3] * a_ref[...] + top[:, 3:4] * b_ref[...]).astype(o_ref.dtype)


def _combine(ys, pos1, pos2, top, out_dtype):
    n_tiles, _, bm = pos1.shape
    D = ys.shape[1]
    M = n_tiles * bm
    vmem = 2 * bm * D * 4 + 2 * bm * D * 4 + 2 * bm * D * 4
    idx_spec = pl.BlockSpec((None, 1, bm), lambda t: (t, 0, 0), memory_space=pltpu.SMEM)
    return pl.pallas_call(
        _combine_kernel,
        out_shape=jax.ShapeDtypeStruct((M, D), out_dtype),
        grid=(n_tiles,),
        in_specs=[idx_spec, idx_spec,
                  pl.BlockSpec((bm, V7X_LANES), lambda t: (t, 0)),
                  pl.BlockSpec(memory_space=pl.ANY)],
        out_specs=pl.BlockSpec((bm, D), lambda t: (t, 0)),
        scratch_shapes=[pltpu.VMEM((bm, D), ys.dtype), pltpu.VMEM((bm, D), ys.dtype),
                        pltpu.SemaphoreType.DMA((2,))],
        compiler_params=_params(vmem, 1),
        name="moe_combine",
    )(pos1, pos2, top, ys)


def _route(cfg, top, latent_only):
    T = top.shape[0]
    E, bm = cfg.n_experts, cfg.moe_bm
    P = 2 * T
    n_tiles = _exact_div(P, bm) + E
    experts = jnp.arange(E, dtype=jnp.int32)
    ef = jnp.clip(top[:, :2].astype(jnp.int32), 0, E - 1).reshape(P)
    onehot = (ef[:, None] == experts[None, :]).astype(jnp.int32)
    csum = jnp.cumsum(onehot, axis=0)
    n_e = csum[-1]
    padded = ((n_e + bm - 1) // bm) * bm
    gstart = jnp.cumsum(padded) - padded
    ustart = jnp.cumsum(n_e) - n_e
    dest = jnp.sum((csum - 1 + gstart[None, :]) * onehot, axis=1).reshape(T, 2)
    n_used = (jnp.sum(padded) // bm).astype(jnp.int32).reshape(1)
    tstart = jnp.arange(n_tiles, dtype=jnp.int32) * bm
    gend = gstart + padded
    te = jnp.minimum(jnp.sum((tstart[:, None] >= gend[None, :]).astype(jnp.int32), axis=1), E - 1)
    tile_oh = (te[:, None] == experts[None, :]).astype(jnp.int32)
    tile_n = jnp.sum(tile_oh * n_e[None, :], axis=1)
    k = ((tstart - jnp.sum(tile_oh * gstart[None, :], axis=1))[:, None]
         + jnp.arange(bm, dtype=jnp.int32)[None, :])
    s = jnp.clip(jnp.sum(tile_oh * ustart[None, :], axis=1)[:, None] + k, 0, P - 1)
    order = jnp.argsort(ef, stable=True).astype(jnp.int32)
    tok = jnp.where(k < tile_n[:, None], order[s] // 2, 0)
    if latent_only:
        tok = (tok // cfg.seq) * (cfg.ctx_len + cfg.seq) + cfg.ctx_len + tok % cfg.seq
    tiles_t = _exact_div(T, bm)
    return dict(te=te, n_used=n_used, src=tok.reshape(n_tiles * bm),
                pos1=dest[:, 0].reshape(tiles_t, 1, bm), pos2=dest[:, 1].reshape(tiles_t, 1, bm))


def _ada_kernel(c_ref, w_ref, b_ref, o_ref):
    c = c_ref[...]
    s = (c * jax.nn.sigmoid(c)).astype(BF16)
    acc = jnp.dot(s, w_ref[...].astype(BF16), preferred_element_type=F32)
    o_ref[...] = acc + b_ref[...]


def _ada_table(cond, w_ada, b_ada3, l, bn=1024):
    R, D = cond.shape
    N = w_ada.shape[-1]
    vmem = 2 * D * bn * 4 + D * bn * 2 + 4 * R * bn * 4 + 2 * R * D * 4
    return pl.pallas_call(
        _ada_kernel,
        out_shape=jax.ShapeDtypeStruct((R, N), F32),
        grid=(_exact_div(N, bn),),
        in_specs=[pl.BlockSpec((R, D), lambda j: (0, 0)),
                  pl.BlockSpec((None, D, bn), lambda j: (l, 0, j)),
                  pl.BlockSpec((None, 1, bn), lambda j: (l, 0, j))],
        out_specs=pl.BlockSpec((R, bn), lambda j: (0, j)),
        compiler_params=_params(vmem, 1),
        name="ada_table",
    )(cond, w_ada, b_ada3)


def _mod_sel(cfg):
    tiles_per_batch = _exact_div(cfg.ctx_len + cfg.seq, cfg.row_tile)
    ctx_tiles = _exact_div(cfg.ctx_len, cfg.row_tile)

    def sel(i):
        return jnp.where(i % tiles_per_batch < ctx_tiles, cfg.batch, i // tiles_per_batch)
    return sel


def _modulate_kernel(x_ref, c_ref, m_ref, xo_ref, u_ref, *, tiles_b, ctx_tiles, shift_row, scale_row):
    pos = pl.program_id(0) % tiles_b
    m = m_ref[...]

    def emit(v):
        xo_ref[...] = v
        u = v * (1.0 + m[scale_row:scale_row + 1, :]) + m[shift_row:shift_row + 1, :]
        u_ref[...] = u.astype(u_ref.dtype)

    @pl.when(pos < ctx_tiles)
    def _():
        emit(c_ref[...])

    @pl.when(pos >= ctx_tiles)
    def _():
        emit(x_ref[...])


def _modulate(cfg, x, ctx, modtab, shift_row, scale_row):
    B, D = cfg.batch, cfg.d_model
    rt = cfg.row_tile
    M = B * (cfg.ctx_len + cfg.seq)
    sel = _mod_sel(cfg)
    tiles_b = _exact_div(cfg.ctx_len + cfg.seq, rt)
    ctx_tiles = _exact_div(cfg.ctx_len, rt)
    lat_tiles = _exact_div(cfg.seq, rt)
    vmem = 4 * rt * D * 4 + 2 * rt * D * 4 + 2 * rt * D * 2 + 2 * 8 * D * 4 + 2 * rt * D * 4
    kern = functools.partial(_modulate_kernel, tiles_b=tiles_b, ctx_tiles=ctx_tiles,
                             shift_row=shift_row, scale_row=scale_row)
    row_spec = pl.BlockSpec((rt, D), lambda i: (i, 0))
    return pl.pallas_call(
        kern,
        out_shape=[jax.ShapeDtypeStruct((M, D), F32), jax.ShapeDtypeStruct((M, D), BF16)],
        grid=(_exact_div(M, rt),),
        in_specs=[pl.BlockSpec((rt, D), lambda i: ((i // tiles_b) * lat_tiles
                                                   + jnp.maximum(i % tiles_b - ctx_tiles, 0), 0)),
                  pl.BlockSpec((rt, D), lambda i: ((i // tiles_b) * ctx_tiles
                                                   + jnp.minimum(i % tiles_b, ctx_tiles - 1), 0)),
                  pl.BlockSpec((None, 6, D), lambda i: (sel(i), 0, 0))],
        out_specs=[row_spec, row_spec],
        compiler_params=_params(vmem, 1),
        name="modulate",
    )(x.reshape(B * cfg.seq, D), ctx.reshape(B * cfg.ctx_len, D), modtab)


def _ln_kernel(*refs, alpha, eps, gate_row, l, next_rows, n_experts):
    it = iter(refs)
    x_ref, y_ref, mc_ref = next(it), next(it), next(it)
    mn_ref = next(it) if next_rows is not None else None
    g_ref, b_ref = next(it), next(it)
    r_ref = next(it) if n_experts else None
    xo_ref = next(it)
    u_ref = next(it) if next_rows is not None else None
    go_ref = next(it) if n_experts else None

    gate = mc_ref[gate_row:gate_row + 1, :]
    z = alpha * x_ref[...] + gate * y_ref[...].astype(F32)
    mu = jnp.mean(z, axis=-1, keepdims=True)
    zc = z - mu
    var = jnp.mean(zc * zc, axis=-1, keepdims=True)
    xn = zc * lax.rsqrt(var + eps) * g_ref[l:l + 1, :] + b_ref[l:l + 1, :]
    xo_ref[...] = xn
    if next_rows is not None:
        shift_row, scale_row = next_rows
        mn = mn_ref[...]
        u = xn * (1.0 + mn[scale_row:scale_row + 1, :]) + mn[shift_row:shift_row + 1, :]
        u_ref[...] = u.astype(u_ref.dtype)
        if n_experts:
            logits = jnp.dot(u, r_ref[...], preferred_element_type=F32,
                             precision=lax.Precision.HIGHEST)
            lane = lax.broadcasted_iota(jnp.int32, logits.shape, 1)
            neg = jnp.float32(-jnp.inf)
            lg = jnp.where(lane < n_experts, logits, neg)
            m1 = jnp.max(lg, axis=-1, keepdims=True)
            i1 = jnp.min(jnp.where(lg == m1, lane, V7X_LANES), axis=-1, keepdims=True)
            lg2 = jnp.where(lane == i1, neg, lg)
            m2 = jnp.max(lg2, axis=-1, keepdims=True)
            i2 = jnp.min(jnp.where(lg2 == m2, lane, V7X_LANES), axis=-1, keepdims=True)
            e2 = jnp.exp(m2 - m1)
            w1 = 1.0 / (1.0 + e2)
            w2 = e2 / (1.0 + e2)
            packed = jnp.where(lane == 0, i1.astype(F32), jnp.where(lane == 1, i2.astype(F32), 0.0))
            go_ref[...] = packed + jnp.where(lane == 2, w1, 0.0) + jnp.where(lane == 3, w2, 0.0)


def _layer_norm(cfg, x_all, y, mod_cur, gate_row, ln_g, ln_b, l, mod_next=None, next_rows=None,
                router=None, u_dtype=BF16, latent_only=False, y_dense=False):
    M, D = x_all.shape
    rt = cfg.row_tile
    alpha = float((2 * cfg.depth) ** 0.25)
    n_experts = cfg.n_experts if router is not None else 0
    if latent_only:
        assert next_rows is None and router is None
        lat_tiles = _exact_div(cfg.seq, rt)
        tiles_b = _exact_div(cfg.ctx_len + cfg.seq, rt)
        ctx_tiles = _exact_div(cfg.ctx_len, rt)
        n_tiles = cfg.batch * lat_tiles
        in_row = pl.BlockSpec((rt, D), lambda i: ((i // lat_tiles) * tiles_b + ctx_tiles + i % lat_tiles, 0))
        mod_spec = pl.BlockSpec((None, 6, D), lambda i: (i // lat_tiles, 0, 0))
        out_rows = cfg.batch * cfg.seq
    else:
        sel = _mod_sel(cfg)
        n_tiles = _exact_div(M, rt)
        in_row = pl.BlockSpec((rt, D), lambda i: (i, 0))
        mod_spec = pl.BlockSpec((None, 6, D), lambda i: (sel(i), 0, 0))
        out_rows = M
    row_spec = pl.BlockSpec((rt, D), lambda i: (i, 0))
    par_spec = pl.BlockSpec(ln_g.shape, lambda i: (0, 0))
    assert latent_only or not y_dense
    in_specs = [in_row, row_spec if y_dense else in_row, mod_spec]
    args = [x_all, y, mod_cur]
    if next_rows is not None:
        in_specs.append(mod_spec)
        args.append(mod_next)
    in_specs += [par_spec, par_spec]
    args += [ln_g, ln_b]
    out_shape = [jax.ShapeDtypeStruct((out_rows, D), F32)]
    out_specs = [row_spec]
    if next_rows is not None:
        out_shape.append(jax.ShapeDtypeStruct((M, D), u_dtype))
        out_specs.append(row_spec)
    if n_experts:
        in_specs.append(pl.BlockSpec(router.shape, lambda i: (0, 0)))
        args.append(router)
        out_shape.append(jax.ShapeDtypeStruct((M, V7X_LANES), F32))
        out_specs.append(pl.BlockSpec((rt, V7X_LANES), lambda i: (i, 0)))
    vmem = 2 * rt * D * (4 + 4 + 4 + 2) + 4 * 8 * D * 4 + 4 * rt * D * 4 + (2 * D * 128 * 4 * 4 if n_experts else 0)
    kern = functools.partial(_ln_kernel, alpha=alpha, eps=cfg.eps, gate_row=gate_row, l=l,
                             next_rows=next_rows, n_experts=n_experts)
    return pl.pallas_call(
        kern,
        out_shape=out_shape,
        grid=(n_tiles,),
        in_specs=in_specs,
        out_specs=out_specs,
        compiler_params=_params(vmem, 1),
        name="deepnorm_ln",
    )(*args)


def _rms_rope(x, gain, cos, sin, eps):
    xn = x * lax.rsqrt(jnp.mean(x * x, axis=-1, keepdims=True) + eps) * gain
    if cos is None:
        return xn
    d = x.shape[-1]
    q = d // 4
    lane = lax.broadcasted_iota(jnp.int32, xn.shape, 1)
    fwd = pltpu.roll(xn, d - q, axis=1)
    bwd = pltpu.roll(xn, q, axis=1)
    swapped = jnp.where(lane % (2 * q) < q, fwd, bwd)
    return xn * cos + swapped * sin


def _dot_nt(a, b):
    return lax.dot_general(a, b, (((1,), (1,)), ((), ())), preferred_element_type=F32)


def _attend_chunks(q, chunks):
    m = l = acc = None
    for k, v, bias in chunks:
        s = _dot_nt(q, k)
        if bias is not None:
            s = s + bias
        cm = jnp.max(s, axis=-1, keepdims=True)
        if m is None:
            m_new = cm
            p = jnp.exp(s - m_new)
            l = jnp.sum(p, axis=-1, keepdims=True)
            acc = jnp.dot(p.astype(BF16), v, preferred_element_type=F32)
        else:
            m_new = jnp.maximum(m, cm)
            a = jnp.exp(m - m_new)
            p = jnp.exp(s - m_new)
            l = a * l + jnp.sum(p, axis=-1, keepdims=True)
            acc = a * acc + jnp.dot(p.astype(BF16), v, preferred_element_type=F32)
        m = m_new
    return acc / l


def _attend_chunks_base2(q, kt_ref, vx_ref, bounds, hd):
    m = acc = None
    for s0, n in bounds:
        s = jnp.dot(q, kt_ref[:, s0:s0 + n], preferred_element_type=F32)
        cm = jnp.max(s, axis=-1, keepdims=True)
        if m is None:
            m_new = cm
            acc = jnp.dot(jnp.exp2(s - m_new).astype(BF16), vx_ref[s0:s0 + n, :], preferred_element_type=F32)
        else:
            m_new = jnp.maximum(m, cm)
            pv = jnp.dot(jnp.exp2(s - m_new).astype(BF16), vx_ref[s0:s0 + n, :], preferred_element_type=F32)
            acc = jnp.exp2(m - m_new) * acc + pv
        m = m_new
    return acc[:, :hd] / acc[:, hd:2 * hd]


def _na_kernel(q_ref, k_ref, v_ref, bias_ref, _, o_ref, *, ctx_len, grid_w, qrows, krows, n_rows, scale,
               hd, heads):
    t = pl.program_id(2)
    ks = jnp.clip(qrows * t - qrows, 0, n_rows - krows)
    start = pl.multiple_of(ctx_len + ks * grid_w, 16)
    nk = krows * grid_w
    for h in range(heads):
        cols = slice(h * hd, (h + 1) * hd)
        q = (q_ref[:, cols].astype(F32) * scale).astype(BF16)
        k_win = k_ref[pl.ds(start, nk), cols]
        v_win = v_ref[pl.ds(start, nk), cols]
        k_ctx = k_ref[0:ctx_len, cols]
        v_ctx = v_ref[0:ctx_len, cols]
        o = _attend_chunks(q, [(k_win, v_win, bias_ref[h]), (k_ctx, v_ctx, None)])
        o_ref[:, cols] = o.astype(o_ref.dtype)


def _na_bias(cfg, rpb):
    W = cfg.grid_w
    n_rows = _exact_div(cfg.seq, W)
    qr, kr = cfg.na_qrows, cfg.na_qrows + cfg.na_win_h
    n_t = _exact_div(n_rows, qr)
    wh, ww = cfg.na_win_h, cfg.na_win_w
    roffs, valids = [], []
    for t in (0, 1, n_t - 1):
        ks = int(np.clip(qr * t - qr, 0, n_rows - kr))
        r = qr * t + np.arange(qr)
        rs = np.clip(r - wh // 2, 0, n_rows - wh)
        kabs = ks + np.arange(kr)
        vrow = (kabs[None, :] >= rs[:, None]) & (kabs[None, :] < rs[:, None] + wh)
        roff = np.clip(kabs[None, :] - r[:, None] + (wh - 1), 0, 2 * wh - 2)
        roffs.append(roff)
        valids.append(vrow)
    roff = np.stack(roffs)
    vrow = np.stack(valids)
    cols = np.arange(W)
    cs = np.clip(cols - ww // 2, 0, W - ww)
    vcol = (cols[None, :] >= cs[:, None]) & (cols[None, :] < cs[:, None] + ww)
    coff = np.clip(cols[None, :] - cols[:, None], -(ww - 1), ww - 1) + (ww - 1)
    n_ro, n_co = 2 * wh - 1, 2 * ww - 1
    onehot = (np.arange(n_co)[:, None] == coff.reshape(1, W * W)).astype(np.float32)
    H = rpb.shape[0]
    colb = jnp.dot(rpb.astype(F32).reshape(H * n_ro, n_co), jnp.asarray(onehot),
                   precision=lax.Precision.HIGHEST).reshape(H, n_ro, W, W)
    colb = jnp.where(vcol[None, None], colb, cfg.neg_inf)
    masked = jnp.full((H, W, W), cfg.neg_inf, F32)
    variants = []
    for v in range(3):
        rows = []
        for a in range(qr):
            tiles = [colb[:, int(roff[v, a, b])] if vrow[v, a, b] else masked for b in range(kr)]
            rows.append(jnp.concatenate(tiles, axis=-1))
        variants.append(jnp.concatenate(rows, axis=1))
    return jnp.stack(variants, axis=1)


def _na_attention(cfg, p, bias, offs):
    M = p.shape[0]
    W, hd = cfg.grid_w, cfg.head_dim
    RB = cfg.ctx_len + cfg.seq
    n_rows = _exact_div(cfg.seq, W)
    qr, kr = cfg.na_qrows, cfg.na_qrows + cfg.na_win_h
    n_t = _exact_div(n_rows, qr)
    qb = qr * W
    q0 = _exact_div(cfg.ctx_len, qb)
    tiles_b = _exact_div(RB, qb)
    hps = cfg.na_hps
    hw = hps * hd
    cq, ck, cv = (_exact_div(offs[i], hw) for i in range(3))

    def q_map(b, h, t):
        return (b * tiles_b + q0 + t, cq + h)

    def var_map(b, h, t):
        return (h, jnp.where(t == 0, 0, jnp.where(t == n_t - 1, 2, 1)), 0, 0)

    kern = functools.partial(_na_kernel, ctx_len=cfg.ctx_len, grid_w=W, qrows=qr, krows=kr,
                             n_rows=n_rows, scale=float(hd) ** -0.5, hd=hd, heads=hps)
    vmem = hps * (4 * RB * hd * 2 + 2 * qb * kr * W * 4 + 6 * qb * (kr * W + cfg.ctx_len) * 4)
    return pl.pallas_call(
        kern,
        out_shape=jax.ShapeDtypeStruct((M, cfg.na_heads * hd), BF16),
        grid=(cfg.batch, _exact_div(cfg.na_heads, hps), n_t),
        in_specs=[pl.BlockSpec((qb, hw), q_map),
                  pl.BlockSpec((RB, hw), lambda b, h, t: (b, ck + h)),
                  pl.BlockSpec((RB, hw), lambda b, h, t: (b, cv + h)),
                  pl.BlockSpec((hps, None, qb, kr * W), var_map),
                  pl.BlockSpec(memory_space=pl.ANY)],
        out_specs=pl.BlockSpec((qb, hw), lambda b, h, t: (b * tiles_b + q0 + t, h)),
        input_output_aliases={4: 0},
        compiler_params=_params(vmem, 3),
        name="na_attention",
    )(p, p, p, bias, jnp.zeros((M, cfg.na_heads * hd), BF16))


def _gq_kernel(q_ref, k_ref, v_ref, qg_ref, kg_ref, cos_ref, sin_ref, _, o_ref, kt_ref, vx_ref, qn_ref, *,
               l, groups, hd, tq, kc, ctx_len, seq, eps, scale):
    qi = pl.program_id(2)

    @pl.when(qi == 0)
    def _():
        cos = cos_ref[...]
        sin = sin_ref[...]
        kn = _rms_rope(k_ref[...].astype(F32), kg_ref[l:l + 1, :], cos, sin, eps)
        kt_ref[...] = kn.T.astype(BF16)
        vx_ref[:, 0:hd] = v_ref[...]
        vx_ref[:, hd:2 * hd] = jnp.ones((v_ref.shape[0], hd), BF16)
        gain = qg_ref[l:l + 1, :] * (scale * LOG2_E)
        for g in range(groups):
            qg = _rms_rope(q_ref[:, g * hd:(g + 1) * hd].astype(F32), gain, cos, sin, eps)
            qn_ref[g] = qg.astype(BF16)

    row0 = pl.multiple_of(ctx_len + qi * tq, 16)
    q = jnp.concatenate([qn_ref[g, pl.ds(row0, tq), :] for g in range(groups)], axis=0)
    bounds = [(0, ctx_len)] + [(ctx_len + c * kc, kc) for c in range(seq // kc)]
    o = _attend_chunks_base2(q, kt_ref, vx_ref, bounds, hd)
    for g in range(groups):
        o_ref[:, g * hd:(g + 1) * hd] = o[g * tq:(g + 1) * tq, :].astype(o_ref.dtype)


def _rope_tables(cfg):
    hd = cfg.head_dim
    q = hd // 4
    t = np.arange(cfg.seq)
    inv = cfg.rope_theta ** (-np.arange(q, dtype=np.float64) / q)
    ang_r = (t // cfg.grid_w)[:, None] * inv[None, :]
    ang_c = (t % cfg.grid_w)[:, None] * inv[None, :]
    cos = np.concatenate([np.cos(ang_r), np.cos(ang_r), np.cos(ang_c), np.cos(ang_c)], axis=1)
    sin = np.concatenate([-np.sin(ang_r), np.sin(ang_r), -np.sin(ang_c), np.sin(ang_c)], axis=1)
    cos = np.concatenate([np.ones((cfg.ctx_len, hd)), cos], axis=0)
    sin = np.concatenate([np.zeros((cfg.ctx_len, hd)), sin], axis=0)
    return jnp.asarray(cos, F32), jnp.asarray(sin, F32)


def _gq_attention(cfg, p, q_gain, k_gain, cos, sin, l, offs):
    M = p.shape[0]
    hd = cfg.head_dim
    RB = cfg.ctx_len + cfg.seq
    G = _exact_div(cfg.gq_heads, cfg.gq_kv)
    tq = cfg.gq_tq
    nq = _exact_div(cfg.seq, tq)
    q0 = _exact_div(cfg.ctx_len, tq)
    tiles_b = _exact_div(RB, tq)
    cq = _exact_div(offs[3], G * hd)
    ck = _exact_div(offs[4], hd)
    cv = _exact_div(offs[5], hd)
    kc = cfg.gq_kc
    _exact_div(cfg.seq, kc)
    kern = functools.partial(_gq_kernel, l=l, groups=G, hd=hd, tq=tq, kc=kc, ctx_len=cfg.ctx_len,
                             seq=cfg.seq, eps=cfg.eps, scale=float(hd) ** -0.5)
    vmem = (4 * RB * hd * 2 + 2 * RB * G * hd * 2 + (3 + G) * RB * hd * 2 + 4 * RB * hd * 4
            + 6 * RB * hd * 4 + 2 * tq * G * hd * 2 + 6 * G * tq * kc * 4)
    return pl.pallas_call(
        kern,
        out_shape=jax.ShapeDtypeStruct((M, cfg.gq_heads * hd), BF16),
        grid=(cfg.batch, cfg.gq_kv, nq),
        in_specs=[pl.BlockSpec((RB, G * hd), lambda b, k, i: (b, cq + k)),
                  pl.BlockSpec((RB, hd), lambda b, k, i: (b, ck + k)),
                  pl.BlockSpec((RB, hd), lambda b, k, i: (b, cv + k)),
                  pl.BlockSpec(q_gain.shape, lambda b, k, i: (0, 0)),
                  pl.BlockSpec(k_gain.shape, lambda b, k, i: (0, 0)),
                  pl.BlockSpec((RB, hd), lambda b, k, i: (0, 0)),
                  pl.BlockSpec((RB, hd), lambda b, k, i: (0, 0)),
                  pl.BlockSpec(memory_space=pl.ANY)],
        out_specs=pl.BlockSpec((tq, G * hd), lambda b, k, i: (b * tiles_b + q0 + i, k)),
        scratch_shapes=[pltpu.VMEM((hd, RB), BF16), pltpu.VMEM((RB, 2 * hd), BF16),
                        pltpu.VMEM((G, RB, hd), BF16)],
        input_output_aliases={7: 0},
        compiler_params=_params(vmem, 3),
        name="gq_attention",
    )(p, p, p, q_gain, k_gain, cos, sin, jnp.zeros((M, cfg.gq_heads * hd), BF16))


def _ctx_attn_kernel(*refs, l, groups, hd, norm, eps, scale):
    if norm:
        q_ref, k_ref, v_ref, qg_ref, kg_ref, _, o_ref = refs
    else:
        q_ref, k_ref, v_ref, _, o_ref = refs
    k = k_ref[...]
    if norm:
        k = _rms_rope(k.astype(F32), kg_ref[l:l + 1, :], None, None, eps).astype(BF16)
    n = q_ref.shape[0]
    qs = []
    for g in range(groups):
        qg = q_ref[:, g * hd:(g + 1) * hd].astype(F32)
        if norm:
            qg = _rms_rope(qg, qg_ref[l:l + 1, :] * scale, None, None, eps)
        else:
            qg = qg * scale
        qs.append(qg.astype(BF16))
    q = jnp.concatenate(qs, axis=0)
    o = _attend_chunks(q, [(k, v_ref[...], None)])
    for g in range(groups):
        o_ref[:, g * hd:(g + 1) * hd] = o[g * n:(g + 1) * n, :].astype(o_ref.dtype)


def _ctx_attention(cfg, p, y_prev, n_kv, groups, col_q, col_k, col_v, l, gains=None):
    hd = cfg.head_dim
    C = cfg.ctx_len
    tiles_b = _exact_div(cfg.ctx_len + cfg.seq, C)
    cq = _exact_div(col_q, groups * hd)
    ck = _exact_div(col_k, hd)
    cv = _exact_div(col_v, hd)
    norm = gains is not None
    in_specs = [pl.BlockSpec((C, groups * hd), lambda b, k: (b * tiles_b, cq + k)),
                pl.BlockSpec((C, hd), lambda b, k: (b * tiles_b, ck + k)),
                pl.BlockSpec((C, hd), lambda b, k: (b * tiles_b, cv + k))]
    args = [p, p, p]
    if norm:
        in_specs += [pl.BlockSpec(gains[0].shape, lambda b, k: (0, 0)),
                     pl.BlockSpec(gains[1].shape, lambda b, k: (0, 0))]
        args += list(gains)
    in_specs.append(pl.BlockSpec(memory_space=pl.ANY))
    args.append(y_prev)
    kern = functools.partial(_ctx_attn_kernel, l=l, groups=groups, hd=hd, norm=norm, eps=cfg.eps,
                             scale=float(hd) ** -0.5)
    return pl.pallas_call(
        kern,
        out_shape=jax.ShapeDtypeStruct(y_prev.shape, y_prev.dtype),
        grid=(cfg.batch, n_kv),
        in_specs=in_specs,
        out_specs=pl.BlockSpec((C, groups * hd), lambda b, k: (b * tiles_b, k)),
        input_output_aliases={len(args) - 1: 0},
        compiler_params=_params(8 << 20, 2),
        name="ctx_attention",
    )(*args)


def _tile_scan(a, b, reverse):
    n = a.shape[0]
    rows = lax.broadcasted_iota(jnp.int32, a.shape, 0)
    d = 1
    while d < n:
        shift = n - d if reverse else d
        keep = rows < n - d if reverse else rows >= d
        a_prev = jnp.where(keep, pltpu.roll(a, shift, axis=0), 1.0)
        b_prev = jnp.where(keep, pltpu.roll(b, shift, axis=0), 0.0)
        b = a * b_prev + b
        a = a * a_prev
        d *= 2
    return a, b


def _lru_kernel(x_ref, gate_ref, cw_ref, cb_ref, wr_ref, br_ref, wi_ref, bi_ref, lam_ref, o_ref,
                xpad_ref, af_ref, bf_ref, ab_ref, bb_ref, *, l, ctx_len, seq, chunk, lru_c, conv_w):
    cwid = x_ref.shape[1]
    nblk = cwid // V7X_LANES
    pad = V7X_SUBLANES
    ctx_base = pad
    lat_base = ctx_base + ctx_len + pad
    zeros = jnp.zeros((pad, cwid), F32)
    xpad_ref[0:pad, :] = zeros
    xpad_ref[ctx_base + ctx_len:lat_base, :] = zeros
    xpad_ref[lat_base + seq:lat_base + seq + pad, :] = zeros
    xpad_ref[ctx_base:ctx_base + ctx_len, :] = x_ref[0:ctx_len, :].astype(F32)
    xpad_ref[lat_base:lat_base + seq, :] = x_ref[ctx_len:ctx_len + seq, :].astype(F32)

    taps = cw_ref[...]
    cbias = cb_ref[l:l + 1, :]
    lam = lam_ref[...]
    nl = -lam
    softplus = jnp.maximum(nl, 0.0) + jnp.log1p(jnp.exp(-jnp.abs(nl)))
    a_refs = (af_ref, ab_ref)
    b_refs = (bf_ref, bb_ref)

    def coeffs(src_base, dst_row, rows):
        xc = cbias
        for tap in range(conv_w):
            xc = xc + taps[tap:tap + 1, :] * xpad_ref[pl.ds(src_base + tap - conv_w // 2, rows), :]
        xcb = xc.astype(BF16)
        for d in range(2):
            rs, is_ = [], []
            for j in range(nblk):
                xj = xcb[:, j * V7X_LANES:(j + 1) * V7X_LANES]
                rs.append(jnp.dot(xj, wr_ref[d, j].astype(BF16), preferred_element_type=F32))
                is_.append(jnp.dot(xj, wi_ref[d, j].astype(BF16), preferred_element_type=F32))
            gr = jax.nn.sigmoid(jnp.concatenate(rs, axis=1) + br_ref[d:d + 1, :])
            gi = jax.nn.sigmoid(jnp.concatenate(is_, axis=1) + bi_ref[d:d + 1, :])
            log_a = -lru_c * gr * softplus[d:d + 1, :]
            a = jnp.exp(log_a)
            a_refs[d][pl.ds(dst_row, rows), :] = a
            one_minus_a2 = -jnp.tanh(log_a) * (1.0 + a * a)
            b_refs[d][pl.ds(dst_row, rows), :] = jnp.sqrt(one_minus_a2) * (gi * xc)

    coeffs(ctx_base, 0, ctx_len)
    for c in range(seq // chunk):
        coeffs(lat_base + c * chunk, ctx_len + c * chunk, chunk)

    nsub = V7X_SUBLANES

    def scan_seg(first, n, h):
        def body(i, hs):
            hf, hb = hs
            rf = pl.multiple_of(first + nsub * i, nsub)
            rb = pl.multiple_of(first + n - nsub - nsub * i, nsub)
            ca, cb = _tile_scan(af_ref[pl.ds(rf, nsub), :], bf_ref[pl.ds(rf, nsub), :], False)
            tile_f = ca * hf + cb
            bf_ref[pl.ds(rf, nsub), :] = tile_f
            ca, cb = _tile_scan(ab_ref[pl.ds(rb, nsub), :], bb_ref[pl.ds(rb, nsub), :], True)
            tile_b = ca * hb + cb
            bb_ref[pl.ds(rb, nsub), :] = tile_b
            return tile_f[nsub - 1:nsub, :], tile_b[0:1, :]
        return lax.fori_loop(0, n // nsub, body, h, unroll=2)

    h0 = jnp.zeros((1, cwid), F32)
    hs = scan_seg(0, ctx_len, (h0, h0))
    scan_seg(ctx_len, seq, hs)

    def out_body(c, carry):
        r0 = pl.multiple_of(c * ctx_len, ctx_len)
        h = bf_ref[pl.ds(r0, ctx_len), :] + bb_ref[pl.ds(r0, ctx_len), :]
        g = gate_ref[pl.ds(r0, ctx_len), :].astype(F32)
        o_ref[pl.ds(r0, ctx_len), :] = (h * jax.nn.gelu(g)).astype(o_ref.dtype)
        return carry

    lax.fori_loop(0, (ctx_len + seq) // ctx_len, out_body, 0)


def _lru_mixer(cfg, p, conv_w, conv_b, w_r, b_r, w_i, b_i, lam, l, offs):
    M = p.shape[0]
    RB = cfg.ctx_len + cfg.seq
    cw = cfg.lru_cw
    ng = _exact_div(cfg.lru_width, cw)
    blk = _exact_div(cfg.lru_width, cfg.lru_blocks)
    assert blk == V7X_LANES
    nb = _exact_div(cw, blk)
    cx = _exact_div(offs[6], cw)
    cg = _exact_div(offs[7], cw)
    _exact_div(cfg.seq, cfg.lru_chunk)
    _exact_div(RB, cfg.ctx_len)
    kern = functools.partial(_lru_kernel, l=l, ctx_len=cfg.ctx_len, seq=cfg.seq, chunk=cfg.lru_chunk,
                             lru_c=cfg.lru_c, conv_w=cfg.conv_w)
    vmem = 5 * (RB + 32) * cw * 4 + 3 * 2 * RB * cw * 2 + 12 * cfg.lru_chunk * cw * 4 + (4 << 20)
    return pl.pallas_call(
        kern,
        out_shape=jax.ShapeDtypeStruct((M, cfg.lru_width), BF16),
        grid=(cfg.batch, ng),
        in_specs=[pl.BlockSpec((RB, cw), lambda b, g: (b, cx + g)),
                  pl.BlockSpec((RB, cw), lambda b, g: (b, cg + g)),
                  pl.BlockSpec((None, cfg.conv_w, cw), lambda b, g: (l, 0, g)),
                  pl.BlockSpec((conv_b.shape[0], cw), lambda b, g: (0, g)),
                  pl.BlockSpec((None, 2, nb, blk, blk), lambda b, g: (l, 0, g, 0, 0)),
                  pl.BlockSpec((None, 2, cw), lambda b, g: (l, 0, g)),
                  pl.BlockSpec((None, 2, nb, blk, blk), lambda b, g: (l, 0, g, 0, 0)),
                  pl.BlockSpec((None, 2, cw), lambda b, g: (l, 0, g)),
                  pl.BlockSpec((None, 2, cw), lambda b, g: (l, 0, g))],
        out_specs=pl.BlockSpec((RB, cw), lambda b, g: (b, g)),
        scratch_shapes=[pltpu.VMEM((RB + 3 * V7X_SUBLANES, cw), F32)] + [pltpu.VMEM((RB, cw), F32)] * 4,
        compiler_params=_params(vmem, 2),
        name="rglru_mixer",
    )(p, p, conv_w, conv_b, w_r, b_r, w_i, b_i, lam)


def _merge_kernel(ya_ref, yb_ref, yc_ref, wa_ref, wb_ref, wc_ref, g0_ref, g1_ref, g2_ref, o_ref,
                  wab_ref, wbb_ref, wcb_ref):
    @pl.when(pl.program_id(1) == 0)
    def _():
        wab_ref[...] = wa_ref[...].astype(BF16)
        wbb_ref[...] = wb_ref[...].astype(BF16)
        wcb_ref[...] = wc_ref[...].astype(BF16)

    acc = jax.nn.sigmoid(g0_ref[...].astype(F32)) * jnp.dot(ya_ref[...], wab_ref[...], preferred_element_type=F32)
    acc += jax.nn.sigmoid(g1_ref[...].astype(F32)) * jnp.dot(yb_ref[...], wbb_ref[...], preferred_element_type=F32)
    acc += jax.nn.sigmoid(g2_ref[...].astype(F32)) * jnp.dot(yc_ref[...], wcb_ref[...], preferred_element_type=F32)
    o_ref[...] = acc.astype(o_ref.dtype)


def _merge(cfg, ya, yb, yc, p, w_na, w_gq, w_lru, l, offs):
    M = ya.shape[0]
    D = cfg.d_model
    bm, bn = cfg.merge_bm, cfg.merge_bn
    nj = _exact_div(D, bn)
    gcol = _exact_div(offs[8], bn)
    ks = (ya.shape[1], yb.shape[1], yc.shape[1])
    ksum = sum(ks)
    vmem = 2 * bm * ksum * 2 + 2 * ksum * bn * 4 + ksum * bn * 2 + 3 * 2 * bm * bn * 2 + 2 * bm * bn * 2 + 4 * bm * bn * 4
    x_specs = [pl.BlockSpec((bm, k), lambda j, i: (i, 0)) for k in ks]
    w_specs = [pl.BlockSpec((None, k, bn), lambda j, i: (l, 0, j)) for k in ks]
    g_specs = [pl.BlockSpec((bm, bn), functools.partial(lambda j, i, s: (i, gcol + s * nj + j), s=s))
               for s in range(3)]
    return pl.pallas_call(
        _merge_kernel,
        out_shape=jax.ShapeDtypeStruct((M, D), BF16),
        grid=(nj, _exact_div(M, bm)),
        in_specs=x_specs + w_specs + g_specs,
        out_specs=pl.BlockSpec((bm, bn), lambda j, i: (i, j)),
        scratch_shapes=[pltpu.VMEM((k, bn), BF16) for k in ks],
        compiler_params=_params(vmem, 2),
        name="branch_merge",
    )(ya, yb, yc, w_na, w_gq, w_lru, p, p, p)


def _forward(cfg, x, c, ctx, c_ctx, w_ada, b_ada, w_in, na_rpb, gq_q_gain, gq_k_gain, lru_conv_w,
             lru_conv_b, lru_w_r, lru_b_r, lru_w_i, lru_b_i, lru_lambda, w_br_na, w_br_gq, w_br_lru,
             w_o, ln1_g, ln1_b, ln2_g, ln2_b, ffn_w1, ffn_w3, ffn_w2, moe_router, moe_w1, moe_w3,
             moe_w2):
    B, D = cfg.batch, cfg.d_model
    offs, widths = _splits(cfg)
    in_cols = offs[-1] + widths[-1]
    bm, bn = cfg.mm_bm, cfg.mm_bn

    cond = jnp.concatenate([c, c_ctx[None, :], jnp.zeros((V7X_SUBLANES - B - 1, D), F32)], axis=0)
    b_ada3 = b_ada.reshape(cfg.depth, 1, 6 * D)
    mods = [_ada_table(cond, w_ada, b_ada3, l).reshape(V7X_SUBLANES, 6, D) for l in range(cfg.depth)]
    cos, sin = _rope_tables(cfg)

    x_all, u = _modulate(cfg, x, ctx, mods[0], 0, 1)
    for l in range(cfg.depth):
        need_ctx = l < cfg.depth - 1
        mod = mods[l]
        p = _matmul(u, [w_in], (l,), bm=bm, bn=bn, out_dtype=BF16, name="in_proj")
        assert p.shape[1] == in_cols
        bias = _na_bias(cfg, na_rpb[l])
        ya = _na_attention(cfg, p, bias, offs)
        yb = _gq_attention(cfg, p, gq_q_gain, gq_k_gain, cos, sin, l, offs)
        if need_ctx:
            ya = _ctx_attention(cfg, p, ya, cfg.na_heads, 1, offs[0], offs[1], offs[2], l)
            yb = _ctx_attention(cfg, p, yb, cfg.gq_kv, cfg.gq_heads // cfg.gq_kv, offs[3], offs[4],
                                offs[5], l, gains=(gq_q_gain, gq_k_gain))
        yc = _lru_mixer(cfg, p, lru_conv_w, lru_conv_b, lru_w_r, lru_b_r, lru_w_i, lru_b_i,
                        lru_lambda, l, offs)
        merged = _merge(cfg, ya, yb, yc, p, w_br_na, w_br_gq, w_br_lru, l, offs)
        y = _matmul(merged, [w_o], (l,), bm=bm, bn=bn, out_dtype=BF16, name="out_proj")

        dense = l % 2 == 0
        router = None
        if not dense:
            router = jnp.pad(moe_router[l // 2], ((0, 0), (0, V7X_LANES - cfg.n_experts)))
        res = _layer_norm(cfg, x_all, y, mod, 2, ln1_g, ln1_b, l, mod_next=mod, next_rows=(3, 4),
                          router=router, u_dtype=BF16 if dense else F32)
        x_all, u2 = res[0], res[1]
        if dense:
            h = _matmul(u2, [ffn_w1, ffn_w3], (l // 2,), bm=bm, bn=cfg.glu_bn, out_dtype=BF16,
                        glu=True, name="ffn_up")
            w2 = ffn_w2[l // 2].astype(BF16)[None]
            f = _matmul(h, [w2], (0,), bm=cfg.down_bm, bn=bn, out_dtype=BF16, name="ffn_down")
        else:
            top = res[2]
            if not need_ctx:
                top = top.reshape(B, cfg.ctx_len + cfg.seq, V7X_LANES)[:, cfg.ctx_len:, :]
                top = top.reshape(B * cfg.seq, V7X_LANES)
            rt = _route(cfg, top, latent_only=not need_ctx)
            xs = _gather_rows(u2, rt["src"], cfg.gather_rows, BF16)
            h = _grouped_matmul(xs, [moe_w1, moe_w3], l // 2, rt["te"], rt["n_used"], bm=cfg.moe_bm,
                                bn=cfg.moe_up_bn, out_dtype=BF16, glu=True, name="moe_up")
            ys = _grouped_matmul(h, [moe_w2], l // 2, rt["te"], rt["n_used"], bm=cfg.moe_bm,
                                 bn=cfg.moe_down_bn, out_dtype=F32, name="moe_down")
            f = _combine(ys, rt["pos1"], rt["pos2"], top, BF16)
        f_dense = (not dense) and (not need_ctx)
        if l + 1 < cfg.depth:
            assert not f_dense
            x_all, u = _layer_norm(cfg, x_all, f, mod, 5, ln2_g, ln2_b, l, mod_next=mods[l + 1],
                                   next_rows=(0, 1))
        else:
            (x_lat,) = _layer_norm(cfg, x_all, f, mod, 5, ln2_g, ln2_b, l, latent_only=True,
                                   y_dense=f_dense)
    return x_lat.reshape(B, cfg.seq, D)


def kernel(x, c, ctx, c_ctx, w_ada, b_ada, w_in, na_rpb, gq_q_gain, gq_k_gain, lru_conv_w, lru_conv_b, lru_w_r, lru_b_r, lru_w_i, lru_b_i, lru_lambda, w_br_na, w_br_gq, w_br_lru, w_o, ln1_g, ln1_b, ln2_g, ln2_b, ffn_w1, ffn_w3, ffn_w2, moe_router, moe_w1, moe_w3, moe_w2):
    return _forward(FULL_CFG, x, c, ctx, c_ctx, w_ada, b_ada, w_in, na_rpb, gq_q_gain, gq_k_gain,
                    lru_conv_w, lru_conv_b, lru_w_r, lru_b_r, lru_w_i, lru_b_i, lru_lambda, w_br_na,
                    w_br_gq, w_br_lru, w_o, ln1_g, ln1_b, ln2_g, ln2_b, ffn_w1, ffn_w3, ffn_w2,
                    moe_router, moe_w1, moe_w3, moe_w2)
```

```python
import collections
import functools

import numpy as np
import jax
import jax.numpy as jnp
from jax import lax
from jax.experimental import pallas as pl
from jax.experimental.pallas import tpu as pltpu

F32 = jnp.float32
BF16 = jnp.bfloat16
LOG2_E = 1.4426950408889634

V7X_LANES = 128
V7X_SUBLANES = 8
V7X_VMEM_BYTES = 64 * 1024 * 1024
VMEM_CAP_BYTES = V7X_VMEM_BYTES - 6 * 1024 * 1024

Cfg = collections.namedtuple(
    "Cfg",
    "d_model batch seq depth ctx_len grid_w head_dim na_heads na_win_h na_win_w gq_heads gq_kv "
    "rope_theta lru_width lru_blocks conv_w lru_c d_ff n_experts d_ff_expert eps neg_inf "
    "row_tile mm_bm mm_bn glu_bn merge_bm merge_bn down_bm moe_bm gather_rows moe_up_bn moe_down_bn na_qrows na_hps gq_tq gq_kc lru_cw "
    "lru_chunk")

FULL_CFG = Cfg(
    d_model=4096, batch=2, seq=4096, depth=2, ctx_len=256, grid_w=64, head_dim=128,
    na_heads=12, na_win_h=8, na_win_w=16, gq_heads=16, gq_kv=4, rope_theta=10000.0,
    lru_width=1536, lru_blocks=12, conv_w=4, lru_c=8.0, d_ff=11008, n_experts=8,
    d_ff_expert=3072, eps=1e-6, neg_inf=-1e30,
    row_tile=256, mm_bm=1088, mm_bn=512, glu_bn=256, merge_bm=512, merge_bn=512, down_bm=512,
    moe_bm=512, gather_rows=256, moe_up_bn=512, moe_down_bn=1024, na_qrows=4, na_hps=6, gq_tq=256, gq_kc=512, lru_cw=256, lru_chunk=512)


def _splits(cfg):
    na_w = cfg.na_heads * cfg.head_dim
    gq_q = cfg.gq_heads * cfg.head_dim
    gq_kv = cfg.gq_kv * cfg.head_dim
    widths = (na_w, na_w, na_w, gq_q, gq_kv, gq_kv, cfg.lru_width, cfg.lru_width, 3 * cfg.d_model)
    offs = tuple(int(v) for v in np.cumsum((0,) + widths[:-1]))
    return offs, widths


def _exact_div(a, b):
    assert a % b == 0, (a, b)
    return a // b


def _params(vmem_bytes, n_axes):
    limit = int(min(VMEM_CAP_BYTES, max(vmem_bytes * 5 // 4 + (4 << 20), 16 << 20)))
    return pltpu.CompilerParams(dimension_semantics=("arbitrary",) * n_axes, vmem_limit_bytes=limit)


def _dot_tile(x, wb_refs, glu):
    acc = jnp.dot(x, wb_refs[0][...], preferred_element_type=F32)
    if glu:
        acc = jax.nn.silu(acc) * jnp.dot(x, wb_refs[1][...], preferred_element_type=F32)
    return acc


def _mm_kernel(*refs, n_w, cast_w, glu):
    x_ref = refs[0]
    w_refs = refs[1:1 + n_w]
    o_ref = refs[1 + n_w]
    wb_refs = refs[2 + n_w:] if cast_w else w_refs

    if cast_w:
        @pl.when(pl.program_id(1) == 0)
        def _():
            for w_ref, wb_ref in zip(w_refs, wb_refs):
                wb_ref[...] = w_ref[...].astype(BF16)

    o_ref[...] = _dot_tile(x_ref[...], wb_refs, glu).astype(o_ref.dtype)


def _gmm_kernel(te_ref, nu_ref, *refs, n_w, glu):
    x_ref = refs[0]
    w_refs = refs[1:1 + n_w]
    o_ref = refs[1 + n_w]
    wb_refs = refs[2 + n_w:]
    i = pl.program_id(1)
    prev = te_ref[jnp.maximum(i - 1, 0)]

    @pl.when(jnp.logical_or(i == 0, te_ref[i] != prev))
    def _():
        for w_ref, wb_ref in zip(w_refs, wb_refs):
            wb_ref[...] = w_ref[...].astype(BF16)

    @pl.when(i < nu_ref[0])
    def _():
        o_ref[...] = _dot_tile(x_ref[...], wb_refs, glu).astype(o_ref.dtype)

    @pl.when(i >= nu_ref[0])
    def _():
        o_ref[...] = jnp.zeros(o_ref.shape, o_ref.dtype)


def _mm_vmem_bytes(n_w, bm, bn, K, wbytes, cast_w, out_dtype):
    return (2 * bm * K * 2 + n_w * (2 * K * bn * wbytes + (K * bn * 2 if cast_w else 0))
            + 2 * bm * bn * jnp.dtype(out_dtype).itemsize + 3 * bm * bn * 4)


def _matmul(x, ws, w_lead, *, bm, bn, out_dtype, glu=False, name="mm"):
    M, K = x.shape
    N = ws[0].shape[-1]
    assert ws[0].shape[-2] == K
    cast_w = ws[0].dtype != BF16
    lead = tuple(w_lead)
    in_specs = [pl.BlockSpec((bm, K), lambda j, i: (i, 0))]
    in_specs += [pl.BlockSpec((None,) * len(lead) + (K, bn), lambda j, i: lead + (0, j)) for _ in ws]
    scratch = [pltpu.VMEM((K, bn), BF16) for _ in ws] if cast_w else []
    vmem = _mm_vmem_bytes(len(ws), bm, bn, K, ws[0].dtype.itemsize, cast_w, out_dtype)
    return pl.pallas_call(
        functools.partial(_mm_kernel, n_w=len(ws), cast_w=cast_w, glu=glu),
        out_shape=jax.ShapeDtypeStruct((M, N), out_dtype),
        grid=(_exact_div(N, bn), _exact_div(M, bm)),
        in_specs=in_specs,
        out_specs=pl.BlockSpec((bm, bn), lambda j, i: (i, j)),
        scratch_shapes=scratch,
        compiler_params=_params(vmem, 2),
        name=name,
    )(x, *ws)


def _grouped_matmul(x, ws, l, tile_expert, n_used, *, bm, bn, out_dtype, glu=False, name="gmm"):
    R, K = x.shape
    N = ws[0].shape[-1]
    assert ws[0].shape[-2] == K
    in_specs = [pl.BlockSpec((bm, K), lambda j, i, te, nu: (i, 0))]
    in_specs += [pl.BlockSpec((None, None, K, bn), lambda j, i, te, nu: (l, te[i], 0, j)) for _ in ws]
    vmem = _mm_vmem_bytes(len(ws), bm, bn, K, 4, True, out_dtype)
    grid_spec = pltpu.PrefetchScalarGridSpec(
        num_scalar_prefetch=2,
        grid=(_exact_div(N, bn), _exact_div(R, bm)),
        in_specs=in_specs,
        out_specs=pl.BlockSpec((bm, bn), lambda j, i, te, nu: (i, j)),
        scratch_shapes=[pltpu.VMEM((K, bn), BF16) for _ in ws])
    return pl.pallas_call(
        functools.partial(_gmm_kernel, n_w=len(ws), glu=glu),
        out_shape=jax.ShapeDtypeStruct((R, N), out_dtype),
        grid_spec=grid_spec,
        compiler_params=_params(vmem, 2),
        name=name,
    )(tile_expert, n_used, x, *ws)


def _row_copy(src_hbm, row, dst_vmem, r, sem):
    return pltpu.make_async_copy(src_hbm.at[pl.ds(row, 1), :], dst_vmem.at[pl.ds(r, 1), :], sem)


def _gather_kernel(src_ref, nxt_ref, x_hbm, o_ref, buf_ref, sems):
    t = pl.program_id(0)
    rows = o_ref.shape[0]
    slot = t % 2

    def issue_tile(idx_ref, s):
        def body(r, c):
            _row_copy(x_hbm, idx_ref[0, r], buf_ref.at[s], r, sems.at[s]).start()
            return c
        lax.fori_loop(0, rows, body, 0, unroll=8)

    @pl.when(t == 0)
    def _():
        issue_tile(src_ref, 0)

    @pl.when(t + 1 < pl.num_programs(0))
    def _():
        issue_tile(nxt_ref, 1 - slot)

    def wait(r, c):
        _row_copy(x_hbm, 0, buf_ref.at[slot], r, sems.at[slot]).wait()
        return c

    lax.fori_loop(0, rows, wait, 0, unroll=8)
    o_ref[...] = buf_ref[slot].astype(o_ref.dtype)


def _gather_rows(x, src, rows, out_dtype):
    n_tiles = _exact_div(src.shape[0], rows)
    src3 = src.reshape(n_tiles, 1, rows)
    D = x.shape[1]
    vmem = 2 * rows * D * 4 + 2 * rows * D * jnp.dtype(out_dtype).itemsize + 2 * rows * D * 4
    return pl.pallas_call(
        _gather_kernel,
        out_shape=jax.ShapeDtypeStruct((n_tiles * rows, D), out_dtype),
        grid=(n_tiles,),
        in_specs=[pl.BlockSpec((None, 1, rows), lambda t: (t, 0, 0), memory_space=pltpu.SMEM),
                  pl.BlockSpec((None, 1, rows), lambda t: (jnp.minimum(t + 1, n_tiles - 1), 0, 0),
                               memory_space=pltpu.SMEM),
                  pl.BlockSpec(memory_space=pl.ANY)],
        out_specs=pl.BlockSpec((rows, D), lambda t: (t, 0)),
        scratch_shapes=[pltpu.VMEM((2, rows, D), x.dtype), pltpu.SemaphoreType.DMA((2,))],
        compiler_params=_params(vmem, 1),
        name="moe_gather",
    )(src3, src3, x)


def _combine_kernel(p1_ref, p2_ref, top_ref, y_hbm, o_ref, a_ref, b_ref, sems):
    bm = o_ref.shape[0]

    def issue(r, c):
        _row_copy(y_hbm, p1_ref[0, r], a_ref, r, sems.at[0]).start(priority=0)
        _row_copy(y_hbm, p2_ref[0, r], b_ref, r, sems.at[1]).start(priority=1)
        return c

    def wait(r, c):
        _row_copy(y_hbm, 0, a_ref, r, sems.at[0]).wait()
        _row_copy(y_hbm, 0, b_ref, r, sems.at[1]).wait()
        return c

    lax.fori_loop(0, bm, issue, 0, unroll=8)
    lax.fori_loop(0, bm, wait, 0, unroll=8)
    top = top_ref[...]
    o_ref[...] = (top[:, 2:3] * a_ref[...] + top[:, 3:4] * b_ref[...]).astype(o_ref.dtype)


def _combine(ys, pos1, pos2, top, out_dtype):
    n_tiles, _, bm = pos1.shape
    D = ys.shape[1]
    M = n_tiles * bm
    vmem = 2 * bm * D * 4 + 2 * bm * D * 4 + 2 * bm * D * 4
    idx_spec = pl.BlockSpec((None, 1, bm), lambda t: (t, 0, 0), memory_space=pltpu.SMEM)
    return pl.pallas_call(
        _combine_kernel,
        out_shape=jax.ShapeDtypeStruct((M, D), out_dtype),
        grid=(n_tiles,),
        in_specs=[idx_spec, idx_spec,
                  pl.BlockSpec((bm, V7X_LANES), lambda t: (t, 0)),
                  pl.BlockSpec(memory_space=pl.ANY)],
        out_specs=pl.BlockSpec((bm, D), lambda t: (t, 0)),
        scratch_shapes=[pltpu.VMEM((bm, D), ys.dtype), pltpu.VMEM((bm, D), ys.dtype),
                        pltpu.SemaphoreType.DMA((2,))],
        compiler_params=_params(vmem, 1),
        name="moe_combine",
    )(pos1, pos2, top, ys)


def _route(cfg, top, latent_only):
    T = top.shape[0]
    E, bm = cfg.n_experts, cfg.moe_bm
    P = 2 * T
    n_tiles = _exact_div(P, bm) + E
    experts = jnp.arange(E, dtype=jnp.int32)
    ef = jnp.clip(top[:, :2].astype(jnp.int32), 0, E - 1).reshape(P)
    onehot = (ef[:, None] == experts[None, :]).astype(jnp.int32)
    csum = jnp.cumsum(onehot, axis=0)
    n_e = csum[-1]
    padded = ((n_e + bm - 1) // bm) * bm
    gstart = jnp.cumsum(padded) - padded
    ustart = jnp.cumsum(n_e) - n_e
    dest = jnp.sum((csum - 1 + gstart[None, :]) * onehot, axis=1).reshape(T, 2)
    n_used = (jnp.sum(padded) // bm).astype(jnp.int32).reshape(1)
    tstart = jnp.arange(n_tiles, dtype=jnp.int32) * bm
    gend = gstart + padded
    te = jnp.minimum(jnp.sum((tstart[:, None] >= gend[None, :]).astype(jnp.int32), axis=1), E - 1)
    tile_oh = (te[:, None] == experts[None, :]).astype(jnp.int32)
    tile_n = jnp.sum(tile_oh * n_e[None, :], axis=1)
    k = ((tstart - jnp.sum(tile_oh * gstart[None, :], axis=1))[:, None]
         + jnp.arange(bm, dtype=jnp.int32)[None, :])
    s = jnp.clip(jnp.sum(tile_oh * ustart[None, :], axis=1)[:, None] + k, 0, P - 1)
    order = jnp.argsort(ef, stable=True).astype(jnp.int32)
    tok = jnp.where(k < tile_n[:, None], order[s] // 2, 0)
    if latent_only:
        tok = (tok // cfg.seq) * (cfg.ctx_len + cfg.seq) + cfg.ctx_len + tok % cfg.seq
    tiles_t = _exact_div(T, bm)
    return dict(te=te, n_used=n_used, src=tok.reshape(n_tiles * bm),
                pos1=dest[:, 0].reshape(tiles_t, 1, bm), pos2=dest[:, 1].reshape(tiles_t, 1, bm))


def _ada_kernel(c_ref, w_ref, b_ref, o_ref):
    c = c_ref[...]
    s = (c * jax.nn.sigmoid(c)).astype(BF16)
    acc = jnp.dot(s, w_ref[...].astype(BF16), preferred_element_type=F32)
    o_ref[...] = acc + b_ref[...]


def _ada_table(cond, w_ada, b_ada3, l, bn=1024):
    R, D = cond.shape
    N = w_ada.shape[-1]
    vmem = 2 * D * bn * 4 + D * bn * 2 + 4 * R * bn * 4 + 2 * R * D * 4
    return pl.pallas_call(
        _ada_kernel,
        out_shape=jax.ShapeDtypeStruct((R, N), F32),
        grid=(_exact_div(N, bn),),
        in_specs=[pl.BlockSpec((R, D), lambda j: (0, 0)),
                  pl.BlockSpec((None, D, bn), lambda j: (l, 0, j)),
                  pl.BlockSpec((None, 1, bn), lambda j: (l, 0, j))],
        out_specs=pl.BlockSpec((R, bn), lambda j: (0, j)),
        compiler_params=_params(vmem, 1),
        name="ada_table",
    )(cond, w_ada, b_ada3)


def _mod_sel(cfg):
    tiles_per_batch = _exact_div(cfg.ctx_len + cfg.seq, cfg.row_tile)
    ctx_tiles = _exact_div(cfg.ctx_len, cfg.row_tile)

    def sel(i):
        return jnp.where(i % tiles_per_batch < ctx_tiles, cfg.batch, i // tiles_per_batch)
    return sel


def _modulate_kernel(x_ref, c_ref, m_ref, xo_ref, u_ref, *, tiles_b, ctx_tiles, shift_row, scale_row):
    pos = pl.program_id(0) % tiles_b
    m = m_ref[...]

    def emit(v):
        xo_ref[...] = v
        u = v * (1.0 + m[scale_row:scale_row + 1, :]) + m[shift_row:shift_row + 1, :]
        u_ref[...] = u.astype(u_ref.dtype)

    @pl.when(pos < ctx_tiles)
    def _():
        emit(c_ref[...])

    @pl.when(pos >= ctx_tiles)
    def _():
        emit(x_ref[...])


def _modulate(cfg, x, ctx, modtab, shift_row, scale_row):
    B, D = cfg.batch, cfg.d_model
    rt = cfg.row_tile
    M = B * (cfg.ctx_len + cfg.seq)
    sel = _mod_sel(cfg)
    tiles_b = _exact_div(cfg.ctx_len + cfg.seq, rt)
    ctx_tiles = _exact_div(cfg.ctx_len, rt)
    lat_tiles = _exact_div(cfg.seq, rt)
    vmem = 4 * rt * D * 4 + 2 * rt * D * 4 + 2 * rt * D * 2 + 2 * 8 * D * 4 + 2 * rt * D * 4
    kern = functools.partial(_modulate_kernel, tiles_b=tiles_b, ctx_tiles=ctx_tiles,
                             shift_row=shift_row, scale_row=scale_row)
    row_spec = pl.BlockSpec((rt, D), lambda i: (i, 0))
    return pl.pallas_call(
        kern,
        out_shape=[jax.ShapeDtypeStruct((M, D), F32), jax.ShapeDtypeStruct((M, D), BF16)],
        grid=(_exact_div(M, rt),),
        in_specs=[pl.BlockSpec((rt, D), lambda i: ((i // tiles_b) * lat_tiles
                                                   + jnp.maximum(i % tiles_b - ctx_tiles, 0), 0)),
                  pl.BlockSpec((rt, D), lambda i: ((i // tiles_b) * ctx_tiles
                                                   + jnp.minimum(i % tiles_b, ctx_tiles - 1), 0)),
                  pl.BlockSpec((None, 6, D), lambda i: (sel(i), 0, 0))],
        out_specs=[row_spec, row_spec],
        compiler_params=_params(vmem, 1),
        name="modulate",
    )(x.reshape(B * cfg.seq, D), ctx.reshape(B * cfg.ctx_len, D), modtab)


def _ln_kernel(*refs, alpha, eps, gate_row, l, next_rows, n_experts):
    it = iter(refs)
    x_ref, y_ref, mc_ref = next(it), next(it), next(it)
    mn_ref = next(it) if next_rows is not None else None
    g_ref, b_ref = next(it), next(it)
    r_ref = next(it) if n_experts else None
    xo_ref = next(it)
    u_ref = next(it) if next_rows is not None else None
    go_ref = next(it) if n_experts else None

    gate = mc_ref[gate_row:gate_row + 1, :]
    z = alpha * x_ref[...] + gate * y_ref[...].astype(F32)
    mu = jnp.mean(z, axis=-1, keepdims=True)
    zc = z - mu
    var = jnp.mean(zc * zc, axis=-1, keepdims=True)
    xn = zc * lax.rsqrt(var + eps) * g_ref[l:l + 1, :] + b_ref[l:l + 1, :]
    xo_ref[...] = xn
    if next_rows is not None:
        shift_row, scale_row = next_rows
        mn = mn_ref[...]
        u = xn * (1.0 + mn[scale_row:scale_row + 1, :]) + mn[shift_row:shift_row + 1, :]
        u_ref[...] = u.astype(u_ref.dtype)
        if n_experts:
            logits = jnp.dot(u, r_ref[...], preferred_element_type=F32,
                             precision=lax.Precision.HIGHEST)
            lane = lax.broadcasted_iota(jnp.int32, logits.shape, 1)
            neg = jnp.float32(-jnp.inf)
            lg = jnp.where(lane < n_experts, logits, neg)
            m1 = jnp.max(lg, axis=-1, keepdims=True)
            i1 = jnp.min(jnp.where(lg == m1, lane, V7X_LANES), axis=-1, keepdims=True)
            lg2 = jnp.where(lane == i1, neg, lg)
            m2 = jnp.max(lg2, axis=-1, keepdims=True)
            i2 = jnp.min(jnp.where(lg2 == m2, lane, V7X_LANES), axis=-1, keepdims=True)
            e2 = jnp.exp(m2 - m1)
            w1 = 1.0 / (1.0 + e2)
            w2 = e2 / (1.0 + e2)
            packed = jnp.where(lane == 0, i1.astype(F32), jnp.where(lane == 1, i2.astype(F32), 0.0))
            go_ref[...] = packed + jnp.where(lane == 2, w1, 0.0) + jnp.where(lane == 3, w2, 0.0)


def _layer_norm(cfg, x_all, y, mod_cur, gate_row, ln_g, ln_b, l, mod_next=None, next_rows=None,
                router=None, u_dtype=BF16, latent_only=False, y_dense=False):
    M, D = x_all.shape
    rt = cfg.row_tile
    alpha = float((2 * cfg.depth) ** 0.25)
    n_experts = cfg.n_experts if router is not None else 0
    if latent_only:
        assert next_rows is None and router is None
        lat_tiles = _exact_div(cfg.seq, rt)
        tiles_b = _exact_div(cfg.ctx_len + cfg.seq, rt)
        ctx_tiles = _exact_div(cfg.ctx_len, rt)
        n_tiles = cfg.batch * lat_tiles
        in_row = pl.BlockSpec((rt, D), lambda i: ((i // lat_tiles) * tiles_b + ctx_tiles + i % lat_tiles, 0))
        mod_spec = pl.BlockSpec((None, 6, D), lambda i: (i // lat_tiles, 0, 0))
        out_rows = cfg.batch * cfg.seq
    else:
        sel = _mod_sel(cfg)
        n_tiles = _exact_div(M, rt)
        in_row = pl.BlockSpec((rt, D), lambda i: (i, 0))
        mod_spec = pl.BlockSpec((None, 6, D), lambda i: (sel(i), 0, 0))
        out_rows = M
    row_spec = pl.BlockSpec((rt, D), lambda i: (i, 0))
    par_spec = pl.BlockSpec(ln_g.shape, lambda i: (0, 0))
    assert latent_only or not y_dense
    in_specs = [in_row, row_spec if y_dense else in_row, mod_spec]
    args = [x_all, y, mod_cur]
    if next_rows is not None:
        in_specs.append(mod_spec)
        args.append(mod_next)
    in_specs += [par_spec, par_spec]
    args += [ln_g, ln_b]
    out_shape = [jax.ShapeDtypeStruct((out_rows, D), F32)]
    out_specs = [row_spec]
    if next_rows is not None:
        out_shape.append(jax.ShapeDtypeStruct((M, D), u_dtype))
        out_specs.append(row_spec)
    if n_experts:
        in_specs.append(pl.BlockSpec(router.shape, lambda i: (0, 0)))
        args.append(router)
        out_shape.append(jax.ShapeDtypeStruct((M, V7X_LANES), F32))
        out_specs.append(pl.BlockSpec((rt, V7X_LANES), lambda i: (i, 0)))
    vmem = 2 * rt * D * (4 + 4 + 4 + 2) + 4 * 8 * D * 4 + 4 * rt * D * 4 + (2 * D * 128 * 4 * 4 if n_experts else 0)
    kern = functools.partial(_ln_kernel, alpha=alpha, eps=cfg.eps, gate_row=gate_row, l=l,
                             next_rows=next_rows, n_experts=n_experts)
    return pl.pallas_call(
        kern,
        out_shape=out_shape,
        grid=(n_tiles,),
        in_specs=in_specs,
        out_specs=out_specs,
        compiler_params=_params(vmem, 1),
        name="deepnorm_ln",
    )(*args)


def _rms_rope(x, gain, cos, sin, eps):
    xn = x * lax.rsqrt(jnp.mean(x * x, axis=-1, keepdims=True) + eps) * gain
    if cos is None:
        return xn
    d = x.shape[-1]
    q = d // 4
    lane = lax.broadcasted_iota(jnp.int32, xn.shape, 1)
    fwd = pltpu.roll(xn, d - q, axis=1)
    bwd = pltpu.roll(xn, q, axis=1)
    swapped = jnp.where(lane % (2 * q) < q, fwd, bwd)
    return xn * cos + swapped * sin


def _dot_nt(a, b):
    return lax.dot_general(a, b, (((1,), (1,)), ((), ())), preferred_element_type=F32)


def _attend_chunks(q, chunks):
    m = l = acc = None
    for k, v, bias in chunks:
        s = _dot_nt(q, k)
        if bias is not None:
            s = s + bias
        cm = jnp.max(s, axis=-1, keepdims=True)
        if m is None:
            m_new = cm
            p = jnp.exp(s - m_new)
            l = jnp.sum(p, axis=-1, keepdims=True)
            acc = jnp.dot(p.astype(BF16), v, preferred_element_type=F32)
        else:
            m_new = jnp.maximum(m, cm)
            a = jnp.exp(m - m_new)
            p = jnp.exp(s - m_new)
            l = a * l + jnp.sum(p, axis=-1, keepdims=True)
            acc = a * acc + jnp.dot(p.astype(BF16), v, preferred_element_type=F32)
        m = m_new
    return acc / l


def _attend_chunks_base2(q, kt_ref, vx_ref, bounds, hd):
    m = acc = None
    for s0, n in bounds:
        s = jnp.dot(q, kt_ref[:, s0:s0 + n], preferred_element_type=F32)
        cm = jnp.max(s, axis=-1, keepdims=True)
        if m is None:
            m_new = cm
            acc = jnp.dot(jnp.exp2(s - m_new).astype(BF16), vx_ref[s0:s0 + n, :], preferred_element_type=F32)
        else:
            m_new = jnp.maximum(m, cm)
            pv = jnp.dot(jnp.exp2(s - m_new).astype(BF16), vx_ref[s0:s0 + n, :], preferred_element_type=F32)
            acc = jnp.exp2(m - m_new) * acc + pv
        m = m_new
    return acc[:, :hd] / acc[:, hd:2 * hd]


def _na_kernel(q_ref, k_ref, v_ref, bias_ref, _, o_ref, *, ctx_len, grid_w, qrows, krows, n_rows, scale,
               hd, heads):
    t = pl.program_id(2)
    ks = jnp.clip(qrows * t - qrows, 0, n_rows - krows)
    start = pl.multiple_of(ctx_len + ks * grid_w, 16)
    nk = krows * grid_w
    for h in range(heads):
        cols = slice(h * hd, (h + 1) * hd)
        q = (q_ref[:, cols].astype(F32) * scale).astype(BF16)
        k_win = k_ref[pl.ds(start, nk), cols]
        v_win = v_ref[pl.ds(start, nk), cols]
        k_ctx = k_ref[0:ctx_len, cols]
        v_ctx = v_ref[0:ctx_len, cols]
        o = _attend_chunks(q, [(k_win, v_win, bias_ref[h]), (k_ctx, v_ctx, None)])
        o_ref[:, cols] = o.astype(o_ref.dtype)


def _na_bias(cfg, rpb):
    W = cfg.grid_w
    n_rows = _exact_div(cfg.seq, W)
    qr, kr = cfg.na_qrows, cfg.na_qrows + cfg.na_win_h
    n_t = _exact_div(n_rows, qr)
    wh, ww = cfg.na_win_h, cfg.na_win_w
    roffs, valids = [], []
    for t in (0, 1, n_t - 1):
        ks = int(np.clip(qr * t - qr, 0, n_rows - kr))
        r = qr * t + np.arange(qr)
        rs = np.clip(r - wh // 2, 0, n_rows - wh)
        kabs = ks + np.arange(kr)
        vrow = (kabs[None, :] >= rs[:, None]) & (kabs[None, :] < rs[:, None] + wh)
        roff = np.clip(kabs[None, :] - r[:, None] + (wh - 1), 0, 2 * wh - 2)
        roffs.append(roff)
        valids.append(vrow)
    roff = np.stack(roffs)
    vrow = np.stack(valids)
    cols = np.arange(W)
    cs = np.clip(cols - ww // 2, 0, W - ww)
    vcol = (cols[None, :] >= cs[:, None]) & (cols[None, :] < cs[:, None] + ww)
    coff = np.clip(cols[None, :] - cols[:, None], -(ww - 1), ww - 1) + (ww - 1)
    n_ro, n_co = 2 * wh - 1, 2 * ww - 1
    onehot = (np.arange(n_co)[:, None] == coff.reshape(1, W * W)).astype(np.float32)
    H = rpb.shape[0]
    colb = jnp.dot(rpb.astype(F32).reshape(H * n_ro, n_co), jnp.asarray(onehot),
                   precision=lax.Precision.HIGHEST).reshape(H, n_ro, W, W)
    colb = jnp.where(vcol[None, None], colb, cfg.neg_inf)
    masked = jnp.full((H, W, W), cfg.neg_inf, F32)
    variants = []
    for v in range(3):
        rows = []
        for a in range(qr):
            tiles = [colb[:, int(roff[v, a, b])] if vrow[v, a, b] else masked for b in range(kr)]
            rows.append(jnp.concatenate(tiles, axis=-1))
        variants.append(jnp.concatenate(rows, axis=1))
    return jnp.stack(variants, axis=1)


def _na_attention(cfg, p, bias, offs):
    M = p.shape[0]
    W, hd = cfg.grid_w, cfg.head_dim
    RB = cfg.ctx_len + cfg.seq
    n_rows = _exact_div(cfg.seq, W)
    qr, kr = cfg.na_qrows, cfg.na_qrows + cfg.na_win_h
    n_t = _exact_div(n_rows, qr)
    qb = qr * W
    q0 = _exact_div(cfg.ctx_len, qb)
    tiles_b = _exact_div(RB, qb)
    hps = cfg.na_hps
    hw = hps * hd
    cq, ck, cv = (_exact_div(offs[i], hw) for i in range(3))

    def q_map(b, h, t):
        return (b * tiles_b + q0 + t, cq + h)

    def var_map(b, h, t):
        return (h, jnp.where(t == 0, 0, jnp.where(t == n_t - 1, 2, 1)), 0, 0)

    kern = functools.partial(_na_kernel, ctx_len=cfg.ctx_len, grid_w=W, qrows=qr, krows=kr,
                             n_rows=n_rows, scale=float(hd) ** -0.5, hd=hd, heads=hps)
    vmem = hps * (4 * RB * hd * 2 + 2 * qb * kr * W * 4 + 6 * qb * (kr * W + cfg.ctx_len) * 4)
    return pl.pallas_call(
        kern,
        out_shape=jax.ShapeDtypeStruct((M, cfg.na_heads * hd), BF16),
        grid=(cfg.batch, _exact_div(cfg.na_heads, hps), n_t),
        in_specs=[pl.BlockSpec((qb, hw), q_map),
                  pl.BlockSpec((RB, hw), lambda b, h, t: (b, ck + h)),
                  pl.BlockSpec((RB, hw), lambda b, h, t: (b, cv + h)),
                  pl.BlockSpec((hps, None, qb, kr * W), var_map),
                  pl.BlockSpec(memory_space=pl.ANY)],
        out_specs=pl.BlockSpec((qb, hw), lambda b, h, t: (b * tiles_b + q0 + t, h)),
        input_output_aliases={4: 0},
        compiler_params=_params(vmem, 3),
        name="na_attention",
    )(p, p, p, bias, jnp.zeros((M, cfg.na_heads * hd), BF16))


def _gq_kernel(q_ref, k_ref, v_ref, qg_ref, kg_ref, cos_ref, sin_ref, _, o_ref, kt_ref, vx_ref, qn_ref, *,
               l, groups, hd, tq, kc, ctx_len, seq, eps, scale):
    qi = pl.program_id(2)

    @pl.when(qi == 0)
    def _():
        cos = cos_ref[...]
        sin = sin_ref[...]
        kn = _rms_rope(k_ref[...].astype(F32), kg_ref[l:l + 1, :], cos, sin, eps)
        kt_ref[...] = kn.T.astype(BF16)
        vx_ref[:, 0:hd] = v_ref[...]
        vx_ref[:, hd:2 * hd] = jnp.ones((v_ref.shape[0], hd), BF16)
        gain = qg_ref[l:l + 1, :] * (scale * LOG2_E)
        for g in range(groups):
            qg = _rms_rope(q_ref[:, g * hd:(g + 1) * hd].astype(F32), gain, cos, sin, eps)
            qn_ref[g] = qg.astype(BF16)

    row0 = pl.multiple_of(ctx_len + qi * tq, 16)
    q = jnp.concatenate([qn_ref[g, pl.ds(row0, tq), :] for g in range(groups)], axis=0)
    bounds = [(0, ctx_len)] + [(ctx_len + c * kc, kc) for c in range(seq // kc)]
    o = _attend_chunks_base2(q, kt_ref, vx_ref, bounds, hd)
    for g in range(groups):
        o_ref[:, g * hd:(g + 1) * hd] = o[g * tq:(g + 1) * tq, :].astype(o_ref.dtype)


def _rope_tables(cfg):
    hd = cfg.head_dim
    q = hd // 4
    t = np.arange(cfg.seq)
    inv = cfg.rope_theta ** (-np.arange(q, dtype=np.float64) / q)
    ang_r = (t // cfg.grid_w)[:, None] * inv[None, :]
    ang_c = (t % cfg.grid_w)[:, None] * inv[None, :]
    cos = np.concatenate([np.cos(ang_r), np.cos(ang_r), np.cos(ang_c), np.cos(ang_c)], axis=1)
    sin = np.concatenate([-np.sin(ang_r), np.sin(ang_r), -np.sin(ang_c), np.sin(ang_c)], axis=1)
    cos = np.concatenate([np.ones((cfg.ctx_len, hd)), cos], axis=0)
    sin = np.concatenate([np.zeros((cfg.ctx_len, hd)), sin], axis=0)
    return jnp.asarray(cos, F32), jnp.asarray(sin, F32)


def _gq_attention(cfg, p, q_gain, k_gain, cos, sin, l, offs):
    M = p.shape[0]
    hd = cfg.head_dim
    RB = cfg.ctx_len + cfg.seq
    G = _exact_div(cfg.gq_heads, cfg.gq_kv)
    tq = cfg.gq_tq
    nq = _exact_div(cfg.seq, tq)
    q0 = _exact_div(cfg.ctx_len, tq)
    tiles_b = _exact_div(RB, tq)
    cq = _exact_div(offs[3], G * hd)
    ck = _exact_div(offs[4], hd)
    cv = _exact_div(offs[5], hd)
    kc = cfg.gq_kc
    _exact_div(cfg.seq, kc)
    kern = functools.partial(_gq_kernel, l=l, groups=G, hd=hd, tq=tq, kc=kc, ctx_len=cfg.ctx_len,
                             seq=cfg.seq, eps=cfg.eps, scale=float(hd) ** -0.5)
    vmem = (4 * RB * hd * 2 + 2 * RB * G * hd * 2 + (3 + G) * RB * hd * 2 + 4 * RB * hd * 4
            + 6 * RB * hd * 4 + 2 * tq * G * hd * 2 + 6 * G * tq * kc * 4)
    return pl.pallas_call(
        kern,
        out_shape=jax.ShapeDtypeStruct((M, cfg.gq_heads * hd), BF16),
        grid=(cfg.batch, cfg.gq_kv, nq),
        in_specs=[pl.BlockSpec((RB, G * hd), lambda b, k, i: (b, cq + k)),
                  pl.BlockSpec((RB, hd), lambda b, k, i: (b, ck + k)),
                  pl.BlockSpec((RB, hd), lambda b, k, i: (b, cv + k)),
                  pl.BlockSpec(q_gain.shape, lambda b, k, i: (0, 0)),
                  pl.BlockSpec(k_gain.shape, lambda b, k, i: (0, 0)),
                  pl.BlockSpec((RB, hd), lambda b, k, i: (0, 0)),
                  pl.BlockSpec((RB, hd), lambda b, k, i: (0, 0)),
                  pl.BlockSpec(memory_space=pl.ANY)],
        out_specs=pl.BlockSpec((tq, G * hd), lambda b, k, i: (b * tiles_b + q0 + i, k)),
        scratch_shapes=[pltpu.VMEM((hd, RB), BF16), pltpu.VMEM((RB, 2 * hd), BF16),
                        pltpu.VMEM((G, RB, hd), BF16)],
        input_output_aliases={7: 0},
        compiler_params=_params(vmem, 3),
        name="gq_attention",
    )(p, p, p, q_gain, k_gain, cos, sin, jnp.zeros((M, cfg.gq_heads * hd), BF16))


def _ctx_attn_kernel(*refs, l, groups, hd, norm, eps, scale):
    if norm:
        q_ref, k_ref, v_ref, qg_ref, kg_ref, _, o_ref = refs
    else:
        q_ref, k_ref, v_ref, _, o_ref = refs
    k = k_ref[...]
    if norm:
        k = _rms_rope(k.astype(F32), kg_ref[l:l + 1, :], None, None, eps).astype(BF16)
    n = q_ref.shape[0]
    qs = []
    for g in range(groups):
        qg = q_ref[:, g * hd:(g + 1) * hd].astype(F32)
        if norm:
            qg = _rms_rope(qg, qg_ref[l:l + 1, :] * scale, None, None, eps)
        else:
            qg = qg * scale
        qs.append(qg.astype(BF16))
    q = jnp.concatenate(qs, axis=0)
    o = _attend_chunks(q, [(k, v_ref[...], None)])
    for g in range(groups):
        o_ref[:, g * hd:(g + 1) * hd] = o[g * n:(g + 1) * n, :].astype(o_ref.dtype)


def _ctx_attention(cfg, p, y_prev, n_kv, groups, col_q, col_k, col_v, l, gains=None):
    hd = cfg.head_dim
    C = cfg.ctx_len
    tiles_b = _exact_div(cfg.ctx_len + cfg.seq, C)
    cq = _exact_div(col_q, groups * hd)
    ck = _exact_div(col_k, hd)
    cv = _exact_div(col_v, hd)
    norm = gains is not None
    in_specs = [pl.BlockSpec((C, groups * hd), lambda b, k: (b * tiles_b, cq + k)),
                pl.BlockSpec((C, hd), lambda b, k: (b * tiles_b, ck + k)),
                pl.BlockSpec((C, hd), lambda b, k: (b * tiles_b, cv + k))]
    args = [p, p, p]
    if norm:
        in_specs += [pl.BlockSpec(gains[0].shape, lambda b, k: (0, 0)),
                     pl.BlockSpec(gains[1].shape, lambda b, k: (0, 0))]
        args += list(gains)
    in_specs.append(pl.BlockSpec(memory_space=pl.ANY))
    args.append(y_prev)
    kern = functools.partial(_ctx_attn_kernel, l=l, groups=groups, hd=hd, norm=norm, eps=cfg.eps,
                             scale=float(hd) ** -0.5)
    return pl.pallas_call(
        kern,
        out_shape=jax.ShapeDtypeStruct(y_prev.shape, y_prev.dtype),
        grid=(cfg.batch, n_kv),
        in_specs=in_specs,
        out_specs=pl.BlockSpec((C, groups * hd), lambda b, k: (b * tiles_b, k)),
        input_output_aliases={len(args) - 1: 0},
        compiler_params=_params(8 << 20, 2),
        name="ctx_attention",
    )(*args)


def _tile_scan(a, b, reverse):
    n = a.shape[0]
    rows = lax.broadcasted_iota(jnp.int32, a.shape, 0)
    d = 1
    while d < n:
        shift = n - d if reverse else d
        keep = rows < n - d if reverse else rows >= d
        a_prev = jnp.where(keep, pltpu.roll(a, shift, axis=0), 1.0)
        b_prev = jnp.where(keep, pltpu.roll(b, shift, axis=0), 0.0)
        b = a * b_prev + b
        a = a * a_prev
        d *= 2
    return a, b


def _lru_kernel(x_ref, gate_ref, cw_ref, cb_ref, wr_ref, br_ref, wi_ref, bi_ref, lam_ref, o_ref,
                xpad_ref, af_ref, bf_ref, ab_ref, bb_ref, *, l, ctx_len, seq, chunk, lru_c, conv_w):
    cwid = x_ref.shape[1]
    nblk = cwid // V7X_LANES
    pad = V7X_SUBLANES
    ctx_base = pad
    lat_base = ctx_base + ctx_len + pad
    zeros = jnp.zeros((pad, cwid), F32)
    xpad_ref[0:pad, :] = zeros
    xpad_ref[ctx_base + ctx_len:lat_base, :] = zeros
    xpad_ref[lat_base + seq:lat_base + seq + pad, :] = zeros
    xpad_ref[ctx_base:ctx_base + ctx_len, :] = x_ref[0:ctx_len, :].astype(F32)
    xpad_ref[lat_base:lat_base + seq, :] = x_ref[ctx_len:ctx_len + seq, :].astype(F32)

    taps = cw_ref[...]
    cbias = cb_ref[l:l + 1, :]
    lam = lam_ref[...]
    nl = -lam
    softplus = jnp.maximum(nl, 0.0) + jnp.log1p(jnp.exp(-jnp.abs(nl)))
    a_refs = (af_ref, ab_ref)
    b_refs = (bf_ref, bb_ref)

    def coeffs(src_base, dst_row, rows):
        xc = cbias
        for tap in range(conv_w):
            xc = xc + taps[tap:tap + 1, :] * xpad_ref[pl.ds(src_base + tap - conv_w // 2, rows), :]
        xcb = xc.astype(BF16)
        for d in range(2):
            rs, is_ = [], []
            for j in range(nblk):
                xj = xcb[:, j * V7X_LANES:(j + 1) * V7X_LANES]
                rs.append(jnp.dot(xj, wr_ref[d, j].astype(BF16), preferred_element_type=F32))
                is_.append(jnp.dot(xj, wi_ref[d, j].astype(BF16), preferred_element_type=F32))
            gr = jax.nn.sigmoid(jnp.concatenate(rs, axis=1) + br_ref[d:d + 1, :])
            gi = jax.nn.sigmoid(jnp.concatenate(is_, axis=1) + bi_ref[d:d + 1, :])
            log_a = -lru_c * gr * softplus[d:d + 1, :]
            a = jnp.exp(log_a)
            a_refs[d][pl.ds(dst_row, rows), :] = a
            one_minus_a2 = -jnp.tanh(log_a) * (1.0 + a * a)
            b_refs[d][pl.ds(dst_row, rows), :] = jnp.sqrt(one_minus_a2) * (gi * xc)

    coeffs(ctx_base, 0, ctx_len)
    for c in range(seq // chunk):
        coeffs(lat_base + c * chunk, ctx_len + c * chunk, chunk)

    nsub = V7X_SUBLANES

    def scan_seg(first, n, h):
        def body(i, hs):
            hf, hb = hs
            rf = pl.multiple_of(first + nsub * i, nsub)
            rb = pl.multiple_of(first + n - nsub - nsub * i, nsub)
            ca, cb = _tile_scan(af_ref[pl.ds(rf, nsub), :], bf_ref[pl.ds(rf, nsub), :], False)
            tile_f = ca * hf + cb
            bf_ref[pl.ds(rf, nsub), :] = tile_f
            ca, cb = _tile_scan(ab_ref[pl.ds(rb, nsub), :], bb_ref[pl.ds(rb, nsub), :], True)
            tile_b = ca * hb + cb
            bb_ref[pl.ds(rb, nsub), :] = tile_b
            return tile_f[nsub - 1:nsub, :], tile_b[0:1, :]
        return lax.fori_loop(0, n // nsub, body, h, unroll=2)

    h0 = jnp.zeros((1, cwid), F32)
    hs = scan_seg(0, ctx_len, (h0, h0))
    scan_seg(ctx_len, seq, hs)

    def out_body(c, carry):
        r0 = pl.multiple_of(c * ctx_len, ctx_len)
        h = bf_ref[pl.ds(r0, ctx_len), :] + bb_ref[pl.ds(r0, ctx_len), :]
        g = gate_ref[pl.ds(r0, ctx_len), :].astype(F32)
        o_ref[pl.ds(r0, ctx_len), :] = (h * jax.nn.gelu(g)).astype(o_ref.dtype)
        return carry

    lax.fori_loop(0, (ctx_len + seq) // ctx_len, out_body, 0)


def _lru_mixer(cfg, p, conv_w, conv_b, w_r, b_r, w_i, b_i, lam, l, offs):
    M = p.shape[0]
    RB = cfg.ctx_len + cfg.seq
    cw = cfg.lru_cw
    ng = _exact_div(cfg.lru_width, cw)
    blk = _exact_div(cfg.lru_width, cfg.lru_blocks)
    assert blk == V7X_LANES
    nb = _exact_div(cw, blk)
    cx = _exact_div(offs[6], cw)
    cg = _exact_div(offs[7], cw)
    _exact_div(cfg.seq, cfg.lru_chunk)
    _exact_div(RB, cfg.ctx_len)
    kern = functools.partial(_lru_kernel, l=l, ctx_len=cfg.ctx_len, seq=cfg.seq, chunk=cfg.lru_chunk,
                             lru_c=cfg.lru_c, conv_w=cfg.conv_w)
    vmem = 5 * (RB + 32) * cw * 4 + 3 * 2 * RB * cw * 2 + 12 * cfg.lru_chunk * cw * 4 + (4 << 20)
    return pl.pallas_call(
        kern,
        out_shape=jax.ShapeDtypeStruct((M, cfg.lru_width), BF16),
        grid=(cfg.batch, ng),
        in_specs=[pl.BlockSpec((RB, cw), lambda b, g: (b, cx + g)),
                  pl.BlockSpec((RB, cw), lambda b, g: (b, cg + g)),
                  pl.BlockSpec((None, cfg.conv_w, cw), lambda b, g: (l, 0, g)),
                  pl.BlockSpec((conv_b.shape[0], cw), lambda b, g: (0, g)),
                  pl.BlockSpec((None, 2, nb, blk, blk), lambda b, g: (l, 0, g, 0, 0)),
                  pl.BlockSpec((None, 2, cw), lambda b, g: (l, 0, g)),
                  pl.BlockSpec((None, 2, nb, blk, blk), lambda b, g: (l, 0, g, 0, 0)),
                  pl.BlockSpec((None, 2, cw), lambda b, g: (l, 0, g)),
                  pl.BlockSpec((None, 2, cw), lambda b, g: (l, 0, g))],
        out_specs=pl.BlockSpec((RB, cw), lambda b, g: (b, g)),
        scratch_shapes=[pltpu.VMEM((RB + 3 * V7X_SUBLANES, cw), F32)] + [pltpu.VMEM((RB, cw), F32)] * 4,
        compiler_params=_params(vmem, 2),
        name="rglru_mixer",
    )(p, p, conv_w, conv_b, w_r, b_r, w_i, b_i, lam)


def _merge_kernel(ya_ref, yb_ref, yc_ref, wa_ref, wb_ref, wc_ref, g0_ref, g1_ref, g2_ref, o_ref,
                  wab_ref, wbb_ref, wcb_ref):
    @pl.when(pl.program_id(1) == 0)
    def _():
        wab_ref[...] = wa_ref[...].astype(BF16)
        wbb_ref[...] = wb_ref[...].astype(BF16)
        wcb_ref[...] = wc_ref[...].astype(BF16)

    acc = jax.nn.sigmoid(g0_ref[...].astype(F32)) * jnp.dot(ya_ref[...], wab_ref[...], preferred_element_type=F32)
    acc += jax.nn.sigmoid(g1_ref[...].astype(F32)) * jnp.dot(yb_ref[...], wbb_ref[...], preferred_element_type=F32)
    acc += jax.nn.sigmoid(g2_ref[...].astype(F32)) * jnp.dot(yc_ref[...], wcb_ref[...], preferred_element_type=F32)
    o_ref[...] = acc.astype(o_ref.dtype)


def _merge(cfg, ya, yb, yc, p, w_na, w_gq, w_lru, l, offs):
    M = ya.shape[0]
    D = cfg.d_model
    bm, bn = cfg.merge_bm, cfg.merge_bn
    nj = _exact_div(D, bn)
    gcol = _exact_div(offs[8], bn)
    ks = (ya.shape[1], yb.shape[1], yc.shape[1])
    ksum = sum(ks)
    vmem = 2 * bm * ksum * 2 + 2 * ksum * bn * 4 + ksum * bn * 2 + 3 * 2 * bm * bn * 2 + 2 * bm * bn * 2 + 4 * bm * bn * 4
    x_specs = [pl.BlockSpec((bm, k), lambda j, i: (i, 0)) for k in ks]
    w_specs = [pl.BlockSpec((None, k, bn), lambda j, i: (l, 0, j)) for k in ks]
    g_specs = [pl.BlockSpec((bm, bn), functools.partial(lambda j, i, s: (i, gcol + s * nj + j), s=s))
               for s in range(3)]
    return pl.pallas_call(
        _merge_kernel,
        out_shape=jax.ShapeDtypeStruct((M, D), BF16),
        grid=(nj, _exact_div(M, bm)),
        in_specs=x_specs + w_specs + g_specs,
        out_specs=pl.BlockSpec((bm, bn), lambda j, i: (i, j)),
        scratch_shapes=[pltpu.VMEM((k, bn), BF16) for k in ks],
        compiler_params=_params(vmem, 2),
        name="branch_merge",
    )(ya, yb, yc, w_na, w_gq, w_lru, p, p, p)


def _forward(cfg, x, c, ctx, c_ctx, w_ada, b_ada, w_in, na_rpb, gq_q_gain, gq_k_gain, lru_conv_w,
             lru_conv_b, lru_w_r, lru_b_r, lru_w_i, lru_b_i, lru_lambda, w_br_na, w_br_gq, w_br_lru,
             w_o, ln1_g, ln1_b, ln2_g, ln2_b, ffn_w1, ffn_w3, ffn_w2, moe_router, moe_w1, moe_w3,
             moe_w2):
    B, D = cfg.batch, cfg.d_model
    offs, widths = _splits(cfg)
    in_cols = offs[-1] + widths[-1]
    bm, bn = cfg.mm_bm, cfg.mm_bn

    cond = jnp.concatenate([c, c_ctx[None, :], jnp.zeros((V7X_SUBLANES - B - 1, D), F32)], axis=0)
    b_ada3 = b_ada.reshape(cfg.depth, 1, 6 * D)
    mods = [_ada_table(cond, w_ada, b_ada3, l).reshape(V7X_SUBLANES, 6, D) for l in range(cfg.depth)]
    cos, sin = _rope_tables(cfg)

    x_all, u = _modulate(cfg, x, ctx, mods[0], 0, 1)
    for l in range(cfg.depth):
        need_ctx = l < cfg.depth - 1
        mod = mods[l]
        p = _matmul(u, [w_in], (l,), bm=bm, bn=bn, out_dtype=BF16, name="in_proj")
        assert p.shape[1] == in_cols
        bias = _na_bias(cfg, na_rpb[l])
        ya = _na_attention(cfg, p, bias, offs)
        yb = _gq_attention(cfg, p, gq_q_gain, gq_k_gain, cos, sin, l, offs)
        if need_ctx:
            ya = _ctx_attention(cfg, p, ya, cfg.na_heads, 1, offs[0], offs[1], offs[2], l)
            yb = _ctx_attention(cfg, p, yb, cfg.gq_kv, cfg.gq_heads // cfg.gq_kv, offs[3], offs[4],
                                offs[5], l, gains=(gq_q_gain, gq_k_gain))
        yc = _lru_mixer(cfg, p, lru_conv_w, lru_conv_b, lru_w_r, lru_b_r, lru_w_i, lru_b_i,
                        lru_lambda, l, offs)
        merged = _merge(cfg, ya, yb, yc, p, w_br_na, w_br_gq, w_br_lru, l, offs)
        y = _matmul(merged, [w_o], (l,), bm=bm, bn=bn, out_dtype=BF16, name="out_proj")

        dense = l % 2 == 0
        router = None
        if not dense:
            router = jnp.pad(moe_router[l // 2], ((0, 0), (0, V7X_LANES - cfg.n_experts)))
        res = _layer_norm(cfg, x_all, y, mod, 2, ln1_g, ln1_b, l, mod_next=mod, next_rows=(3, 4),
                          router=router, u_dtype=BF16 if dense else F32)
        x_all, u2 = res[0], res[1]
        if dense:
            h = _matmul(u2, [ffn_w1, ffn_w3], (l // 2,), bm=bm, bn=cfg.glu_bn, out_dtype=BF16,
                        glu=True, name="ffn_up")
            w2 = ffn_w2[l // 2].astype(BF16)[None]
            f = _matmul(h, [w2], (0,), bm=cfg.down_bm, bn=bn, out_dtype=BF16, name="ffn_down")
        else:
            top = res[2]
            if not need_ctx:
                top = top.reshape(B, cfg.ctx_len + cfg.seq, V7X_LANES)[:, cfg.ctx_len:, :]
                top = top.reshape(B * cfg.seq, V7X_LANES)
            rt = _route(cfg, top, latent_only=not need_ctx)
            xs = _gather_rows(u2, rt["src"], cfg.gather_rows, BF16)
            h = _grouped_matmul(xs, [moe_w1, moe_w3], l // 2, rt["te"], rt["n_used"], bm=cfg.moe_bm,
                                bn=cfg.moe_up_bn, out_dtype=BF16, glu=True, name="moe_up")
            ys = _grouped_matmul(h, [moe_w2], l // 2, rt["te"], rt["n_used"], bm=cfg.moe_bm,
                                 bn=cfg.moe_down_bn, out_dtype=F32, name="moe_down")
            f = _combine(ys, rt["pos1"], rt["pos2"], top, BF16)
        f_dense = (not dense) and (not need_ctx)
        if l + 1 < cfg.depth:
            assert not f_dense
            x_all, u = _layer_norm(cfg, x_all, f, mod, 5, ln2_g, ln2_b, l, mod_next=mods[l + 1],
                                   next_rows=(0, 1))
        else:
            (x_lat,) = _layer_norm(cfg, x_all, f, mod, 5, ln2_g, ln2_b, l, latent_only=True,
                                   y_dense=f_dense)
    return x_lat.reshape(B, cfg.seq, D)


def kernel(x, c, ctx, c_ctx, w_ada, b_ada, w_in, na_rpb, gq_q_gain, gq_k_gain, lru_conv_w, lru_conv_b, lru_w_r, lru_b_r, lru_w_i, lru_b_i, lru_lambda, w_br_na, w_br_gq, w_br_lru, w_o, ln1_g, ln1_b, ln2_g, ln2_b, ffn_w1, ffn_w3, ffn_w2, moe_router, moe_w1, moe_w3, moe_w2):
    return _forward(FULL_CFG, x, c, ctx, c_ctx, w_ada, b_ada, w_in, na_rpb, gq_q_gain, gq_k_gain,
                    lru_conv_w, lru_conv_b, lru_w_r, lru_b_r, lru_w_i, lru_b_i, lru_lambda, w_br_na,
                    w_br_gq, w_br_lru, w_o, ln1_g, ln1_b, ln2_g, ln2_b, ffn_w1, ffn_w3, ffn_w2,
                    moe_router, moe_w1, moe_w3, moe_w2)
```
